```python
import math
import jax, jax.numpy as jnp
from jax import lax
import numpy as np

D_MODEL = 1024
BATCH = 4
SEQ = 4096
DEPTH = 2
DEC_BATCH = 128
DEC_SEQ = 4
PAST_LEN = 2048
PAGE_SIZE = 128

HEAD_DIM = 64
ATTN_SCALE = HEAD_DIM ** -0.5
ROPE_THETA = 10000.0
Q_BLOCK = 128
DIFF_HEADS = 4
NSA_HEADS = 4
NSA_KV_GROUPS = 1
NSA_HPG = NSA_HEADS // NSA_KV_GROUPS
CMP_LEN = 32
CMP_STRIDE = 16
CMP_HIDDEN = 128
SLC_LEN = 64
SLC_TOPN = 16
WINDOW = 512
MOBA_HEADS = 4
MOBA_BLOCK = 256
MOBA_TOPK = 3
N_EXPERTS = 64
N_EXPERT_GROUPS = 8
TOPK_GROUPS = 4
TOP_K = 6
D_EXPERT = 128
D_SHARED = 256
ROUTED_SCALE = 2.5
LN_EPS = 1e-5
RMS_EPS = 1e-5
DEEPNORM_ALPHA = (2 * DEPTH) ** 0.25
DEEPNORM_BETA = (8 * DEPTH) ** -0.25
NEG_POS = -(2 ** 30)

DA_QK = 2 * DIFF_HEADS * HEAD_DIM
DA_V = DIFF_HEADS * 2 * HEAD_DIM
NSA_Q = NSA_HEADS * HEAD_DIM
NSA_KV = NSA_KV_GROUPS * HEAD_DIM
NSA_GATE = 3 * NSA_HEADS
MB_W = MOBA_HEADS * HEAD_DIM
SPLIT_SIZES = (DA_QK, DA_QK, DA_V, NSA_Q, NSA_KV, NSA_KV, NSA_KV, NSA_KV, NSA_KV, NSA_KV,
               NSA_GATE, MB_W, MB_W, MB_W, 3 * D_MODEL)
SPLIT_POINTS = tuple(sum(SPLIT_SIZES[:i + 1]) for i in range(len(SPLIT_SIZES) - 1))
N_IN = sum(SPLIT_SIZES)
DIFF_OUT = DA_V
DIFF_ROW = DA_QK + DA_V
NSA_ROW = 4 * NSA_KV
MOBA_ROW = 2 * MB_W
WIN_ROW = 2 * NSA_KV

kernel_name = "hybrid_diff_nsa_moba_moe_decode_step"


def rope(x, pos):
    half = HEAD_DIM // 2
    inv = ROPE_THETA ** (-jnp.arange(half, dtype=jnp.float32) / half)
    ang = pos.astype(jnp.float32)[:, None] * inv[None, :]
    cos, sin = jnp.cos(ang)[:, None, :], jnp.sin(ang)[:, None, :]
    x1, x2 = x[..., :half].astype(jnp.float32), x[..., half:].astype(jnp.float32)
    return jnp.concatenate([x1 * cos - x2 * sin, x1 * sin + x2 * cos], -1).astype(x.dtype)


def masked_softmax(s, mask):
    s = jnp.where(mask, s.astype(jnp.float32), -jnp.inf)
    m = jnp.max(s, -1, keepdims=True)
    m = jnp.where(jnp.isfinite(m), m, 0.0)
    p = jnp.where(mask, jnp.exp(s - m), 0.0)
    return p / jnp.maximum(jnp.sum(p, -1, keepdims=True), 1e-30)


def layer_norm(x, g, b):
    xf = x.astype(jnp.float32)
    mu = jnp.mean(xf, -1, keepdims=True)
    var = jnp.mean(jnp.square(xf - mu), -1, keepdims=True)
    return ((xf - mu) * lax.rsqrt(var + LN_EPS) * g.astype(jnp.float32) + b.astype(jnp.float32)).astype(x.dtype)


def sweep(fn, q_pos, *q_args):
    sq = q_pos.shape[0]
    if sq <= Q_BLOCK or sq % Q_BLOCK:
        return fn(q_pos, *q_args)
    nb = sq // Q_BLOCK

    def to_blocks(a):
        return jnp.moveaxis(a.reshape(a.shape[0], nb, Q_BLOCK, *a.shape[2:]), 1, 0)

    out = lax.map(lambda args: fn(*args),
                  (q_pos.reshape(nb, Q_BLOCK),) + tuple(to_blocks(a) for a in q_args))
    out = jnp.moveaxis(out, 0, 1)
    return out.reshape(out.shape[0], sq, *out.shape[3:])


def diff_attend(q, k, v, q_pos, lam, subln_g, lambda_init):
    B, L = k.shape[0], k.shape[1]
    k_pos = jnp.arange(L, dtype=jnp.int32)

    def block(qp, qb):
        Q = qb.shape[1]
        s = jnp.einsum('bqhd,bkhd->bhqk', qb, k) * ATTN_SCALE
        p = masked_softmax(s, (k_pos[None, :] <= qp[:, None])[None, None])
        p = p.reshape(B, DIFF_HEADS, 2, Q, L)
        a = p[:, :, 0] - lam * p[:, :, 1]
        o = jnp.einsum('bhqk,bkhe->bqhe', a.astype(v.dtype), v).astype(jnp.float32)
        o = o * lax.rsqrt(jnp.mean(o * o, -1, keepdims=True) + RMS_EPS) * subln_g.astype(jnp.float32)
        return (o * (1.0 - lambda_init)).astype(qb.dtype).reshape(B, Q, DIFF_OUT)

    return sweep(block, q_pos, q)


def nsa_compress(kv, pos_emb, w1, w2):
    B, L, G, D = kv.shape
    n_cmp = (L - CMP_LEN) // CMP_STRIDE + 1
    idx = jnp.arange(n_cmp)[:, None] * CMP_STRIDE + jnp.arange(CMP_LEN)[None, :]
    blocks = kv[:, idx] + pos_emb[:, None, :]
    flat = jnp.swapaxes(blocks, 2, 3).reshape(B, n_cmp, G, CMP_LEN * D)
    return jax.nn.gelu(flat @ w1) @ w2


def nsa_cmp_slc(q, q_r, ck, cv, sk, sv, q_pos):
    B = q.shape[0]
    G, D = NSA_KV_GROUPS, HEAD_DIM
    n_cmp = ck.shape[1]
    n_slc = sk.shape[1] // SLC_LEN
    top_n = min(SLC_TOPN, n_slc)
    cmp_start = jnp.arange(n_cmp) * CMP_STRIDE
    cmp_end = cmp_start + CMP_LEN - 1
    slc_start = jnp.arange(n_slc) * SLC_LEN
    overlap = ((cmp_start[:, None] <= slc_start[None, :] + SLC_LEN - 1)
               & (cmp_end[:, None] >= slc_start[None, :])).astype(jnp.float32)
    skb = jnp.transpose(sk.reshape(B, n_slc, SLC_LEN, G, D), (0, 3, 1, 2, 4))
    svb = jnp.transpose(sv.reshape(B, n_slc, SLC_LEN, G, D), (0, 3, 1, 2, 4))
    bi = jnp.arange(B)[:, None, None, None]
    gi = jnp.arange(G)[None, :, None, None]
    blk_ids = jnp.arange(n_slc)

    def block(qp, qb, qrb):
        Q = qb.shape[1]
        qg = qb.reshape(B, Q, G, NSA_HPG, D)
        qrg = qrb.reshape(B, Q, G, NSA_HPG, D)
        s = jnp.einsum('bqgjd,bngd->bgjqn', qg, ck) * ATTN_SCALE
        p = masked_softmax(s, (cmp_end[None, :] <= qp[:, None])[None, None, None])
        o_cmp = jnp.einsum('bgjqn,bngd->bqgjd', p.astype(cv.dtype), cv).reshape(B, Q, NSA_HEADS, D)
        imp = jnp.einsum('bgjqn,nm->bgqm', p, overlap)
        cur = qp // SLC_LEN
        valid = blk_ids[None, :] <= cur[:, None]
        forced = (blk_ids[None, :] == 0) | (blk_ids[None, :] == cur[:, None]) | (blk_ids[None, :] == cur[:, None] - 1)
        score = jnp.where(forced, jnp.inf, jnp.where(valid, imp, -jnp.inf))
        _, sel = lax.top_k(score, top_n)
        sel_ok = sel <= cur[:, None]
        kg = skb[bi, gi, sel]
        vg = svb[bi, gi, sel]
        kpos = sel[..., None] * SLC_LEN + jnp.arange(SLC_LEN)
        mask = sel_ok[..., None] & (kpos <= qp[:, None, None])
        nk = top_n * SLC_LEN
        s2 = jnp.einsum('bqgjd,bgqnld->bgjqnl', qrg, kg) * ATTN_SCALE
        p2 = masked_softmax(s2.reshape(B, G, NSA_HPG, Q, nk), mask.reshape(B, G, 1, Q, nk))
        o_slc = jnp.einsum('bgjqk,bgqkd->bqgjd', p2.astype(vg.dtype), vg.reshape(B, G, Q, nk, D))
        return jnp.stack([o_cmp, o_slc.reshape(B, Q, NSA_HEADS, D)], axis=3)

    return sweep(block, q_pos, q, q_r)


def nsa_window(q_r, wk, wv, q_pos, k_pos0):
    B, Lk, G, D = wk.shape
    pad = ((0, 0), (WINDOW, 0), (0, 0), (0, 0))
    wk_p, wv_p = jnp.pad(wk, pad), jnp.pad(wv, pad)
    kp = jnp.concatenate([jnp.full((WINDOW,), NEG_POS, jnp.int32),
                          k_pos0 + jnp.arange(Lk, dtype=jnp.int32)])

    def block(qp, qb):
        Q = qb.shape[1]
        start = qp[0] - k_pos0
        kb = lax.dynamic_slice_in_dim(wk_p, start, WINDOW + Q, axis=1)
        vb = lax.dynamic_slice_in_dim(wv_p, start, WINDOW + Q, axis=1)
        pb = lax.dynamic_slice_in_dim(kp, start, WINDOW + Q)
        mask = (pb[None, :] <= qp[:, None]) & (pb[None, :] > qp[:, None] - WINDOW)
        qg = qb.reshape(B, Q, G, NSA_HPG, D)
        s = jnp.einsum('bqgjd,bkgd->bgjqk', qg, kb) * ATTN_SCALE
        p = masked_softmax(s, mask[None, None, None])
        o = jnp.einsum('bgjqk,bkgd->bqgjd', p.astype(vb.dtype), vb)
        return o.reshape(B, Q, NSA_HEADS, D)

    return sweep(block, q_pos, q_r)


def moba_attend(q, k, v, q_pos):
    B, Lp, H, D = k.shape
    n_blk = Lp // MOBA_BLOCK
    kb = jnp.transpose(k.reshape(B, n_blk, MOBA_BLOCK, H, D), (0, 3, 1, 2, 4))
    vb = jnp.transpose(v.reshape(B, n_blk, MOBA_BLOCK, H, D), (0, 3, 1, 2, 4))
    k_mean = jnp.mean(kb.astype(jnp.float32), axis=3)
    n_sel = min(MOBA_TOPK, n_blk)
    bi = jnp.arange(B)[:, None, None, None]
    hi = jnp.arange(H)[None, :, None, None]
    blk_ids = jnp.arange(n_blk)

    def block(qp, qb):
        Q = qb.shape[1]
        cur = qp // MOBA_BLOCK
        gate = jnp.einsum('bqhd,bhnd->bhqn', qb.astype(jnp.float32), k_mean)
        gate = jnp.where(blk_ids[None, :] < cur[:, None], gate, -jnp.inf)
        _, sel = lax.top_k(gate, n_sel)
        sel_ok = sel < cur[:, None]
        sel = jnp.concatenate([sel, jnp.broadcast_to(cur[:, None], (B, H, Q, 1)).astype(sel.dtype)], -1)
        ok = jnp.concatenate([sel_ok, jnp.ones((B, H, Q, 1), bool)], -1)
        kg = kb[bi, hi, sel]
        vg = vb[bi, hi, sel]
        kpos = sel[..., None] * MOBA_BLOCK + jnp.arange(MOBA_BLOCK)
        mask = ok[..., None] & (kpos <= qp[:, None, None])
        nk = (n_sel + 1) * MOBA_BLOCK
        s = jnp.einsum('bqhd,bhqnld->bhqnl', qb, kg) * ATTN_SCALE
        p = masked_softmax(s.reshape(B, H, Q, nk), mask.reshape(B, H, Q, nk))
        o = jnp.einsum('bhqk,bhqkd->bqhd', p.astype(vg.dtype), vg.reshape(B, H, Q, nk, D))
        return o.reshape(B, Q, H * D)

    return sweep(block, q_pos, q)


def pad_to(a, mult):
    extra = (-a.shape[1]) % mult
    return jnp.pad(a, ((0, 0), (0, extra), (0, 0), (0, 0)))


def token_mix(x, q_pos, past_diff, past_nsa, past_moba, past_win, win_pos0,
              w_in, diff_lambda, diff_subln, cmp_pos, cmp_w1, cmp_w2,
              w_br_diff, w_br_nsa, w_br_moba, w_out, lambda_init):
    B, S, _ = x.shape
    (da_q, da_k, da_v, n_q, c_k, c_v, s_k, s_v, w_k, w_v, n_g,
     m_q, m_k, m_v, mg) = jnp.split(x @ w_in, SPLIT_POINTS, axis=-1)
    da_q = rope(da_q.reshape(B, S, 2 * DIFF_HEADS, HEAD_DIM), q_pos)
    da_k = rope(da_k.reshape(B, S, 2 * DIFF_HEADS, HEAD_DIM), q_pos).reshape(B, S, DA_QK)
    s_k = rope(s_k.reshape(B, S, NSA_KV_GROUPS, HEAD_DIM), q_pos).reshape(B, S, NSA_KV)
    w_k = rope(w_k.reshape(B, S, NSA_KV_GROUPS, HEAD_DIM), q_pos).reshape(B, S, NSA_KV)
    m_q = rope(m_q.reshape(B, S, MOBA_HEADS, HEAD_DIM), q_pos)
    m_k = rope(m_k.reshape(B, S, MOBA_HEADS, HEAD_DIM), q_pos).reshape(B, S, MB_W)
    rows_diff = jnp.concatenate([da_k, da_v], -1)
    rows_nsa = jnp.concatenate([c_k, c_v, s_k, s_v], -1)
    rows_moba = jnp.concatenate([m_k, m_v], -1)
    rows_win = jnp.concatenate([w_k, w_v], -1)
    if past_diff is None:
        full_diff, full_nsa, full_moba, full_win = rows_diff, rows_nsa, rows_moba, rows_win
    else:
        full_diff = jnp.concatenate([past_diff, rows_diff], 1)
        full_nsa = jnp.concatenate([past_nsa, rows_nsa], 1)
        full_moba = jnp.concatenate([past_moba, rows_moba], 1)
        full_win = jnp.concatenate([past_win, rows_win], 1)
    L = full_diff.shape[1]

    fk, fv = jnp.split(full_diff, [DA_QK], -1)
    lp = diff_lambda.astype(jnp.float32)
    lam = jnp.exp(jnp.sum(lp[0] * lp[1])) - jnp.exp(jnp.sum(lp[2] * lp[3])) + lambda_init
    o_diff = diff_attend(da_q, fk.reshape(B, L, 2 * DIFF_HEADS, HEAD_DIM),
                         fv.reshape(B, L, DIFF_HEADS, 2 * HEAD_DIM), q_pos, lam, diff_subln, lambda_init)

    ck, cv, sk, sv = [a.reshape(B, L, NSA_KV_GROUPS, HEAD_DIM) for a in
                      jnp.split(full_nsa, [NSA_KV, 2 * NSA_KV, 3 * NSA_KV], -1)]
    ck_c = nsa_compress(ck, cmp_pos[0], cmp_w1[0], cmp_w2[0])
    cv_c = nsa_compress(cv, cmp_pos[1], cmp_w1[1], cmp_w2[1])
    nq = n_q.reshape(B, S, NSA_HEADS, HEAD_DIM)
    nq_r = rope(nq, q_pos)
    o_cs = nsa_cmp_slc(nq, nq_r, ck_c, cv_c, pad_to(sk, SLC_LEN), pad_to(sv, SLC_LEN), q_pos)
    Lw = full_win.shape[1]
    wk, wv = [a.reshape(B, Lw, NSA_KV_GROUPS, HEAD_DIM) for a in jnp.split(full_win, [NSA_KV], -1)]
    o_w = nsa_window(nq_r, wk, wv, q_pos, win_pos0)
    ng = jax.nn.sigmoid(n_g.reshape(B, S, NSA_HEADS, 3).astype(jnp.float32))
    o_nsa = (ng[..., 0:1] * o_cs[:, :, :, 0] + ng[..., 1:2] * o_cs[:, :, :, 1]
             + ng[..., 2:3] * o_w).astype(x.dtype).reshape(B, S, NSA_Q)

    mk, mv = [a.reshape(B, L, MOBA_HEADS, HEAD_DIM) for a in jnp.split(full_moba, [MB_W], -1)]
    o_moba = moba_attend(m_q, pad_to(mk, MOBA_BLOCK), pad_to(mv, MOBA_BLOCK), q_pos)

    g = jax.nn.sigmoid(mg.reshape(B, S, 3, D_MODEL).astype(jnp.float32))
    merged = (g[:, :, 0] * (o_diff @ w_br_diff) + g[:, :, 1] * (o_nsa @ w_br_nsa)
              + g[:, :, 2] * (o_moba @ w_br_moba))
    y = (merged.astype(x.dtype) @ w_out).astype(x.dtype)
    n_keep = min(WINDOW, L)
    new_win = full_win[:, Lw - n_keep:]
    return y, rows_diff, rows_nsa, rows_moba, new_win


def moe(x, router_w, router_b, w_gate, w_up, w_down, ws_gate, ws_up, ws_down):
    shp = x.shape
    t = x.reshape(-1, shp[-1])
    s = jax.nn.sigmoid((t @ router_w).astype(jnp.float32))
    sb = s + router_b.astype(jnp.float32)
    grp = sb.reshape(-1, N_EXPERT_GROUPS, N_EXPERTS // N_EXPERT_GROUPS)
    grp_score = jnp.sum(lax.top_k(grp, 2)[0], -1)
    _, top_g = lax.top_k(grp_score, TOPK_GROUPS)
    gmask = jnp.sum(jax.nn.one_hot(top_g, N_EXPERT_GROUPS, dtype=jnp.float32), -2) > 0
    emask = jnp.repeat(gmask, N_EXPERTS // N_EXPERT_GROUPS, axis=-1)
    _, top_e = lax.top_k(jnp.where(emask, sb, -jnp.inf), TOP_K)
    w = jnp.take_along_axis(s, top_e, -1)
    w = w / jnp.sum(w, -1, keepdims=True) * ROUTED_SCALE
    gate = jnp.einsum('tk,tke->te', w, jax.nn.one_hot(top_e, N_EXPERTS, dtype=jnp.float32))
    h = jax.nn.silu(jnp.einsum('td,edf->tef', t, w_gate)) * jnp.einsum('td,edf->tef', t, w_up)
    y = jnp.einsum('tef,efd->td', (h * gate[:, :, None]).astype(t.dtype), w_down)
    sh = (jax.nn.silu(t @ ws_gate) * (t @ ws_up)) @ ws_down
    return (y + sh).astype(x.dtype).reshape(shp)


def gather_pages(pool, l, page_table):
    dec_b, n_pages = page_table.shape
    return pool[l, page_table].reshape(dec_b, n_pages * PAGE_SIZE, pool.shape[-1])


def setup_inputs(seed: int = 0) -> dict:
    key = jax.random.key(seed)
    ks = jax.random.split(key, 32)
    n_pages = PAST_LEN // PAGE_SIZE
    n_phys = (DEC_BATCH * n_pages * 5) // 4
    w_buf = min(WINDOW, PAST_LEN)

    def nrm(k, shape, scale):
        return jax.random.normal(k, shape, jnp.float32) * scale

    page_table = jax.random.permutation(ks[7], n_phys)[:DEC_BATCH * n_pages].reshape(
        DEC_BATCH, n_pages).astype(jnp.int32)
    return {
        "x_prompt": nrm(ks[0], (BATCH, SEQ, D_MODEL), 1.0),
        "x_sample": nrm(ks[1], (DEC_BATCH, DEC_SEQ, D_MODEL), 1.0),
        "cache_diff": nrm(ks[2], (DEPTH, n_phys, PAGE_SIZE, DIFF_ROW), 1.0),
        "cache_nsa": nrm(ks[3], (DEPTH, n_phys, PAGE_SIZE, NSA_ROW), 1.0),
        "cache_moba": nrm(ks[4], (DEPTH, n_phys, PAGE_SIZE, MOBA_ROW), 1.0),
        "state_nsa_win": nrm(ks[5], (DEPTH, DEC_BATCH, w_buf, WIN_ROW), 1.0),
        "page_table": page_table,
        "w_in": nrm(ks[8], (DEPTH, D_MODEL, N_IN), D_MODEL ** -0.5),
        "diff_lambda": nrm(ks[9], (DEPTH, 4, HEAD_DIM), 0.1),
        "diff_subln": 1.0 + nrm(ks[10], (DEPTH, 2 * HEAD_DIM), 0.01),
        "nsa_cmp_pos": nrm(ks[11], (DEPTH, 2, CMP_LEN, HEAD_DIM), 0.1),
        "nsa_cmp_w1": nrm(ks[12], (DEPTH, 2, CMP_LEN * HEAD_DIM, CMP_HIDDEN), (CMP_LEN * HEAD_DIM) ** -0.5),
        "nsa_cmp_w2": nrm(ks[13], (DEPTH, 2, CMP_HIDDEN, HEAD_DIM), CMP_HIDDEN ** -0.5),
        "w_br_diff": nrm(ks[14], (DEPTH, DIFF_OUT, D_MODEL), DIFF_OUT ** -0.5 * DEEPNORM_BETA),
        "w_br_nsa": nrm(ks[15], (DEPTH, NSA_Q, D_MODEL), NSA_Q ** -0.5 * DEEPNORM_BETA),
        "w_br_moba": nrm(ks[16], (DEPTH, MB_W, D_MODEL), MB_W ** -0.5 * DEEPNORM_BETA),
        "w_out": nrm(ks[17], (DEPTH, D_MODEL, D_MODEL), D_MODEL ** -0.5 * DEEPNORM_BETA),
        "ln1_g": 1.0 + nrm(ks[18], (DEPTH, D_MODEL), 0.01),
        "ln1_b": nrm(ks[19], (DEPTH, D_MODEL), 0.01),
        "ln2_g": 1.0 + nrm(ks[20], (DEPTH, D_MODEL), 0.01),
        "ln2_b": nrm(ks[21], (DEPTH, D_MODEL), 0.01),
        "router_w": nrm(ks[22], (DEPTH, D_MODEL, N_EXPERTS), D_MODEL ** -0.5),
        "router_b": nrm(ks[23], (DEPTH, N_EXPERTS), 0.01),
        "exp_w_gate": nrm(ks[24], (DEPTH, N_EXPERTS, D_MODEL, D_EXPERT), D_MODEL ** -0.5),
        "exp_w_up": nrm(ks[25], (DEPTH, N_EXPERTS, D_MODEL, D_EXPERT), D_MODEL ** -0.5),
        "exp_w_down": nrm(ks[26], (DEPTH, N_EXPERTS, D_EXPERT, D_MODEL), D_EXPERT ** -0.5 * DEEPNORM_BETA),
        "sh_w_gate": nrm(ks[27], (DEPTH, D_MODEL, D_SHARED), D_MODEL ** -0.5),
        "sh_w_up": nrm(ks[28], (DEPTH, D_MODEL, D_SHARED), D_MODEL ** -0.5),
        "sh_w_down": nrm(ks[29], (DEPTH, D_SHARED, D_MODEL), D_SHARED ** -0.5 * DEEPNORM_BETA),
    }


def reference(x_prompt, x_sample, cache_diff, cache_nsa, cache_moba, state_nsa_win, page_table,
              w_in, diff_lambda, diff_subln, nsa_cmp_pos, nsa_cmp_w1, nsa_cmp_w2,
              w_br_diff, w_br_nsa, w_br_moba, w_out, ln1_g, ln1_b, ln2_g, ln2_b,
              router_w, router_b, exp_w_gate, exp_w_up, exp_w_down, sh_w_gate, sh_w_up, sh_w_down):
    past_len = page_table.shape[1] * PAGE_SIZE
    pos_p = jnp.arange(x_prompt.shape[1], dtype=jnp.int32)
    pos_s = past_len + jnp.arange(x_sample.shape[1], dtype=jnp.int32)
    win_pos0 = past_len - state_nsa_win.shape[2]
    xp, xs = x_prompt, x_sample
    st_p = [[], [], [], []]
    st_s = [[], [], [], []]
    for l in range(DEPTH):
        lambda_init = 0.8 - 0.6 * math.exp(-0.3 * l)
        mix_w = (w_in[l], diff_lambda[l], diff_subln[l], nsa_cmp_pos[l], nsa_cmp_w1[l], nsa_cmp_w2[l],
                 w_br_diff[l], w_br_nsa[l], w_br_moba[l], w_out[l], lambda_init)
        moe_w = (router_w[l], router_b[l], exp_w_gate[l], exp_w_up[l], exp_w_down[l],
                 sh_w_gate[l], sh_w_up[l], sh_w_down[l])
        y_p, rd_p, rn_p, rm_p, wb_p = token_mix(xp, pos_p, None, None, None, None, 0, *mix_w)
        xp = layer_norm(DEEPNORM_ALPHA * xp + y_p, ln1_g[l], ln1_b[l])
        xp = layer_norm(DEEPNORM_ALPHA * xp + moe(xp, *moe_w), ln2_g[l], ln2_b[l])
        y_s, rd_s, rn_s, rm_s, wb_s = token_mix(
            xs, pos_s, gather_pages(cache_diff, l, page_table), gather_pages(cache_nsa, l, page_table),
            gather_pages(cache_moba, l, page_table), state_nsa_win[l], win_pos0, *mix_w)
        xs = layer_norm(DEEPNORM_ALPHA * xs + y_s, ln1_g[l], ln1_b[l])
        xs = layer_norm(DEEPNORM_ALPHA * xs + moe(xs, *moe_w), ln2_g[l], ln2_b[l])
        for lst, val in zip(st_p, (rd_p, rn_p, rm_p, wb_p)):
            lst.append(val)
        for lst, val in zip(st_s, (rd_s, rn_s, rm_s, wb_s)):
            lst.append(val)
    new_diff_prompt, new_nsa_prompt, new_moba_prompt, new_win_prompt = [jnp.stack(a, 0) for a in st_p]
    new_diff_sample, new_nsa_sample, new_moba_sample, new_win_sample = [jnp.stack(a, 0) for a in st_s]
    return (xp, xs, new_diff_prompt, new_diff_sample, new_nsa_prompt, new_nsa_sample,
            new_moba_prompt, new_moba_sample, new_win_prompt, new_win_sample)
```

```python
import functools
import math

import jax
import jax.numpy as jnp
from jax import lax
from jax.experimental import pallas as pl
from jax.experimental.pallas import tpu as pltpu

F32 = jnp.float32
BF16 = jnp.bfloat16

D_MODEL = 1024
DEPTH = 2
PAGE_SIZE = 128
HEAD_DIM = 64
HALF = HEAD_DIM // 2
ATTN_SCALE = HEAD_DIM ** -0.5
ROPE_THETA = 10000.0
DIFF_HEADS = 4
NSA_HEADS = 4
CMP_LEN = 32
CMP_STRIDE = 16
SLC_LEN = 64
SLC_TOPN = 16
WINDOW = 512
MOBA_HEADS = 4
MOBA_BLOCK = 256
MOBA_TOPK = 3
N_EXPERTS = 64
N_EXPERT_GROUPS = 8
TOPK_GROUPS = 4
TOP_K = 6
D_EXPERT = 128
D_SHARED = 256
ROUTED_SCALE = 2.5
LN_EPS = 1e-5
RMS_EPS = 1e-5
DEEPNORM_ALPHA = (2 * DEPTH) ** 0.25

C_DAQ, C_DAK, C_DAV, C_NQ = 0, 512, 1024, 1536
C_CK, C_CV, C_SK, C_SV, C_WK, C_WV, C_NG = 1792, 1856, 1920, 1984, 2048, 2112, 2176
N_GATE = 3 * NSA_HEADS
C_MQ_SRC = C_NG + N_GATE
C_MQ, C_MK, C_MV, C_MG = 2304, 2560, 2816, 3072
N_IN_PAD = C_MG + 3 * D_MODEL

LANES = 128
NEG = -1e30
VMEM_LIMIT = 56 * 1024 * 1024


def _nt_dot(a, b):
    return lax.dot_general(a, b, (((1,), (1,)), ((), ())), preferred_element_type=F32)


def _mm_kernel(x_ref, w_ref, o_ref):
    o_ref[...] = jnp.dot(x_ref[...].astype(BF16), w_ref[...],
                         preferred_element_type=F32).astype(o_ref.dtype)


def matmul(x, w, tm, tn, out_dtype=F32):
    m, k = x.shape
    n = w.shape[1]
    assert m % tm == 0 and n % tn == 0, (x.shape, w.shape, tm, tn)
    return pl.pallas_call(
        _mm_kernel,
        out_shape=jax.ShapeDtypeStruct((m, n), out_dtype),
        grid=(m // tm, n // tn),
        in_specs=[pl.BlockSpec((tm, k), lambda i, j: (i, 0)),
                  pl.BlockSpec((k, tn), lambda i, j: (0, j))],
        out_specs=pl.BlockSpec((tm, tn), lambda i, j: (i, j)),
        compiler_params=pltpu.CompilerParams(
            dimension_semantics=("parallel", "arbitrary"), vmem_limit_bytes=VMEM_LIMIT),
        name="matmul",
    )(x, w)


def _online_update(s, v, m_ref, l_ref, acc_ref):
    tk = s.shape[1]
    dv = acc_ref.shape[-1]
    m_prev = m_ref[...]
    m_next = jnp.maximum(m_prev, jnp.max(s, axis=-1, keepdims=True))
    alpha = jnp.exp(m_prev - m_next)
    p = jnp.exp(s - pltpu.repeat(m_next, tk // LANES, axis=1))
    l_ref[...] = alpha * l_ref[...] + jnp.sum(p, axis=-1, keepdims=True)
    m_ref[...] = m_next
    acc_ref[...] = acc_ref[...] * alpha[:, :dv] + jnp.dot(p.astype(BF16), v, preferred_element_type=F32)


def _init_state(*refs):
    for m_ref, l_ref, acc_ref in zip(refs[0::3], refs[1::3], refs[2::3]):
        m_ref[...] = jnp.full(m_ref.shape, NEG, F32)
        l_ref[...] = jnp.zeros(l_ref.shape, F32)
        acc_ref[...] = jnp.zeros(acc_ref.shape, F32)


def _normalised(l_ref, acc_ref):
    dv = acc_ref.shape[-1]
    return acc_ref[...] / jnp.maximum(l_ref[...], 1e-30)[:, :dv]


def _diff_attn_kernel(q_ref, k_ref, v_ref, par_ref, o_ref,
                      m0, l0, a0, m1, l1, a1, *, tq, tk, out_scale):
    qi = pl.program_id(2)
    q = q_ref[...]
    lane = lax.broadcasted_iota(jnp.int32, q.shape, 1)
    zero = jnp.zeros_like(q)
    qa = jnp.where(lane < HEAD_DIM, q, zero)
    qb = jnp.where(lane >= HEAD_DIM, q, zero)
    q0 = qi * tq
    row = q0 + lax.broadcasted_iota(jnp.int32, (tq, tk), 0)
    col = lax.broadcasted_iota(jnp.int32, (tq, tk), 1)
    _init_state(m0, l0, a0, m1, l1, a1)

    def body(j, carry):
        start = pl.multiple_of(j * tk, tk)
        k = k_ref[pl.ds(start, tk), :]
        v = v_ref[pl.ds(start, tk), :]
        mask = (col + start) <= row
        _online_update(jnp.where(mask, _nt_dot(qa, k), NEG), v, m0, l0, a0)
        _online_update(jnp.where(mask, _nt_dot(qb, k), NEG), v, m1, l1, a1)
        return carry

    lax.fori_loop(0, (q0 + tq + tk - 1) // tk, body, 0)
    lam = par_ref[0:1, :]
    g = par_ref[1:2, :]
    o = _normalised(l0, a0) - lam * _normalised(l1, a1)
    o = o * lax.rsqrt(jnp.mean(o * o, axis=-1, keepdims=True) + RMS_EPS) * g
    o_ref[...] = (o * out_scale).astype(o_ref.dtype)


def diff_attention(q, k, v, lam, subln_g, lambda_init, tq=256, tk=256):
    b, s, _ = q.shape
    par = jnp.concatenate([jnp.broadcast_to(lam.astype(F32), (1, LANES)),
                           subln_g.astype(F32).reshape(1, LANES),
                           jnp.zeros((6, LANES), F32)], 0)
    kern = functools.partial(_diff_attn_kernel, tq=tq, tk=tk, out_scale=1.0 - lambda_init)
    st = [pltpu.VMEM((tq, LANES), F32)] * 6
    return pl.pallas_call(
        kern,
        out_shape=jax.ShapeDtypeStruct((b, s, DIFF_HEADS * LANES), F32),
        grid=(b, DIFF_HEADS, s // tq),
        in_specs=[pl.BlockSpec((None, tq, LANES), lambda bi, h, i: (bi, i, h)),
                  pl.BlockSpec((None, s, LANES), lambda bi, h, i: (bi, 0, h)),
                  pl.BlockSpec((None, s, LANES), lambda bi, h, i: (bi, 0, h)),
                  pl.BlockSpec((8, LANES), lambda bi, h, i: (0, 0))],
        out_specs=pl.BlockSpec((None, tq, LANES), lambda bi, h, i: (bi, i, h)),
        scratch_shapes=st,
        compiler_params=pltpu.CompilerParams(
            dimension_semantics=("parallel", "parallel", "arbitrary"), vmem_limit_bytes=VMEM_LIMIT),
        name="diff_attention",
    )(q, k, v, par)


def _nsa_attn_kernel(*refs, tq, tk, mode):
    if mode == "slc":
        q_ref, k_ref, v_ref, sel_ref, e_ref, o_ref, m, l, acc = refs
    else:
        q_ref, k_ref, v_ref, o_ref, m, l, acc = refs
    qi = pl.program_id(1)
    q0 = qi * tq
    nh = NSA_HEADS
    qs = jnp.concatenate([q_ref[:, h * HEAD_DIM:(h + 1) * HEAD_DIM] for h in range(nh)], axis=0)
    row = q0 + lax.broadcasted_iota(jnp.int32, (tq, tk), 0)
    row = jnp.concatenate([row] * nh, axis=0)
    col = lax.broadcasted_iota(jnp.int32, (nh * tq, tk), 1)
    if mode == "slc":
        sel = sel_ref[...]
        sel4 = jnp.concatenate([sel] * nh, axis=0)
    _init_state(m, l, acc)

    def body(j, carry):
        start = pl.multiple_of(j * tk, tk)
        k = k_ref[pl.ds(start, tk), :]
        v = v_ref[pl.ds(start, tk), :]
        kpos = col + start
        mask = kpos <= row
        if mode == "slc":
            chosen = jnp.dot(sel4, e_ref[j], preferred_element_type=F32)
            mask = mask & (chosen > 0.5)
        else:
            mask = mask & (kpos > row - WINDOW)
        _online_update(jnp.where(mask, _nt_dot(qs, k), NEG), v, m, l, acc)
        return carry

    hi = (q0 + tq + tk - 1) // tk
    lo = 0 if mode == "slc" else jnp.maximum(q0 - WINDOW + 1, 0) // tk
    lax.fori_loop(lo, hi, body, 0)
    o = _normalised(l, acc)
    for h in range(nh):
        o_ref[:, h * HEAD_DIM:(h + 1) * HEAD_DIM] = o[h * tq:(h + 1) * tq].astype(o_ref.dtype)


def _block_expander(n_chunks, n_blocks, tk, block_len):
    key = jnp.arange(n_chunks)[:, None, None] * tk + jnp.arange(tk)[None, None, :]
    return (key // block_len == jnp.arange(n_blocks)[None, :, None]).astype(BF16)


def nsa_attention(q, k, v, selmask=None, tq=128, tk=256):
    b, s, _ = q.shape
    mode = "slc" if selmask is not None else "win"
    in_specs = [pl.BlockSpec((None, tq, NSA_HEADS * HEAD_DIM), lambda bi, i: (bi, i, 0)),
                pl.BlockSpec((None, s, HEAD_DIM), lambda bi, i: (bi, 0, 0)),
                pl.BlockSpec((None, s, HEAD_DIM), lambda bi, i: (bi, 0, 0))]
    args = [q, k, v]
    if mode == "slc":
        n_slc = selmask.shape[-1]
        expander = _block_expander(s // tk, n_slc, tk, SLC_LEN)
        in_specs += [pl.BlockSpec((None, tq, n_slc), lambda bi, i: (bi, i, 0)),
                     pl.BlockSpec((s // tk, n_slc, tk), lambda bi, i: (0, 0, 0))]
        args += [selmask, expander]
    r = NSA_HEADS * tq
    return pl.pallas_call(
        functools.partial(_nsa_attn_kernel, tq=tq, tk=tk, mode=mode),
        out_shape=jax.ShapeDtypeStruct((b, s, NSA_HEADS * HEAD_DIM), F32),
        grid=(b, s // tq),
        in_specs=in_specs,
        out_specs=pl.BlockSpec((None, tq, NSA_HEADS * HEAD_DIM), lambda bi, i: (bi, i, 0)),
        scratch_shapes=[pltpu.VMEM((r, LANES), F32), pltpu.VMEM((r, LANES), F32),
                        pltpu.VMEM((r, HEAD_DIM), F32)],
        compiler_params=pltpu.CompilerParams(
            dimension_semantics=("parallel", "arbitrary"), vmem_limit_bytes=VMEM_LIMIT),
        name="nsa_" + mode + "_attention",
    )(*args)


def _moba_attn_kernel(q_ref, k_ref, v_ref, bm_ref, e_ref, o_ref, m0, l0, a0, m1, l1, a1, *, tq):
    tk = MOBA_BLOCK
    qi = pl.program_id(2)
    q = q_ref[...]
    lane = lax.broadcasted_iota(jnp.int32, q.shape, 1)
    zero = jnp.zeros_like(q)
    qa = jnp.where(lane < HEAD_DIM, q, zero)
    qb = jnp.where(lane >= HEAD_DIM, q, zero)
    q0 = qi * tq
    row = q0 + lax.broadcasted_iota(jnp.int32, (tq, tk), 0)
    col = lax.broadcasted_iota(jnp.int32, (tq, tk), 1)
    bm0 = bm_ref[0]
    bm1 = bm_ref[1]
    _init_state(m0, l0, a0, m1, l1, a1)

    def body(j, carry):
        start = pl.multiple_of(j * tk, tk)
        k = k_ref[pl.ds(start, tk), :]
        v = v_ref[pl.ds(start, tk), :]
        causal = (col + start) <= row
        e = e_ref[j]
        mask0 = causal & (jnp.dot(bm0, e, preferred_element_type=F32) > 0.5)
        mask1 = causal & (jnp.dot(bm1, e, preferred_element_type=F32) > 0.5)
        _online_update(jnp.where(mask0, _nt_dot(qa, k), NEG), v, m0, l0, a0)
        _online_update(jnp.where(mask1, _nt_dot(qb, k), NEG), v, m1, l1, a1)
        return carry

    lax.fori_loop(0, (q0 + tq + tk - 1) // tk, body, 0)
    o_ref[...] = jnp.where(lane < HEAD_DIM, _normalised(l0, a0), _normalised(l1, a1)).astype(o_ref.dtype)


def moba_attention(q, k, v, blkmask, tq=256):
    b, s, _ = q.shape
    n_blk = blkmask.shape[-1]
    expander = _block_expander(s // MOBA_BLOCK, n_blk, MOBA_BLOCK, MOBA_BLOCK)
    st = [pltpu.VMEM((tq, LANES), F32)] * 6
    return pl.pallas_call(
        functools.partial(_moba_attn_kernel, tq=tq),
        out_shape=jax.ShapeDtypeStruct((b, s, MOBA_HEADS * HEAD_DIM), F32),
        grid=(b, MOBA_HEADS // 2, s // tq),
        in_specs=[pl.BlockSpec((None, tq, LANES), lambda bi, h, i: (bi, i, h)),
                  pl.BlockSpec((None, s, LANES), lambda bi, h, i: (bi, 0, h)),
                  pl.BlockSpec((None, s, LANES), lambda bi, h, i: (bi, 0, h)),
                  pl.BlockSpec((None, 2, tq, n_blk), lambda bi, h, i: (bi, h, i, 0)),
                  pl.BlockSpec((s // MOBA_BLOCK, n_blk, MOBA_BLOCK), lambda bi, h, i: (0, 0, 0))],
        out_specs=pl.BlockSpec((None, tq, LANES), lambda bi, h, i: (bi, i, h)),
        scratch_shapes=st,
        compiler_params=pltpu.CompilerParams(
            dimension_semantics=("parallel", "parallel", "arbitrary"), vmem_limit_bytes=VMEM_LIMIT),
        name="moba_attention",
    )(q, k, v, blkmask, expander)


def _layer_norm(z, g, b):
    mu = jnp.mean(z, axis=-1, keepdims=True)
    zc = z - mu
    var = jnp.mean(zc * zc, axis=-1, keepdims=True)
    return zc * lax.rsqrt(var + LN_EPS) * g + b


def _merge_kernel(x_ref, od_ref, on_ref, om_ref, g0_ref, g1_ref, g2_ref,
                  wd_ref, wn_ref, wm_ref, wo_ref, ln_ref, o_ref):
    def branch(o_r, w_r, g_r):
        y = jnp.dot(o_r[...].astype(BF16), w_r[...], preferred_element_type=F32)
        return jax.nn.sigmoid(g_r[...]) * y

    merged = branch(od_ref, wd_ref, g0_ref) + branch(on_ref, wn_ref, g1_ref) + branch(om_ref, wm_ref, g2_ref)
    y = jnp.dot(merged.astype(BF16), wo_ref[...], preferred_element_type=F32)
    z = DEEPNORM_ALPHA * x_ref[...] + y
    o_ref[...] = _layer_norm(z, ln_ref[0:1, :], ln_ref[1:2, :])


def merge_project_norm(x, o_diff, o_nsa, o_moba, proj, w_d, w_n, w_m, w_o, ln_g, ln_b, tm=256):
    t = x.shape[0]
    ln = jnp.concatenate([ln_g.reshape(1, -1), ln_b.reshape(1, -1), jnp.zeros((6, D_MODEL), F32)], 0)
    gate_blk = C_MG // D_MODEL
    row = lambda i: (i, 0)
    full = lambda i: (0, 0)
    return pl.pallas_call(
        _merge_kernel,
        out_shape=jax.ShapeDtypeStruct((t, D_MODEL), F32),
        grid=(t // tm,),
        in_specs=[pl.BlockSpec((tm, D_MODEL), row),
                  pl.BlockSpec((tm, o_diff.shape[1]), row),
                  pl.BlockSpec((tm, o_nsa.shape[1]), row),
                  pl.BlockSpec((tm, o_moba.shape[1]), row),
                  pl.BlockSpec((tm, D_MODEL), lambda i: (i, gate_blk)),
                  pl.BlockSpec((tm, D_MODEL), lambda i: (i, gate_blk + 1)),
                  pl.BlockSpec((tm, D_MODEL), lambda i: (i, gate_blk + 2)),
                  pl.BlockSpec(w_d.shape, full), pl.BlockSpec(w_n.shape, full),
                  pl.BlockSpec(w_m.shape, full), pl.BlockSpec(w_o.shape, full),
                  pl.BlockSpec((8, D_MODEL), full)],
        out_specs=pl.BlockSpec((tm, D_MODEL), row),
        compiler_params=pltpu.CompilerParams(
            dimension_semantics=("parallel",), vmem_limit_bytes=VMEM_LIMIT),
        name="merge_project_norm",
    )(x, o_diff, o_nsa, o_moba, proj, proj, proj, w_d, w_n, w_m, w_o, ln)


def _moe_kernel(x_ref, ghi_ref, glo_ref, e_ref, wg_ref, wu_ref, wd_ref, ln_ref, o_ref, acc_ref):
    f = pl.program_id(1)

    @pl.when(f == 0)
    def _():
        acc_ref[...] = jnp.zeros(acc_ref.shape, F32)

    xb = x_ref[...].astype(BF16)
    hg = jnp.dot(xb, wg_ref[...], preferred_element_type=F32)
    hu = jnp.dot(xb, wu_ref[...], preferred_element_type=F32)
    e = e_ref[...]
    gate = (jnp.dot(ghi_ref[...], e, preferred_element_type=F32)
            + jnp.dot(glo_ref[...], e, preferred_element_type=F32))
    h = jax.nn.silu(hg) * hu * gate
    acc_ref[...] += jnp.dot(h.astype(BF16), wd_ref[...], preferred_element_type=F32)

    @pl.when(f == pl.num_programs(1) - 1)
    def _():
        z = DEEPNORM_ALPHA * x_ref[...] + acc_ref[...]
        o_ref[...] = _layer_norm(z, ln_ref[0:1, :], ln_ref[1:2, :])


def moe_norm(x, gate, wg, wu, wd, ln_g, ln_b, tm=768, tf=768):
    t = x.shape[0]
    f_tot = wg.shape[1]
    n_col = gate.shape[1]
    ghi = gate.astype(BF16)
    glo = (gate - ghi.astype(F32)).astype(BF16)
    expander = (jnp.arange(f_tot)[None, :] // D_EXPERT == jnp.arange(n_col)[:, None]).astype(BF16)
    ln = jnp.concatenate([ln_g.reshape(1, -1), ln_b.reshape(1, -1), jnp.zeros((6, D_MODEL), F32)], 0)
    return pl.pallas_call(
        _moe_kernel,
        out_shape=jax.ShapeDtypeStruct((t, D_MODEL), F32),
        grid=(t // tm, f_tot // tf),
        in_specs=[pl.BlockSpec((tm, D_MODEL), lambda i, f: (i, 0)),
                  pl.BlockSpec((tm, n_col), lambda i, f: (i, 0)),
                  pl.BlockSpec((tm, n_col), lambda i, f: (i, 0)),
                  pl.BlockSpec((n_col, tf), lambda i, f: (0, f)),
                  pl.BlockSpec((D_MODEL, tf), lambda i, f: (0, f)),
                  pl.BlockSpec((D_MODEL, tf), lambda i, f: (0, f)),
                  pl.BlockSpec((tf, D_MODEL), lambda i, f: (f, 0)),
                  pl.BlockSpec((8, D_MODEL), lambda i, f: (0, 0))],
        out_specs=pl.BlockSpec((tm, D_MODEL), lambda i, f: (i, 0)),
        scratch_shapes=[pltpu.VMEM((tm, D_MODEL), F32)],
        compiler_params=pltpu.CompilerParams(
            dimension_semantics=("parallel", "arbitrary"), vmem_limit_bytes=VMEM_LIMIT),
        name="moe_norm",
    )(x, ghi, glo, expander, wg, wu, wd, ln)


def _rope(x, cos, sin):
    t, w = x.shape
    xh = x.reshape(t, w // HEAD_DIM, 2, HALF)
    x1, x2 = xh[:, :, 0], xh[:, :, 1]
    c, s = cos[:, None, :], sin[:, None, :]
    return jnp.stack([x1 * c - x2 * s, x1 * s + x2 * c], axis=2).reshape(t, w)


def _masked_softmax(s, mask):
    s = jnp.where(mask, s.astype(F32), -jnp.inf)
    m = jnp.max(s, -1, keepdims=True)
    m = jnp.where(jnp.isfinite(m), m, 0.0)
    p = jnp.where(mask, jnp.exp(s - m), 0.0)
    return p / jnp.maximum(jnp.sum(p, -1, keepdims=True), 1e-30)


def _compress(kv, pos_emb, w1, w2):
    b, l, _ = kv.shape
    n_chunk = l // CMP_STRIDE
    n_cmp = (l - CMP_LEN) // CMP_STRIDE + 1
    x = kv[:, :n_chunk * CMP_STRIDE].reshape(b * n_chunk, CMP_STRIDE * HEAD_DIM)
    half = CMP_STRIDE * HEAD_DIM
    w_cat = jnp.concatenate([w1[:half], w1[half:]], axis=1).astype(BF16)
    rows = x.shape[0]
    tm = 512 if rows % 512 == 0 else rows
    y = matmul(x, w_cat, tm, w_cat.shape[1]).reshape(b, n_chunk, 2, w1.shape[1])
    bias = pos_emb.reshape(1, -1) @ w1
    hid = y[:, :n_cmp, 0] + y[:, 1:n_cmp + 1, 1] + bias
    return jax.nn.gelu(hid) @ w2


def _nsa_select(nq, ck_c, cv_c, q_pos, n_slc):
    n_cmp = ck_c.shape[1]
    top_n = min(SLC_TOPN, n_slc)
    cmp_start = jnp.arange(n_cmp) * CMP_STRIDE
    cmp_end = cmp_start + CMP_LEN - 1
    slc_start = jnp.arange(n_slc) * SLC_LEN
    overlap = ((cmp_start[:, None] <= slc_start[None, :] + SLC_LEN - 1)
               & (cmp_end[:, None] >= slc_start[None, :])).astype(F32)
    s = jnp.einsum('bqjd,bnd->bjqn', nq, ck_c) * ATTN_SCALE
    p = _masked_softmax(s, (cmp_end[None, :] <= q_pos[:, None])[None, None])
    o_cmp = jnp.einsum('bjqn,bnd->bqjd', p, cv_c)
    imp = jnp.einsum('bjqn,nm->bqm', p, overlap)
    blk = jnp.arange(n_slc)
    cur = q_pos // SLC_LEN
    valid = blk[None, :] <= cur[:, None]
    forced = (blk[None, :] == 0) | (blk[None, :] == cur[:, None]) | (blk[None, :] == cur[:, None] - 1)
    score = jnp.where(forced, jnp.inf, jnp.where(valid, imp, -jnp.inf))
    _, sel = lax.top_k(score, top_n)
    chosen = jnp.any(sel[..., None] == blk, axis=-2)
    return o_cmp, chosen & valid


def _moba_select(mq, k_mean, q_pos):
    n_blk = k_mean.shape[2]
    n_sel = min(MOBA_TOPK, n_blk)
    blk = jnp.arange(n_blk)
    cur = q_pos // MOBA_BLOCK
    earlier = blk[None, :] < cur[:, None]
    gate = jnp.einsum('bqhd,bhnd->bhqn', mq, k_mean)
    gate = jnp.where(earlier, gate, -jnp.inf)
    _, sel = lax.top_k(gate, n_sel)
    chosen = jnp.any(sel[..., None] == blk, axis=-2)
    return (chosen & earlier) | (blk[None, :] == cur[:, None])


def _route(t, router_w, router_b):
    s = jax.nn.sigmoid((t @ router_w).astype(F32))
    sb = s + router_b.astype(F32)
    grp = sb.reshape(-1, N_EXPERT_GROUPS, N_EXPERTS // N_EXPERT_GROUPS)
    grp_score = jnp.sum(lax.top_k(grp, 2)[0], -1)
    _, top_g = lax.top_k(grp_score, TOPK_GROUPS)
    gmask = jnp.sum(jax.nn.one_hot(top_g, N_EXPERT_GROUPS, dtype=F32), -2) > 0
    emask = jnp.repeat(gmask, N_EXPERTS // N_EXPERT_GROUPS, axis=-1)
    _, top_e = lax.top_k(jnp.where(emask, sb, -jnp.inf), TOP_K)
    w = jnp.take_along_axis(s, top_e, -1)
    w = w / jnp.sum(w, -1, keepdims=True) * ROUTED_SCALE
    return jnp.einsum('tk,tke->te', w, jax.nn.one_hot(top_e, N_EXPERTS, dtype=F32))


def _sample_mix(da_q, nq, nq_r, mq, n_gate, rows_diff, rows_nsa, rows_moba, rows_win,
                past_diff, past_nsa, past_moba, past_win, q_pos, win_pos0,
                lam, subln_g, lambda_init, cmp_pos, cmp_w1, cmp_w2):
    b, sq = da_q.shape[:2]
    lp = past_diff.shape[1]
    new_pos = q_pos
    causal_new = new_pos[None, :] <= q_pos[:, None]

    q = da_q.reshape(b, sq, 2 * DIFF_HEADS, HEAD_DIM)
    pk = past_diff[..., :512].reshape(b, lp, 2 * DIFF_HEADS, HEAD_DIM)
    pv = past_diff[..., 512:].reshape(b, lp, DIFF_HEADS, 2 * HEAD_DIM)
    nk = rows_diff[..., :512].reshape(b, sq, 2 * DIFF_HEADS, HEAD_DIM)
    nv = rows_diff[..., 512:].reshape(b, sq, DIFF_HEADS, 2 * HEAD_DIM)
    s_p = jnp.einsum('bqhd,bkhd->bhqk', q, pk) * ATTN_SCALE
    s_n = jnp.einsum('bqhd,bkhd->bhqk', q, nk) * ATTN_SCALE
    mask = jnp.concatenate([jnp.ones((sq, lp), bool), causal_new], -1)
    p = _masked_softmax(jnp.concatenate([s_p, s_n], -1), mask[None, None])
    p = p.reshape(b, DIFF_HEADS, 2, sq, lp + sq)
    a = p[:, :, 0] - lam * p[:, :, 1]
    o = (jnp.einsum('bhqk,bkhe->bqhe', a[..., :lp], pv)
         + jnp.einsum('bhqk,bkhe->bqhe', a[..., lp:], nv)).astype(F32)
    o = o * lax.rsqrt(jnp.mean(o * o, -1, keepdims=True) + RMS_EPS) * subln_g.astype(F32)
    o_diff = (o * (1.0 - lambda_init)).reshape(b, sq, DIFF_HEADS * 2 * HEAD_DIM)

    full_nsa = jnp.concatenate([past_nsa, rows_nsa], 1)
    l_tot = full_nsa.shape[1]
    ck, cv, sk, sv = [full_nsa[..., i * HEAD_DIM:(i + 1) * HEAD_DIM] for i in range(4)]
    ck_c = _compress(ck, cmp_pos[0], cmp_w1[0], cmp_w2[0])
    cv_c = _compress(cv, cmp_pos[1], cmp_w1[1], cmp_w2[1])
    l_pad = -(-l_tot // SLC_LEN) * SLC_LEN
    n_slc = l_pad // SLC_LEN
    o_cmp, selmask = _nsa_select(nq, ck_c, cv_c, q_pos, n_slc)
    kpos = jnp.arange(l_tot)
    key_ok = jnp.take_along_axis(selmask, jnp.broadcast_to((kpos // SLC_LEN)[None, None, :], (b, sq, l_tot)), -1)
    mask = key_ok & (kpos[None, None, :] <= q_pos[None, :, None])
    s2 = jnp.einsum('bqjd,bkd->bjqk', nq_r, sk) * ATTN_SCALE
    p2 = _masked_softmax(s2, mask[:, None])
    o_slc = jnp.einsum('bjqk,bkd->bqjd', p2, sv)
    full_win = jnp.concatenate([past_win, rows_win], 1)
    wk, wv = full_win[..., :HEAD_DIM], full_win[..., HEAD_DIM:]
    wpos = win_pos0 + jnp.arange(full_win.shape[1])
    mask = (wpos[None, :] <= q_pos[:, None]) & (wpos[None, :] > q_pos[:, None] - WINDOW)
    s3 = jnp.einsum('bqjd,bkd->bjqk', nq_r, wk) * ATTN_SCALE
    p3 = _masked_softmax(s3, mask[None, None])
    o_win = jnp.einsum('bjqk,bkd->bqjd', p3, wv)
    ng = jax.nn.sigmoid(n_gate.reshape(b, sq, NSA_HEADS, 3).astype(F32))
    o_nsa = (ng[..., 0:1] * o_cmp + ng[..., 1:2] * o_slc + ng[..., 2:3] * o_win).reshape(b, sq, -1)

    n_past_blk = lp // MOBA_BLOCK
    pmk = past_moba[..., :256].reshape(b, lp, MOBA_HEADS, HEAD_DIM)
    pmv = past_moba[..., 256:].reshape(b, lp, MOBA_HEADS, HEAD_DIM)
    nmk = rows_moba[..., :256].reshape(b, sq, MOBA_HEADS, HEAD_DIM)
    nmv = rows_moba[..., 256:].reshape(b, sq, MOBA_HEADS, HEAD_DIM)
    k_mean = jnp.mean(pmk.reshape(b, n_past_blk, MOBA_BLOCK, MOBA_HEADS, HEAD_DIM).astype(F32), axis=2)
    k_mean = jnp.transpose(k_mean, (0, 2, 1, 3))
    new_mean = jnp.sum(nmk.astype(F32), axis=1) / MOBA_BLOCK
    k_mean = jnp.concatenate([k_mean, new_mean[:, :, None, :]], axis=2)
    blkmask = _moba_select(mq, k_mean, q_pos)
    past_ok = jnp.repeat(blkmask[..., :n_past_blk], MOBA_BLOCK, axis=-1)
    s_p = jnp.einsum('bqhd,bkhd->bhqk', mq, pmk) * ATTN_SCALE
    s_n = jnp.einsum('bqhd,bkhd->bhqk', mq, nmk) * ATTN_SCALE
    mask = jnp.concatenate([past_ok, jnp.broadcast_to(causal_new, (b, MOBA_HEADS, sq, sq))], -1)
    p = _masked_softmax(jnp.concatenate([s_p, s_n], -1), mask)
    o_moba = (jnp.einsum('bhqk,bkhd->bqhd', p[..., :lp], pmv)
              + jnp.einsum('bhqk,bkhd->bqhd', p[..., lp:], nmv)).reshape(b, sq, -1)
    return o_diff, o_nsa, o_moba


def kernel(x_prompt, x_sample, cache_diff, cache_nsa, cache_moba, state_nsa_win, page_table, w_in, diff_lambda, diff_subln, nsa_cmp_pos, nsa_cmp_w1, nsa_cmp_w2, w_br_diff, w_br_nsa, w_br_moba, w_out, ln1_g, ln1_b, ln2_g, ln2_b, router_w, router_b, exp_w_gate, exp_w_up, exp_w_down, sh_w_gate, sh_w_up, sh_w_down):
    bp, sp, _ = x_prompt.shape
    bs, ss, _ = x_sample.shape
    tp, ts = bp * sp, bs * ss
    n_pages = page_table.shape[1]
    past_len = n_pages * PAGE_SIZE
    pos_p = jnp.arange(sp, dtype=jnp.int32)
    pos_s = past_len + jnp.arange(ss, dtype=jnp.int32)
    win_pos0 = past_len - state_nsa_win.shape[2]
    pos = jnp.concatenate([jnp.tile(pos_p, bp), jnp.tile(pos_s, bs)])
    inv = ROPE_THETA ** (-jnp.arange(HALF, dtype=F32) / HALF)
    ang = pos.astype(F32)[:, None] * inv[None, :]
    cos, sin = jnp.cos(ang), jnp.sin(ang)

    x = jnp.concatenate([x_prompt.reshape(tp, D_MODEL), x_sample.reshape(ts, D_MODEL)], 0)
    st_p = [[], [], [], []]
    st_s = [[], [], [], []]
    for l in range(DEPTH):
        lambda_init = 0.8 - 0.6 * math.exp(-0.3 * l)
        w_pad = jnp.concatenate([w_in[l][:, :C_MQ_SRC],
                                 jnp.zeros((D_MODEL, C_MQ - C_MQ_SRC), F32),
                                 w_in[l][:, C_MQ_SRC:]], axis=1).astype(BF16)
        proj = matmul(x, w_pad, 768, 768)

        da_q = _rope(proj[:, C_DAQ:C_DAK], cos, sin)
        da_k = _rope(proj[:, C_DAK:C_DAV], cos, sin)
        da_v = proj[:, C_DAV:C_NQ]
        n_q = proj[:, C_NQ:C_CK]
        nq_r = _rope(n_q, cos, sin)
        s_k = _rope(proj[:, C_SK:C_SV], cos, sin)
        w_k = _rope(proj[:, C_WK:C_WV], cos, sin)
        n_gate = proj[:, C_NG:C_NG + N_GATE]
        m_q = _rope(proj[:, C_MQ:C_MK], cos, sin)
        m_k = _rope(proj[:, C_MK:C_MV], cos, sin)
        m_v = proj[:, C_MV:C_MG]
        rows_diff = jnp.concatenate([da_k, da_v], -1)
        rows_nsa = jnp.concatenate([proj[:, C_CK:C_SK], s_k, proj[:, C_SV:C_WK]], -1)
        rows_moba = jnp.concatenate([m_k, m_v], -1)
        rows_win = jnp.concatenate([w_k, proj[:, C_WV:C_NG]], -1)

        lp_ = diff_lambda[l].astype(F32)
        lam = jnp.exp(jnp.sum(lp_[0] * lp_[1])) - jnp.exp(jnp.sum(lp_[2] * lp_[3])) + lambda_init

        def pr(a):
            return a[:tp].reshape(bp, sp, a.shape[-1])

        o_diff_p = diff_attention((pr(da_q) * ATTN_SCALE).astype(BF16), pr(da_k).astype(BF16),
                                  pr(da_v).astype(BF16), lam, diff_subln[l], lambda_init)
        ck_c = _compress(pr(proj[:, C_CK:C_CV]), nsa_cmp_pos[l, 0], nsa_cmp_w1[l, 0], nsa_cmp_w2[l, 0])
        cv_c = _compress(pr(proj[:, C_CV:C_SK]), nsa_cmp_pos[l, 1], nsa_cmp_w1[l, 1], nsa_cmp_w2[l, 1])
        o_cmp, selmask = _nsa_select(pr(n_q).reshape(bp, sp, NSA_HEADS, HEAD_DIM), ck_c, cv_c, pos_p,
                                     sp // SLC_LEN)
        nqr_b = (pr(nq_r) * ATTN_SCALE).astype(BF16)
        o_slc = nsa_attention(nqr_b, pr(s_k).astype(BF16), pr(proj[:, C_SV:C_WK]).astype(BF16),
                              selmask.astype(BF16))
        o_win = nsa_attention(nqr_b, pr(w_k).astype(BF16), pr(proj[:, C_WV:C_NG]).astype(BF16))
        ng = jax.nn.sigmoid(pr(n_gate).reshape(bp, sp, NSA_HEADS, 3))
        o_nsa_p = (ng[..., 0:1] * o_cmp + ng[..., 1:2] * o_slc.reshape(bp, sp, NSA_HEADS, HEAD_DIM)
                   + ng[..., 2:3] * o_win.reshape(bp, sp, NSA_HEADS, HEAD_DIM)).reshape(bp, sp, -1)
        mk_p = pr(m_k)
        k_mean = jnp.mean(mk_p.reshape(bp, sp // MOBA_BLOCK, MOBA_BLOCK, MOBA_HEADS, HEAD_DIM), axis=2)
        blkmask = _moba_select(pr(m_q).reshape(bp, sp, MOBA_HEADS, HEAD_DIM),
                               jnp.transpose(k_mean, (0, 2, 1, 3)), pos_p)
        o_moba_p = moba_attention((pr(m_q) * ATTN_SCALE).astype(BF16), mk_p.astype(BF16),
                                  pr(m_v).astype(BF16), blkmask.astype(BF16))

        def sm(a):
            return a[tp:].reshape(bs, ss, a.shape[-1])

        def pages(pool):
            return pool[l][page_table].reshape(bs, past_len, pool.shape[-1])

        o_diff_s, o_nsa_s, o_moba_s = _sample_mix(
            sm(da_q), sm(n_q).reshape(bs, ss, NSA_HEADS, HEAD_DIM), sm(nq_r).reshape(bs, ss, NSA_HEADS, HEAD_DIM),
            sm(m_q).reshape(bs, ss, MOBA_HEADS, HEAD_DIM), sm(n_gate),
            sm(rows_diff), sm(rows_nsa), sm(rows_moba), sm(rows_win),
            pages(cache_diff), pages(cache_nsa), pages(cache_moba), state_nsa_win[l], pos_s, win_pos0,
            lam, diff_subln[l], lambda_init, nsa_cmp_pos[l], nsa_cmp_w1[l], nsa_cmp_w2[l])

        def both(a_p, a_s):
            return jnp.concatenate([a_p.reshape(tp, -1), a_s.reshape(ts, -1)], 0)

        x = merge_project_norm(x, both(o_diff_p, o_diff_s), both(o_nsa_p, o_nsa_s), both(o_moba_p, o_moba_s),
                               proj, w_br_diff[l].astype(BF16), w_br_nsa[l].astype(BF16),
                               w_br_moba[l].astype(BF16), w_out[l].astype(BF16), ln1_g[l], ln1_b[l])

        gate = _route(x, router_w[l], router_b[l])
        gate = jnp.concatenate([gate, jnp.ones((tp + ts, D_SHARED // D_EXPERT), F32),
                                jnp.zeros((tp + ts, LANES - N_EXPERTS - D_SHARED // D_EXPERT), F32)], -1)
        wg = jnp.concatenate([jnp.transpose(exp_w_gate[l], (1, 0, 2)).reshape(D_MODEL, -1), sh_w_gate[l]], 1)
        wu = jnp.concatenate([jnp.transpose(exp_w_up[l], (1, 0, 2)).reshape(D_MODEL, -1), sh_w_up[l]], 1)
        wd = jnp.concatenate([exp_w_down[l].reshape(-1, D_MODEL), sh_w_down[l]], 0)
        x = moe_norm(x, gate, wg.astype(BF16), wu.astype(BF16), wd.astype(BF16), ln2_g[l], ln2_b[l])

        for lst, val in zip(st_p, (rows_diff, rows_nsa, rows_moba)):
            lst.append(pr(val))
        n_keep = min(WINDOW, sp)
        st_p[3].append(pr(rows_win)[:, sp - n_keep:])
        for lst, val in zip(st_s, (rows_diff, rows_nsa, rows_moba)):
            lst.append(sm(val))
        full_win = jnp.concatenate([state_nsa_win[l], sm(rows_win)], 1)
        n_keep = min(WINDOW, past_len + ss)
        st_s[3].append(full_win[:, full_win.shape[1] - n_keep:])

    outs_p = [jnp.stack(a, 0) for a in st_p]
    outs_s = [jnp.stack(a, 0) for a in st_s]
    return (x[:tp].reshape(bp, sp, D_MODEL), x[tp:].reshape(bs, ss, D_MODEL),
            outs_p[0], outs_s[0], outs_p[1], outs_s[1], outs_p[2], outs_s[2], outs_p[3], outs_s[3])
```

```python
import functools
import math

import jax
import jax.numpy as jnp
from jax import lax
from jax.experimental import pallas as pl
from jax.experimental.pallas import tpu as pltpu

F32 = jnp.float32
BF16 = jnp.bfloat16

D_MODEL = 1024
DEPTH = 2
PAGE_SIZE = 128
HEAD_DIM = 64
HALF = HEAD_DIM // 2
ATTN_SCALE = HEAD_DIM ** -0.5
ROPE_THETA = 10000.0
DIFF_HEADS = 4
NSA_HEADS = 4
CMP_LEN = 32
CMP_STRIDE = 16
SLC_LEN = 64
SLC_TOPN = 16
WINDOW = 512
MOBA_HEADS = 4
MOBA_BLOCK = 256
MOBA_TOPK = 3
N_EXPERTS = 64
N_EXPERT_GROUPS = 8
TOPK_GROUPS = 4
TOP_K = 6
D_EXPERT = 128
D_SHARED = 256
ROUTED_SCALE = 2.5
LN_EPS = 1e-5
RMS_EPS = 1e-5
DEEPNORM_ALPHA = (2 * DEPTH) ** 0.25

C_DAQ, C_DAK, C_DAV, C_NQ = 0, 512, 1024, 1536
C_CK, C_CV, C_SK, C_SV, C_WK, C_WV, C_NG = 1792, 1856, 1920, 1984, 2048, 2112, 2176
N_GATE = 3 * NSA_HEADS
C_MQ_SRC = C_NG + N_GATE
C_MQ, C_MK, C_MV, C_MG = 2304, 2560, 2816, 3072
N_IN_PAD = C_MG + 3 * D_MODEL

LANES = 128
NEG = -1e30
VMEM_LIMIT = 56 * 1024 * 1024


def _nt_dot(a, b):
    return lax.dot_general(a, b, (((1,), (1,)), ((), ())), preferred_element_type=F32)


def _mm_kernel(x_ref, w_ref, o_ref):
    o_ref[...] = jnp.dot(x_ref[...].astype(BF16), w_ref[...],
                         preferred_element_type=F32).astype(o_ref.dtype)


def matmul(x, w, tm, tn, out_dtype=F32):
    m, k = x.shape
    n = w.shape[1]
    assert m % tm == 0 and n % tn == 0, (x.shape, w.shape, tm, tn)
    return pl.pallas_call(
        _mm_kernel,
        out_shape=jax.ShapeDtypeStruct((m, n), out_dtype),
        grid=(m // tm, n // tn),
        in_specs=[pl.BlockSpec((tm, k), lambda i, j: (i, 0)),
                  pl.BlockSpec((k, tn), lambda i, j: (0, j))],
        out_specs=pl.BlockSpec((tm, tn), lambda i, j: (i, j)),
        compiler_params=pltpu.CompilerParams(
            dimension_semantics=("parallel", "arbitrary"), vmem_limit_bytes=VMEM_LIMIT),
        name="matmul",
    )(x, w)


def _online_update(s, v, m_ref, l_ref, acc_ref):
    tk = s.shape[1]
    dv = acc_ref.shape[-1]
    m_prev = m_ref[...]
    m_next = jnp.maximum(m_prev, jnp.max(s, axis=-1, keepdims=True))
    alpha = jnp.exp(m_prev - m_next)
    p = jnp.exp(s - pltpu.repeat(m_next, tk // LANES, axis=1))
    l_ref[...] = alpha * l_ref[...] + jnp.sum(p, axis=-1, keepdims=True)
    m_ref[...] = m_next
    acc_ref[...] = acc_ref[...] * alpha[:, :dv] + jnp.dot(p.astype(BF16), v, preferred_element_type=F32)


def _init_state(*refs):
    for m_ref, l_ref, acc_ref in zip(refs[0::3], refs[1::3], refs[2::3]):
        m_ref[...] = jnp.full(m_ref.shape, NEG, F32)
        l_ref[...] = jnp.zeros(l_ref.shape, F32)
        acc_ref[...] = jnp.zeros(acc_ref.shape, F32)


def _normalised(l_ref, acc_ref):
    dv = acc_ref.shape[-1]
    return acc_ref[...] / jnp.maximum(l_ref[...], 1e-30)[:, :dv]


def _diff_attn_kernel(q_ref, k_ref, v_ref, par_ref, o_ref,
                      m0, l0, a0, m1, l1, a1, *, tq, tk, out_scale):
    qi = pl.program_id(2)
    q = q_ref[...]
    lane = lax.broadcasted_iota(jnp.int32, q.shape, 1)
    zero = jnp.zeros_like(q)
    qa = jnp.where(lane < HEAD_DIM, q, zero)
    qb = jnp.where(lane >= HEAD_DIM, q, zero)
    q0 = qi * tq
    row = q0 + lax.broadcasted_iota(jnp.int32, (tq, tk), 0)
    col = lax.broadcasted_iota(jnp.int32, (tq, tk), 1)
    _init_state(m0, l0, a0, m1, l1, a1)

    def body(j, carry):
        start = pl.multiple_of(j * tk, tk)
        k = k_ref[pl.ds(start, tk), :]
        v = v_ref[pl.ds(start, tk), :]
        mask = (col + start) <= row
        _online_update(jnp.where(mask, _nt_dot(qa, k), NEG), v, m0, l0, a0)
        _online_update(jnp.where(mask, _nt_dot(qb, k), NEG), v, m1, l1, a1)
        return carry

    lax.fori_loop(0, (q0 + tq + tk - 1) // tk, body, 0)
    lam = par_ref[0:1, :]
    g = par_ref[1:2, :]
    o = _normalised(l0, a0) - lam * _normalised(l1, a1)
    o = o * lax.rsqrt(jnp.mean(o * o, axis=-1, keepdims=True) + RMS_EPS) * g
    o_ref[...] = (o * out_scale).astype(o_ref.dtype)


def diff_attention(q, k, v, lam, subln_g, lambda_init, tq=256, tk=256):
    b, s, _ = q.shape
    par = jnp.concatenate([jnp.broadcast_to(lam.astype(F32), (1, LANES)),
                           subln_g.astype(F32).reshape(1, LANES),
                           jnp.zeros((6, LANES), F32)], 0)
    kern = functools.partial(_diff_attn_kernel, tq=tq, tk=tk, out_scale=1.0 - lambda_init)
    st = [pltpu.VMEM((tq, LANES), F32)] * 6
    return pl.pallas_call(
        kern,
        out_shape=jax.ShapeDtypeStruct((b, s, DIFF_HEADS * LANES), F32),
        grid=(b, DIFF_HEADS, s // tq),
        in_specs=[pl.BlockSpec((None, tq, LANES), lambda bi, h, i: (bi, i, h)),
                  pl.BlockSpec((None, s, LANES), lambda bi, h, i: (bi, 0, h)),
                  pl.BlockSpec((None, s, LANES), lambda bi, h, i: (bi, 0, h)),
                  pl.BlockSpec((8, LANES), lambda bi, h, i: (0, 0))],
        out_specs=pl.BlockSpec((None, tq, LANES), lambda bi, h, i: (bi, i, h)),
        scratch_shapes=st,
        compiler_params=pltpu.CompilerParams(
            dimension_semantics=("parallel", "parallel", "arbitrary"), vmem_limit_bytes=VMEM_LIMIT),
        name="diff_attention",
    )(q, k, v, par)


def _nsa_attn_kernel(*refs, tq, tk, mode):
    if mode == "slc":
        q_ref, k_ref, v_ref, sel_ref, e_ref, o_ref, m, l, acc = refs
    else:
        q_ref, k_ref, v_ref, o_ref, m, l, acc = refs
    qi = pl.program_id(1)
    q0 = qi * tq
    nh = NSA_HEADS
    qs = jnp.concatenate([q_ref[:, h * HEAD_DIM:(h + 1) * HEAD_DIM] for h in range(nh)], axis=0)
    row = q0 + lax.broadcasted_iota(jnp.int32, (tq, tk), 0)
    row = jnp.concatenate([row] * nh, axis=0)
    col = lax.broadcasted_iota(jnp.int32, (nh * tq, tk), 1)
    if mode == "slc":
        sel = sel_ref[...]
        sel4 = jnp.concatenate([sel] * nh, axis=0)
    _init_state(m, l, acc)

    def body(j, carry):
        start = pl.multiple_of(j * tk, tk)
        k = k_ref[pl.ds(start, tk), :]
        v = v_ref[pl.ds(start, tk), :]
        kpos = col + start
        mask = kpos <= row
        if mode == "slc":
            chosen = jnp.dot(sel4, e_ref[j], preferred_element_type=F32)
            mask = mask & (chosen > 0.5)
        else:
            mask = mask & (kpos > row - WINDOW)
        _online_update(jnp.where(mask, _nt_dot(qs, k), NEG), v, m, l, acc)
        return carry

    hi = (q0 + tq + tk - 1) // tk
    lo = 0 if mode == "slc" else jnp.maximum(q0 - WINDOW + 1, 0) // tk
    lax.fori_loop(lo, hi, body, 0)
    o = _normalised(l, acc)
    for h in range(nh):
        o_ref[:, h * HEAD_DIM:(h + 1) * HEAD_DIM] = o[h * tq:(h + 1) * tq].astype(o_ref.dtype)


def _block_expander(n_chunks, n_blocks, tk, block_len):
    key = jnp.arange(n_chunks)[:, None, None] * tk + jnp.arange(tk)[None, None, :]
    return (key // block_len == jnp.arange(n_blocks)[None, :, None]).astype(BF16)


def nsa_attention(q, k, v, selmask=None, tq=128, tk=256):
    b, s, _ = q.shape
    mode = "slc" if selmask is not None else "win"
    in_specs = [pl.BlockSpec((None, tq, NSA_HEADS * HEAD_DIM), lambda bi, i: (bi, i, 0)),
                pl.BlockSpec((None, s, HEAD_DIM), lambda bi, i: (bi, 0, 0)),
                pl.BlockSpec((None, s, HEAD_DIM), lambda bi, i: (bi, 0, 0))]
    args = [q, k, v]
    if mode == "slc":
        n_slc = selmask.shape[-1]
        expander = _block_expander(s // tk, n_slc, tk, SLC_LEN)
        in_specs += [pl.BlockSpec((None, tq, n_slc), lambda bi, i: (bi, i, 0)),
                     pl.BlockSpec((s // tk, n_slc, tk), lambda bi, i: (0, 0, 0))]
        args += [selmask, expander]
    r = NSA_HEADS * tq
    return pl.pallas_call(
        functools.partial(_nsa_attn_kernel, tq=tq, tk=tk, mode=mode),
        out_shape=jax.ShapeDtypeStruct((b, s, NSA_HEADS * HEAD_DIM), F32),
        grid=(b, s // tq),
        in_specs=in_specs,
        out_specs=pl.BlockSpec((None, tq, NSA_HEADS * HEAD_DIM), lambda bi, i: (bi, i, 0)),
        scratch_shapes=[pltpu.VMEM((r, LANES), F32), pltpu.VMEM((r, LANES), F32),
                        pltpu.VMEM((r, HEAD_DIM), F32)],
        compiler_params=pltpu.CompilerParams(
            dimension_semantics=("parallel", "arbitrary"), vmem_limit_bytes=VMEM_LIMIT),
        name="nsa_" + mode + "_attention",
    )(*args)


def _moba_attn_kernel(q_ref, k_ref, v_ref, bm_ref, e_ref, o_ref, m0, l0, a0, m1, l1, a1, *, tq):
    tk = MOBA_BLOCK
    qi = pl.program_id(2)
    q = q_ref[...]
    lane = lax.broadcasted_iota(jnp.int32, q.shape, 1)
    zero = jnp.zeros_like(q)
    qa = jnp.where(lane < HEAD_DIM, q, zero)
    qb = jnp.where(lane >= HEAD_DIM, q, zero)
    q0 = qi * tq
    row = q0 + lax.broadcasted_iota(jnp.int32, (tq, tk), 0)
    col = lax.broadcasted_iota(jnp.int32, (tq, tk), 1)
    bm0 = bm_ref[0]
    bm1 = bm_ref[1]
    _init_state(m0, l0, a0, m1, l1, a1)

    def body(j, carry):
        start = pl.multiple_of(j * tk, tk)
        k = k_ref[pl.ds(start, tk), :]
        v = v_ref[pl.ds(start, tk), :]
        causal = (col + start) <= row
        e = e_ref[j]
        mask0 = causal & (jnp.dot(bm0, e, preferred_element_type=F32) > 0.5)
        mask1 = causal & (jnp.dot(bm1, e, preferred_element_type=F32) > 0.5)
        _online_update(jnp.where(mask0, _nt_dot(qa, k), NEG), v, m0, l0, a0)
        _online_update(jnp.where(mask1, _nt_dot(qb, k), NEG), v, m1, l1, a1)
        return carry

    lax.fori_loop(0, (q0 + tq + tk - 1) // tk, body, 0)
    o_ref[...] = jnp.where(lane < HEAD_DIM, _normalised(l0, a0), _normalised(l1, a1)).astype(o_ref.dtype)


def moba_attention(q, k, v, blkmask, tq=256):
    b, s, _ = q.shape
    n_blk = blkmask.shape[-1]
    expander = _block_expander(s // MOBA_BLOCK, n_blk, MOBA_BLOCK, MOBA_BLOCK)
    st = [pltpu.VMEM((tq, LANES), F32)] * 6
    return pl.pallas_call(
        functools.partial(_moba_attn_kernel, tq=tq),
        out_shape=jax.ShapeDtypeStruct((b, s, MOBA_HEADS * HEAD_DIM), F32),
        grid=(b, MOBA_HEADS // 2, s // tq),
        in_specs=[pl.BlockSpec((None, tq, LANES), lambda bi, h, i: (bi, i, h)),
                  pl.BlockSpec((None, s, LANES), lambda bi, h, i: (bi, 0, h)),
                  pl.BlockSpec((None, s, LANES), lambda bi, h, i: (bi, 0, h)),
                  pl.BlockSpec((None, 2, tq, n_blk), lambda bi, h, i: (bi, h, i, 0)),
                  pl.BlockSpec((s // MOBA_BLOCK, n_blk, MOBA_BLOCK), lambda bi, h, i: (0, 0, 0))],
        out_specs=pl.BlockSpec((None, tq, LANES), lambda bi, h, i: (bi, i, h)),
        scratch_shapes=st,
        compiler_params=pltpu.CompilerParams(
            dimension_semantics=("parallel", "parallel", "arbitrary"), vmem_limit_bytes=VMEM_LIMIT),
        name="moba_attention",
    )(q, k, v, blkmask, expander)


def _layer_norm(z, g, b):
    mu = jnp.mean(z, axis=-1, keepdims=True)
    zc = z - mu
    var = jnp.mean(zc * zc, axis=-1, keepdims=True)
    return zc * lax.rsqrt(var + LN_EPS) * g + b


def _merge_kernel(x_ref, od_ref, on_ref, om_ref, g0_ref, g1_ref, g2_ref,
                  wd_ref, wn_ref, wm_ref, wo_ref, ln_ref, o_ref):
    def branch(o_r, w_r, g_r):
        y = jnp.dot(o_r[...].astype(BF16), w_r[...], preferred_element_type=F32)
        return jax.nn.sigmoid(g_r[...]) * y

    merged = branch(od_ref, wd_ref, g0_ref) + branch(on_ref, wn_ref, g1_ref) + branch(om_ref, wm_ref, g2_ref)
    y = jnp.dot(merged.astype(BF16), wo_ref[...], preferred_element_type=F32)
    z = DEEPNORM_ALPHA * x_ref[...] + y
    o_ref[...] = _layer_norm(z, ln_ref[0:1, :], ln_ref[1:2, :])


def merge_project_norm(x, o_diff, o_nsa, o_moba, proj, w_d, w_n, w_m, w_o, ln_g, ln_b, tm=256):
    t = x.shape[0]
    ln = jnp.concatenate([ln_g.reshape(1, -1), ln_b.reshape(1, -1), jnp.zeros((6, D_MODEL), F32)], 0)
    gate_blk = C_MG // D_MODEL
    row = lambda i: (i, 0)
    full = lambda i: (0, 0)
    return pl.pallas_call(
        _merge_kernel,
        out_shape=jax.ShapeDtypeStruct((t, D_MODEL), F32),
        grid=(t // tm,),
        in_specs=[pl.BlockSpec((tm, D_MODEL), row),
                  pl.BlockSpec((tm, o_diff.shape[1]), row),
                  pl.BlockSpec((tm, o_nsa.shape[1]), row),
                  pl.BlockSpec((tm, o_moba.shape[1]), row),
                  pl.BlockSpec((tm, D_MODEL), lambda i: (i, gate_blk)),
                  pl.BlockSpec((tm, D_MODEL), lambda i: (i, gate_blk + 1)),
                  pl.BlockSpec((tm, D_MODEL), lambda i: (i, gate_blk + 2)),
                  pl.BlockSpec(w_d.shape, full), pl.BlockSpec(w_n.shape, full),
                  pl.BlockSpec(w_m.shape, full), pl.BlockSpec(w_o.shape, full),
                  pl.BlockSpec((8, D_MODEL), full)],
        out_specs=pl.BlockSpec((tm, D_MODEL), row),
        compiler_params=pltpu.CompilerParams(
            dimension_semantics=("parallel",), vmem_limit_bytes=VMEM_LIMIT),
        name="merge_project_norm",
    )(x, o_diff, o_nsa, o_moba, proj, proj, proj, w_d, w_n, w_m, w_o, ln)


def _moe_kernel(x_ref, ghi_ref, glo_ref, e_ref, wg_ref, wu_ref, wd_ref, ln_ref, o_ref, acc_ref):
    f = pl.program_id(1)

    @pl.when(f == 0)
    def _():
        acc_ref[...] = jnp.zeros(acc_ref.shape, F32)

    xb = x_ref[...].astype(BF16)
    hg = jnp.dot(xb, wg_ref[...], preferred_element_type=F32)
    hu = jnp.dot(xb, wu_ref[...], preferred_element_type=F32)
    e = e_ref[...]
    gate = (jnp.dot(ghi_ref[...], e, preferred_element_type=F32)
            + jnp.dot(glo_ref[...], e, preferred_element_type=F32))
    h = jax.nn.silu(hg) * hu * gate
    acc_ref[...] += jnp.dot(h.astype(BF16), wd_ref[...], preferred_element_type=F32)

    @pl.when(f == pl.num_programs(1) - 1)
    def _():
        z = DEEPNORM_ALPHA * x_ref[...] + acc_ref[...]
        o_ref[...] = _layer_norm(z, ln_ref[0:1, :], ln_ref[1:2, :])


def moe_norm(x, gate, wg, wu, wd, ln_g, ln_b, tm=768, tf=768):
    t = x.shape[0]
    f_tot = wg.shape[1]
    n_col = gate.shape[1]
    ghi = gate.astype(BF16)
    glo = (gate - ghi.astype(F32)).astype(BF16)
    expander = (jnp.arange(f_tot)[None, :] // D_EXPERT == jnp.arange(n_col)[:, None]).astype(BF16)
    ln = jnp.concatenate([ln_g.reshape(1, -1), ln_b.reshape(1, -1), jnp.zeros((6, D_MODEL), F32)], 0)
    return pl.pallas_call(
        _moe_kernel,
        out_shape=jax.ShapeDtypeStruct((t, D_MODEL), F32),
        grid=(t // tm, f_tot // tf),
        in_specs=[pl.BlockSpec((tm, D_MODEL), lambda i, f: (i, 0)),
                  pl.BlockSpec((tm, n_col), lambda i, f: (i, 0)),
                  pl.BlockSpec((tm, n_col), lambda i, f: (i, 0)),
                  pl.BlockSpec((n_col, tf), lambda i, f: (0, f)),
                  pl.BlockSpec((D_MODEL, tf), lambda i, f: (0, f)),
                  pl.BlockSpec((D_MODEL, tf), lambda i, f: (0, f)),
                  pl.BlockSpec((tf, D_MODEL), lambda i, f: (f, 0)),
                  pl.BlockSpec((8, D_MODEL), lambda i, f: (0, 0))],
        out_specs=pl.BlockSpec((tm, D_MODEL), lambda i, f: (i, 0)),
        scratch_shapes=[pltpu.VMEM((tm, D_MODEL), F32)],
        compiler_params=pltpu.CompilerParams(
            dimension_semantics=("parallel", "arbitrary"), vmem_limit_bytes=VMEM_LIMIT),
        name="moe_norm",
    )(x, ghi, glo, expander, wg, wu, wd, ln)


TOK_PAD = 8
NEW_PAD = 16


def _softmax_two(s, sn):
    m = jnp.maximum(jnp.max(s, axis=-1, keepdims=True), jnp.max(sn, axis=-1, keepdims=True))
    p = jnp.exp(s - m)
    pn = jnp.exp(sn - m)
    inv = 1.0 / (jnp.sum(p, axis=-1, keepdims=True) + jnp.sum(pn, axis=-1, keepdims=True))
    return p * inv, pn * inv


def _new_row_mask(rows, n_new):
    t = lax.broadcasted_iota(jnp.int32, (rows, NEW_PAD), 0) & (TOK_PAD - 1)
    i = lax.broadcasted_iota(jnp.int32, (rows, NEW_PAD), 1)
    return (i <= t) & (i < n_new)


def _rank_select(score, n_candidates, top_n):
    lane = lax.broadcasted_iota(jnp.int32, score.shape, 1)
    rank = jnp.zeros(score.shape, F32)
    for c in range(n_candidates):
        col = score[:, c:c + 1]
        ahead = (col > score) | ((col == score) & (c < lane))
        rank = rank + jnp.where(ahead, 1.0, 0.0)
    return rank < top_n


def _dec_diff_kernel(pt_ref, q_ref, new_ref, par_ref, *rest, n_pages, n_new, out_scale):
    pages = rest[:n_pages]
    o_ref, s_ref = rest[n_pages:]
    nqk = 2 * DIFF_HEADS * HEAD_DIM
    half = DIFF_HEADS * TOK_PAD
    q = q_ref[...]
    for j in range(n_pages):
        s_ref[:, j * PAGE_SIZE:(j + 1) * PAGE_SIZE] = _nt_dot(q, pages[j][:, 0:nqk].astype(BF16))
    sn = _nt_dot(q, new_ref[:, 0:nqk].astype(BF16))
    sn = jnp.where(_new_row_mask(2 * half, n_new), sn, NEG)
    p, pn = _softmax_two(s_ref[...], sn)
    lam = par_ref[0:1, 0:1]
    a = (p[0:half] - lam * p[half:2 * half]).astype(BF16)
    an = (pn[0:half] - lam * pn[half:2 * half]).astype(BF16)
    o = jnp.dot(an, new_ref[:, nqk:].astype(BF16), preferred_element_type=F32)
    for j in range(n_pages):
        o = o + jnp.dot(a[:, j * PAGE_SIZE:(j + 1) * PAGE_SIZE], pages[j][:, nqk:].astype(BF16),
                        preferred_element_type=F32)
    g = par_ref[1:2, :]
    for h in range(DIFF_HEADS):
        oh = o[h * TOK_PAD:(h + 1) * TOK_PAD, h * LANES:(h + 1) * LANES]
        oh = oh * lax.rsqrt(jnp.mean(oh * oh, axis=-1, keepdims=True) + RMS_EPS) * g
        o_ref[:, h * LANES:(h + 1) * LANES] = oh * out_scale


def _page_specs(layer, n_pages, width):
    return [pl.BlockSpec((None, None, PAGE_SIZE, width),
                         lambda b, pt, j=j: (layer, pt[b * n_pages + j], 0, 0)) for j in range(n_pages)]


def _per_seq(shape):
    return pl.BlockSpec((None,) + shape, lambda b, pt: (b,) + (0,) * len(shape))


def _shared(shape):
    return pl.BlockSpec(shape, lambda b, pt: (0,) * len(shape))


def _pad_rows(a, n):
    return jnp.pad(a, ((0, 0), (0, n - a.shape[1])) + ((0, 0),) * (a.ndim - 2))


def decode_diff_attention(layer, page_table, cache, da_q, rows_new, lam, subln_g, lambda_init):
    b, n_new, _ = da_q.shape
    n_pages = page_table.shape[1]
    q = (da_q * ATTN_SCALE).reshape(b, n_new, DIFF_HEADS, 2, HEAD_DIM)
    q = _pad_rows(jnp.transpose(q, (0, 3, 2, 1, 4)).reshape(b * 2 * DIFF_HEADS, n_new, HEAD_DIM), TOK_PAD)
    q = q.reshape(b, 2, DIFF_HEADS, TOK_PAD, 1, HEAD_DIM)
    head = 2 * jnp.arange(DIFF_HEADS)[None, :] + jnp.arange(2)[:, None]
    place = (head[:, :, None] == jnp.arange(2 * DIFF_HEADS)).astype(F32)
    qbd = (q * place[None, :, :, None, :, None]).reshape(b, 2 * DIFF_HEADS * TOK_PAD, 2 * DIFF_HEADS * HEAD_DIM)
    par = jnp.concatenate([jnp.broadcast_to(lam.astype(F32), (1, LANES)), subln_g.astype(F32).reshape(1, LANES),
                           jnp.zeros((6, LANES), F32)], 0)
    rows = 2 * DIFF_HEADS * TOK_PAD
    width = cache.shape[-1]
    out = pl.pallas_call(
        functools.partial(_dec_diff_kernel, n_pages=n_pages, n_new=n_new, out_scale=1.0 - lambda_init),
        out_shape=jax.ShapeDtypeStruct((b, TOK_PAD, DIFF_HEADS * LANES), F32),
        grid_spec=pltpu.PrefetchScalarGridSpec(
            num_scalar_prefetch=1, grid=(b,),
            in_specs=[_per_seq((rows, 2 * DIFF_HEADS * HEAD_DIM)), _per_seq((NEW_PAD, width)), _shared((8, LANES))]
            + _page_specs(layer, n_pages, width),
            out_specs=_per_seq((TOK_PAD, DIFF_HEADS * LANES)),
            scratch_shapes=[pltpu.VMEM((rows, n_pages * PAGE_SIZE), F32)]),
        compiler_params=pltpu.CompilerParams(dimension_semantics=("arbitrary",), vmem_limit_bytes=VMEM_LIMIT),
        name="decode_diff_attention",
    )(page_table.reshape(-1), qbd.astype(BF16), _pad_rows(rows_new, NEW_PAD), par, *([cache] * n_pages))
    return out[:, :n_new]


def _dec_moba_kernel(pt_ref, q_ref, new_ref, *rest, n_pages, n_new, past_len):
    pages = rest[:n_pages]
    o_ref, s_ref, km_ref = rest[n_pages:]
    w = MOBA_HEADS * HEAD_DIM
    rows = MOBA_HEADS * TOK_PAD
    ppb = MOBA_BLOCK // PAGE_SIZE
    n_blk = n_pages // ppb
    q = q_ref[...]
    km_ref[...] = jnp.zeros(km_ref.shape, F32)
    for n in range(n_blk):
        tot = jnp.sum(pages[ppb * n][:, 0:w], axis=0, keepdims=True)
        for j in range(ppb * n + 1, ppb * (n + 1)):
            tot = tot + jnp.sum(pages[j][:, 0:w], axis=0, keepdims=True)
        km_ref[n:n + 1, :] = tot / MOBA_BLOCK
    gate = _nt_dot(q, km_ref[...].astype(BF16))
    lane = lax.broadcasted_iota(jnp.int32, gate.shape, 1)
    t = lax.broadcasted_iota(jnp.int32, gate.shape, 0) & (TOK_PAD - 1)
    earlier = (lane < (past_len + t) // MOBA_BLOCK) & (lane < n_blk)
    gate = jnp.where(earlier, gate, -jnp.inf)
    chosen = jnp.where(earlier & _rank_select(gate, n_blk, MOBA_TOPK), 1.0, 0.0)
    for j in range(n_pages):
        sj = _nt_dot(q, pages[j][:, 0:w].astype(BF16))
        n = j // ppb
        s_ref[:, j * PAGE_SIZE:(j + 1) * PAGE_SIZE] = jnp.where(chosen[:, n:n + 1] > 0.5, sj, NEG)
    sn = jnp.where(_new_row_mask(rows, n_new), _nt_dot(q, new_ref[:, 0:w].astype(BF16)), NEG)
    p, pn = _softmax_two(s_ref[...], sn)
    p = p.astype(BF16)
    o = jnp.dot(pn.astype(BF16), new_ref[:, w:].astype(BF16), preferred_element_type=F32)
    for j in range(n_pages):
        o = o + jnp.dot(p[:, j * PAGE_SIZE:(j + 1) * PAGE_SIZE], pages[j][:, w:].astype(BF16),
                        preferred_element_type=F32)
    for h in range(MOBA_HEADS):
        o_ref[:, h * HEAD_DIM:(h + 1) * HEAD_DIM] = o[h * TOK_PAD:(h + 1) * TOK_PAD, h * HEAD_DIM:(h + 1) * HEAD_DIM]


def _head_tiles(a, n_heads, scale):
    b, n_new, _ = a.shape
    a = (a * scale).reshape(b, n_new, n_heads, HEAD_DIM)
    return _pad_rows(jnp.transpose(a, (0, 2, 1, 3)).reshape(b * n_heads, n_new, HEAD_DIM), TOK_PAD).reshape(
        b, n_heads, TOK_PAD, HEAD_DIM)


def decode_moba_attention(layer, page_table, cache, m_q, rows_new):
    b, n_new, _ = m_q.shape
    n_pages = page_table.shape[1]
    past_len = n_pages * PAGE_SIZE
    assert past_len % MOBA_BLOCK == 0 and n_new <= TOK_PAD and n_pages * PAGE_SIZE // MOBA_BLOCK <= NEW_PAD
    q = _head_tiles(m_q, MOBA_HEADS, ATTN_SCALE)
    place = jnp.eye(MOBA_HEADS, dtype=F32)
    q = (q[:, :, :, None, :] * place[None, :, None, :, None]).reshape(b, MOBA_HEADS * TOK_PAD, MOBA_HEADS * HEAD_DIM)
    rows = MOBA_HEADS * TOK_PAD
    width = cache.shape[-1]
    out = pl.pallas_call(
        functools.partial(_dec_moba_kernel, n_pages=n_pages, n_new=n_new, past_len=past_len),
        out_shape=jax.ShapeDtypeStruct((b, TOK_PAD, MOBA_HEADS * HEAD_DIM), F32),
        grid_spec=pltpu.PrefetchScalarGridSpec(
            num_scalar_prefetch=1, grid=(b,),
            in_specs=[_per_seq((rows, MOBA_HEADS * HEAD_DIM)), _per_seq((NEW_PAD, width))]
            + _page_specs(layer, n_pages, width),
            out_specs=_per_seq((TOK_PAD, MOBA_HEADS * HEAD_DIM)),
            scratch_shapes=[pltpu.VMEM((rows, past_len), F32), pltpu.VMEM((NEW_PAD, MOBA_HEADS * HEAD_DIM), F32)]),
        compiler_params=pltpu.CompilerParams(dimension_semantics=("arbitrary",), vmem_limit_bytes=VMEM_LIMIT),
        name="decode_moba_attention",
    )(page_table.reshape(-1), q.astype(BF16), _pad_rows(rows_new, NEW_PAD), *([cache] * n_pages))
    return out[:, :n_new]


def _dec_nsa_kernel(pt_ref, qn_ref, qr_ref, gate_ref, new_ref, wnew_ref, win_ref,
                    wc_ref, cb_ref, w2_ref, ov_ref, ex_ref, *rest, n_pages, n_new, past_len, win_pos0):
    pages = rest[:n_pages]
    o_ref, cmp_ref, slc_ref = rest[n_pages:]
    rows = NSA_HEADS * TOK_PAD
    n_chunk = past_len // CMP_STRIDE
    n_cmp = (past_len + n_new - CMP_LEN) // CMP_STRIDE + 1
    n_slc = -(-(past_len + n_new) // SLC_LEN)
    hid = cb_ref.shape[1] // 2
    for j in range(n_pages):
        cmp_ref[j * PAGE_SIZE:(j + 1) * PAGE_SIZE, :] = pages[j][:, 0:2 * HEAD_DIM]
        slc_ref[j * PAGE_SIZE:(j + 1) * PAGE_SIZE, :] = pages[j][:, 2 * HEAD_DIM:4 * HEAD_DIM]

    y = jnp.zeros((n_chunk, 4 * hid), F32)
    for r in range(CMP_STRIDE):
        xr = cmp_ref[pl.ds(r, n_chunk, stride=CMP_STRIDE), :].astype(BF16)
        y = y + jnp.dot(xr, wc_ref[r], preferred_element_type=F32)
    hk = y[:, 0:hid] + pltpu.roll(y[:, hid:2 * hid], n_chunk - 1, 0) + cb_ref[0:1, 0:hid]
    hv = y[:, 2 * hid:3 * hid] + pltpu.roll(y[:, 3 * hid:4 * hid], n_chunk - 1, 0) + cb_ref[0:1, hid:2 * hid]
    cc = jnp.dot(jax.nn.gelu(jnp.concatenate([hk, hv], axis=1)).astype(BF16), w2_ref[...],
                 preferred_element_type=F32).astype(BF16)

    t = lax.broadcasted_iota(jnp.int32, (rows, n_chunk), 0) & (TOK_PAD - 1)
    n_idx = lax.broadcasted_iota(jnp.int32, (rows, n_chunk), 1)
    ok = (n_idx * CMP_STRIDE + CMP_LEN - 1 <= past_len + t) & (n_idx < n_cmp)
    sc = jnp.where(ok, _nt_dot(qn_ref[...], cc), NEG)
    pc = jnp.exp(sc - jnp.max(sc, axis=-1, keepdims=True))
    pc = pc / jnp.sum(pc, axis=-1, keepdims=True)
    p_hi = pc.astype(BF16)
    p_lo = (pc - p_hi.astype(F32)).astype(BF16)
    o_cmp = jnp.dot(p_hi, cc, preferred_element_type=F32)[:, HEAD_DIM:]
    ov = ov_ref[...]
    imp = jnp.dot(p_hi, ov, preferred_element_type=F32) + jnp.dot(p_lo, ov, preferred_element_type=F32)
    imp = imp[0:TOK_PAD] + imp[TOK_PAD:2 * TOK_PAD] + imp[2 * TOK_PAD:3 * TOK_PAD] + imp[3 * TOK_PAD:4 * TOK_PAD]
    blk = lax.broadcasted_iota(jnp.int32, imp.shape, 1)
    cur = (past_len + lax.broadcasted_iota(jnp.int32, imp.shape, 0)) // SLC_LEN
    valid = blk <= cur
    forced = (blk == 0) | (blk == cur) | (blk == cur - 1)
    score = jnp.where(forced, jnp.inf, jnp.where(valid, imp, -jnp.inf))
    chosen = jnp.where(valid & _rank_select(score, n_slc, min(SLC_TOPN, n_slc)), 1.0, 0.0).astype(BF16)
    key_ok = jnp.dot(chosen, ex_ref[...], preferred_element_type=F32)
    key_ok = jnp.concatenate([key_ok] * NSA_HEADS, axis=0)
    new_mask = _new_row_mask(rows, n_new)

    qr = qr_ref[...]
    kv = slc_ref[...].astype(BF16)
    kv_new = new_ref[:, 2 * HEAD_DIM:4 * HEAD_DIM].astype(BF16)
    p, pn = _softmax_two(jnp.where(key_ok > 0.5, _nt_dot(qr, kv), NEG), jnp.where(new_mask, _nt_dot(qr, kv_new), NEG))
    o_slc = (jnp.dot(p.astype(BF16), kv, preferred_element_type=F32)
             + jnp.dot(pn.astype(BF16), kv_new, preferred_element_type=F32))[:, HEAD_DIM:]

    kv = win_ref[...].astype(BF16)
    kv_new = wnew_ref[...].astype(BF16)
    n_win = kv.shape[0]
    wpos = win_pos0 + lax.broadcasted_iota(jnp.int32, (rows, n_win), 1)
    qpos = past_len + (lax.broadcasted_iota(jnp.int32, (rows, n_win), 0) & (TOK_PAD - 1))
    ok = (wpos <= qpos) & (wpos > qpos - WINDOW)
    p, pn = _softmax_two(jnp.where(ok, _nt_dot(qr, kv), NEG), jnp.where(new_mask, _nt_dot(qr, kv_new), NEG))
    o_win = (jnp.dot(p.astype(BF16), kv, preferred_element_type=F32)
             + jnp.dot(pn.astype(BF16), kv_new, preferred_element_type=F32))[:, HEAD_DIM:]

    g = jax.nn.sigmoid(gate_ref[...])
    o = g[:, 0:1] * o_cmp + g[:, 1:2] * o_slc + g[:, 2:3] * o_win
    for h in range(NSA_HEADS):
        o_ref[:, h * HEAD_DIM:(h + 1) * HEAD_DIM] = o[h * TOK_PAD:(h + 1) * TOK_PAD]


def decode_nsa_attention(layer, page_table, cache, win_state, n_q, nq_r, n_gate, rows_new, rows_win,
                         cmp_pos, cmp_w1, cmp_w2):
    b, n_new, _ = n_q.shape
    n_pages = page_table.shape[1]
    past_len = n_pages * PAGE_SIZE
    n_win = win_state.shape[2]
    assert past_len % SLC_LEN == 0 and n_new < CMP_STRIDE and past_len >= CMP_LEN and n_new <= TOK_PAD
    n_chunk = past_len // CMP_STRIDE
    assert n_chunk == LANES, "compressed tokens are laid out on one lane tile"
    hid = cmp_w1.shape[-1]
    lane_pad = lambda a: jnp.pad(a, ((0, 0),) * (a.ndim - 1) + ((0, LANES - a.shape[-1]),))
    qn = lane_pad(_head_tiles(n_q, NSA_HEADS, ATTN_SCALE)).reshape(b, -1, LANES).astype(BF16)
    qr = lane_pad(_head_tiles(nq_r, NSA_HEADS, ATTN_SCALE)).reshape(b, -1, LANES).astype(BF16)
    g = jnp.transpose(n_gate.reshape(b, n_new, NSA_HEADS, 3), (0, 2, 1, 3))
    g = lane_pad(_pad_rows(g.reshape(b * NSA_HEADS, n_new, 3), TOK_PAD)).reshape(b, -1, LANES)
    w1 = cmp_w1.reshape(2, 2, CMP_STRIDE, HEAD_DIM, hid)
    zero = jnp.zeros((CMP_STRIDE, HEAD_DIM, 2 * hid), F32)
    top = jnp.concatenate([w1[0, 0], w1[0, 1], zero], axis=-1)
    bot = jnp.concatenate([zero, w1[1, 0], w1[1, 1]], axis=-1)
    wc = jnp.concatenate([top, bot], axis=1).astype(BF16)
    bias = jnp.concatenate([cmp_pos[0].reshape(1, -1) @ cmp_w1[0], cmp_pos[1].reshape(1, -1) @ cmp_w1[1]], -1)
    cb = jnp.concatenate([bias, jnp.zeros((7, 2 * hid), F32)], 0)
    zw = jnp.zeros((hid, HEAD_DIM), F32)
    w2 = jnp.concatenate([jnp.concatenate([cmp_w2[0], zw], 1), jnp.concatenate([zw, cmp_w2[1]], 1)], 0).astype(BF16)
    cmp_start = jnp.arange(n_chunk) * CMP_STRIDE
    slc_start = jnp.arange(LANES) * SLC_LEN
    ov = ((cmp_start[:, None] <= slc_start[None, :] + SLC_LEN - 1)
          & (cmp_start[:, None] + CMP_LEN - 1 >= slc_start[None, :])).astype(BF16)
    ex = (jnp.arange(past_len)[None, :] // SLC_LEN == jnp.arange(LANES)[:, None]).astype(BF16)
    rows = NSA_HEADS * TOK_PAD
    width = cache.shape[-1]
    out = pl.pallas_call(
        functools.partial(_dec_nsa_kernel, n_pages=n_pages, n_new=n_new, past_len=past_len,
                          win_pos0=past_len - n_win),
        out_shape=jax.ShapeDtypeStruct((b, TOK_PAD, NSA_HEADS * HEAD_DIM), F32),
        grid_spec=pltpu.PrefetchScalarGridSpec(
            num_scalar_prefetch=1, grid=(b,),
            in_specs=[_per_seq((rows, LANES)), _per_seq((rows, LANES)), _per_seq((rows, LANES)),
                      _per_seq((NEW_PAD, width)), _per_seq((NEW_PAD, 2 * HEAD_DIM)),
                      pl.BlockSpec((None, None, n_win, 2 * HEAD_DIM), lambda bi, pt: (layer, bi, 0, 0)),
                      _shared(wc.shape), _shared(cb.shape), _shared(w2.shape), _shared(ov.shape), _shared(ex.shape)]
            + _page_specs(layer, n_pages, width),
            out_specs=_per_seq((TOK_PAD, NSA_HEADS * HEAD_DIM)),
            scratch_shapes=[pltpu.VMEM((past_len, 2 * HEAD_DIM), F32), pltpu.VMEM((past_len, 2 * HEAD_DIM), F32)]),
        compiler_params=pltpu.CompilerParams(dimension_semantics=("arbitrary",), vmem_limit_bytes=VMEM_LIMIT),
        name="decode_nsa_attention",
    )(page_table.reshape(-1), qn, qr, g, _pad_rows(rows_new, NEW_PAD), _pad_rows(rows_win, NEW_PAD), win_state,
      wc, cb, w2, ov, ex, *([cache] * n_pages))
    return out[:, :n_new]


def _rope(x, cos, sin):
    t, w = x.shape
    xh = x.reshape(t, w // HEAD_DIM, 2, HALF)
    x1, x2 = xh[:, :, 0], xh[:, :, 1]
    c, s = cos[:, None, :], sin[:, None, :]
    return jnp.stack([x1 * c - x2 * s, x1 * s + x2 * c], axis=2).reshape(t, w)


def _masked_softmax(s, mask):
    s = jnp.where(mask, s.astype(F32), -jnp.inf)
    m = jnp.max(s, -1, keepdims=True)
    m = jnp.where(jnp.isfinite(m), m, 0.0)
    p = jnp.where(mask, jnp.exp(s - m), 0.0)
    return p / jnp.maximum(jnp.sum(p, -1, keepdims=True), 1e-30)


def _compress(kv, pos_emb, w1, w2):
    b, l, _ = kv.shape
    n_chunk = l // CMP_STRIDE
    n_cmp = (l - CMP_LEN) // CMP_STRIDE + 1
    x = kv[:, :n_chunk * CMP_STRIDE].reshape(b * n_chunk, CMP_STRIDE * HEAD_DIM)
    half = CMP_STRIDE * HEAD_DIM
    w_cat = jnp.concatenate([w1[:half], w1[half:]], axis=1).astype(BF16)
    rows = x.shape[0]
    tm = 512 if rows % 512 == 0 else rows
    y = matmul(x, w_cat, tm, w_cat.shape[1]).reshape(b, n_chunk, 2, w1.shape[1])
    bias = pos_emb.reshape(1, -1) @ w1
    hid = y[:, :n_cmp, 0] + y[:, 1:n_cmp + 1, 1] + bias
    return jax.nn.gelu(hid) @ w2


def _nsa_select(nq, ck_c, cv_c, q_pos, n_slc):
    n_cmp = ck_c.shape[1]
    top_n = min(SLC_TOPN, n_slc)
    cmp_start = jnp.arange(n_cmp) * CMP_STRIDE
    cmp_end = cmp_start + CMP_LEN - 1
    slc_start = jnp.arange(n_slc) * SLC_LEN
    overlap = ((cmp_start[:, None] <= slc_start[None, :] + SLC_LEN - 1)
               & (cmp_end[:, None] >= slc_start[None, :])).astype(F32)
    s = jnp.einsum('bqjd,bnd->bjqn', nq, ck_c) * ATTN_SCALE
    p = _masked_softmax(s, (cmp_end[None, :] <= q_pos[:, None])[None, None])
    o_cmp = jnp.einsum('bjqn,bnd->bqjd', p, cv_c)
    imp = jnp.einsum('bjqn,nm->bqm', p, overlap)
    blk = jnp.arange(n_slc)
    cur = q_pos // SLC_LEN
    valid = blk[None, :] <= cur[:, None]
    forced = (blk[None, :] == 0) | (blk[None, :] == cur[:, None]) | (blk[None, :] == cur[:, None] - 1)
    score = jnp.where(forced, jnp.inf, jnp.where(valid, imp, -jnp.inf))
    _, sel = lax.top_k(score, top_n)
    chosen = jnp.any(sel[..., None] == blk, axis=-2)
    return o_cmp, chosen & valid


def _moba_select(mq, k_mean, q_pos):
    n_blk = k_mean.shape[2]
    n_sel = min(MOBA_TOPK, n_blk)
    blk = jnp.arange(n_blk)
    cur = q_pos // MOBA_BLOCK
    earlier = blk[None, :] < cur[:, None]
    gate = jnp.einsum('bqhd,bhnd->bhqn', mq, k_mean)
    gate = jnp.where(earlier, gate, -jnp.inf)
    _, sel = lax.top_k(gate, n_sel)
    chosen = jnp.any(sel[..., None] == blk, axis=-2)
    return (chosen & earlier) | (blk[None, :] == cur[:, None])


def _route(t, router_w, router_b):
    s = jax.nn.sigmoid((t @ router_w).astype(F32))
    sb = s + router_b.astype(F32)
    grp = sb.reshape(-1, N_EXPERT_GROUPS, N_EXPERTS // N_EXPERT_GROUPS)
    grp_score = jnp.sum(lax.top_k(grp, 2)[0], -1)
    _, top_g = lax.top_k(grp_score, TOPK_GROUPS)
    gmask = jnp.sum(jax.nn.one_hot(top_g, N_EXPERT_GROUPS, dtype=F32), -2) > 0
    emask = jnp.repeat(gmask, N_EXPERTS // N_EXPERT_GROUPS, axis=-1)
    _, top_e = lax.top_k(jnp.where(emask, sb, -jnp.inf), TOP_K)
    w = jnp.take_along_axis(s, top_e, -1)
    w = w / jnp.sum(w, -1, keepdims=True) * ROUTED_SCALE
    return jnp.einsum('tk,tke->te', w, jax.nn.one_hot(top_e, N_EXPERTS, dtype=F32))


def kernel(x_prompt, x_sample, cache_diff, cache_nsa, cache_moba, state_nsa_win, page_table, w_in, diff_lambda, diff_subln, nsa_cmp_pos, nsa_cmp_w1, nsa_cmp_w2, w_br_diff, w_br_nsa, w_br_moba, w_out, ln1_g, ln1_b, ln2_g, ln2_b, router_w, router_b, exp_w_gate, exp_w_up, exp_w_down, sh_w_gate, sh_w_up, sh_w_down):
    bp, sp, _ = x_prompt.shape
    bs, ss, _ = x_sample.shape
    tp, ts = bp * sp, bs * ss
    n_pages = page_table.shape[1]
    past_len = n_pages * PAGE_SIZE
    pos_p = jnp.arange(sp, dtype=jnp.int32)
    pos_s = past_len + jnp.arange(ss, dtype=jnp.int32)
    win_pos0 = past_len - state_nsa_win.shape[2]
    pos = jnp.concatenate([jnp.tile(pos_p, bp), jnp.tile(pos_s, bs)])
    inv = ROPE_THETA ** (-jnp.arange(HALF, dtype=F32) / HALF)
    ang = pos.astype(F32)[:, None] * inv[None, :]
    cos, sin = jnp.cos(ang), jnp.sin(ang)

    x = jnp.concatenate([x_prompt.reshape(tp, D_MODEL), x_sample.reshape(ts, D_MODEL)], 0)
    st_p = [[], [], [], []]
    st_s = [[], [], [], []]
    for l in range(DEPTH):
        lambda_init = 0.8 - 0.6 * math.exp(-0.3 * l)
        w_pad = jnp.concatenate([w_in[l][:, :C_MQ_SRC],
                                 jnp.zeros((D_MODEL, C_MQ - C_MQ_SRC), F32),
                                 w_in[l][:, C_MQ_SRC:]], axis=1).astype(BF16)
        proj = matmul(x, w_pad, 768, 768)

        da_q = _rope(proj[:, C_DAQ:C_DAK], cos, sin)
        da_k = _rope(proj[:, C_DAK:C_DAV], cos, sin)
        da_v = proj[:, C_DAV:C_NQ]
        n_q = proj[:, C_NQ:C_CK]
        nq_r = _rope(n_q, cos, sin)
        s_k = _rope(proj[:, C_SK:C_SV], cos, sin)
        w_k = _rope(proj[:, C_WK:C_WV], cos, sin)
        n_gate = proj[:, C_NG:C_NG + N_GATE]
        m_q = _rope(proj[:, C_MQ:C_MK], cos, sin)
        m_k = _rope(proj[:, C_MK:C_MV], cos, sin)
        m_v = proj[:, C_MV:C_MG]
        rows_diff = jnp.concatenate([da_k, da_v], -1)
        rows_nsa = jnp.concatenate([proj[:, C_CK:C_SK], s_k, proj[:, C_SV:C_WK]], -1)
        rows_moba = jnp.concatenate([m_k, m_v], -1)
        rows_win = jnp.concatenate([w_k, proj[:, C_WV:C_NG]], -1)

        lp_ = diff_lambda[l].astype(F32)
        lam = jnp.exp(jnp.sum(lp_[0] * lp_[1])) - jnp.exp(jnp.sum(lp_[2] * lp_[3])) + lambda_init

        def pr(a):
            return a[:tp].reshape(bp, sp, a.shape[-1])

        o_diff_p = diff_attention((pr(da_q) * ATTN_SCALE).astype(BF16), pr(da_k).astype(BF16),
                                  pr(da_v).astype(BF16), lam, diff_subln[l], lambda_init)
        ck_c = _compress(pr(proj[:, C_CK:C_CV]), nsa_cmp_pos[l, 0], nsa_cmp_w1[l, 0], nsa_cmp_w2[l, 0])
        cv_c = _compress(pr(proj[:, C_CV:C_SK]), nsa_cmp_pos[l, 1], nsa_cmp_w1[l, 1], nsa_cmp_w2[l, 1])
        o_cmp, selmask = _nsa_select(pr(n_q).reshape(bp, sp, NSA_HEADS, HEAD_DIM), ck_c, cv_c, pos_p,
                                     sp // SLC_LEN)
        nqr_b = (pr(nq_r) * ATTN_SCALE).astype(BF16)
        o_slc = nsa_attention(nqr_b, pr(s_k).astype(BF16), pr(proj[:, C_SV:C_WK]).astype(BF16),
                              selmask.astype(BF16))
        o_win = nsa_attention(nqr_b, pr(w_k).astype(BF16), pr(proj[:, C_WV:C_NG]).astype(BF16))
        ng = jax.nn.sigmoid(pr(n_gate).reshape(bp, sp, NSA_HEADS, 3))
        o_nsa_p = (ng[..., 0:1] * o_cmp + ng[..., 1:2] * o_slc.reshape(bp, sp, NSA_HEADS, HEAD_DIM)
                   + ng[..., 2:3] * o_win.reshape(bp, sp, NSA_HEADS, HEAD_DIM)).reshape(bp, sp, -1)
        mk_p = pr(m_k)
        k_mean = jnp.mean(mk_p.reshape(bp, sp // MOBA_BLOCK, MOBA_BLOCK, MOBA_HEADS, HEAD_DIM), axis=2)
        blkmask = _moba_select(pr(m_q).reshape(bp, sp, MOBA_HEADS, HEAD_DIM),
                               jnp.transpose(k_mean, (0, 2, 1, 3)), pos_p)
        o_moba_p = moba_attention((pr(m_q) * ATTN_SCALE).astype(BF16), mk_p.astype(BF16),
                                  pr(m_v).astype(BF16), blkmask.astype(BF16))

        def sm(a):
            return a[tp:].reshape(bs, ss, a.shape[-1])

        o_diff_s = decode_diff_attention(l, page_table, cache_diff, sm(da_q), sm(rows_diff),
                                         lam, diff_subln[l], lambda_init)
        o_nsa_s = decode_nsa_attention(l, page_table, cache_nsa, state_nsa_win, sm(n_q), sm(nq_r), sm(n_gate),
                                       sm(rows_nsa), sm(rows_win), nsa_cmp_pos[l], nsa_cmp_w1[l], nsa_cmp_w2[l])
        o_moba_s = decode_moba_attention(l, page_table, cache_moba, sm(m_q), sm(rows_moba))

        def both(a_p, a_s):
            return jnp.concatenate([a_p.reshape(tp, -1), a_s.reshape(ts, -1)], 0)

        x = merge_project_norm(x, both(o_diff_p, o_diff_s), both(o_nsa_p, o_nsa_s), both(o_moba_p, o_moba_s),
                               proj, w_br_diff[l].astype(BF16), w_br_nsa[l].astype(BF16),
                               w_br_moba[l].astype(BF16), w_out[l].astype(BF16), ln1_g[l], ln1_b[l])

        gate = _route(x, router_w[l], router_b[l])
        gate = jnp.concatenate([gate, jnp.ones((tp + ts, D_SHARED // D_EXPERT), F32),
                                jnp.zeros((tp + ts, LANES - N_EXPERTS - D_SHARED // D_EXPERT), F32)], -1)
        wg = jnp.concatenate([jnp.transpose(exp_w_gate[l], (1, 0, 2)).reshape(D_MODEL, -1), sh_w_gate[l]], 1)
        wu = jnp.concatenate([jnp.transpose(exp_w_up[l], (1, 0, 2)).reshape(D_MODEL, -1), sh_w_up[l]], 1)
        wd = jnp.concatenate([exp_w_down[l].reshape(-1, D_MODEL), sh_w_down[l]], 0)
        x = moe_norm(x, gate, wg.astype(BF16), wu.astype(BF16), wd.astype(BF16), ln2_g[l], ln2_b[l])

        for lst, val in zip(st_p, (rows_diff, rows_nsa, rows_moba)):
            lst.append(pr(val))
        n_keep = min(WINDOW, sp)
        st_p[3].append(pr(rows_win)[:, sp - n_keep:])
        for lst, val in zip(st_s, (rows_diff, rows_nsa, rows_moba)):
            lst.append(sm(val))
        full_win = jnp.concatenate([state_nsa_win[l], sm(rows_win)], 1)
        n_keep = min(WINDOW, past_len + ss)
        st_s[3].append(full_win[:, full_win.shape[1] - n_keep:])

    outs_p = [jnp.stack(a, 0) for a in st_p]
    outs_s = [jnp.stack(a, 0) for a in st_s]
    return (x[:tp].reshape(bp, sp, D_MODEL), x[tp:].reshape(bs, ss, D_MODEL),
            outs_p[0], outs_s[0], outs_p[1], outs_s[1], outs_p[2], outs_s[2], outs_p[3], outs_s[3])
```

```python
import functools
import math

import jax
import jax.numpy as jnp
from jax import lax
from jax.experimental import pallas as pl
from jax.experimental.pallas import tpu as pltpu

F32 = jnp.float32
BF16 = jnp.bfloat16

D_MODEL = 1024
DEPTH = 2
PAGE_SIZE = 128
HEAD_DIM = 64
HALF = HEAD_DIM // 2
ATTN_SCALE = HEAD_DIM ** -0.5
ROPE_THETA = 10000.0
DIFF_HEADS = 4
NSA_HEADS = 4
CMP_LEN = 32
CMP_STRIDE = 16
SLC_LEN = 64
SLC_TOPN = 16
WINDOW = 512
MOBA_HEADS = 4
MOBA_BLOCK = 256
MOBA_TOPK = 3
N_EXPERTS = 64
N_EXPERT_GROUPS = 8
TOPK_GROUPS = 4
TOP_K = 6
D_EXPERT = 128
D_SHARED = 256
ROUTED_SCALE = 2.5
LN_EPS = 1e-5
RMS_EPS = 1e-5
DEEPNORM_ALPHA = (2 * DEPTH) ** 0.25

C_DAQ, C_DAK, C_DAV, C_NQ = 0, 512, 1024, 1536
C_CK, C_CV, C_SK, C_SV, C_WK, C_WV, C_NG = 1792, 1856, 1920, 1984, 2048, 2112, 2176
N_GATE = 3 * NSA_HEADS
C_MQ_SRC = C_NG + N_GATE
C_MQ, C_MK, C_MV, C_MG = 2304, 2560, 2816, 3072
N_IN_PAD = C_MG + 3 * D_MODEL

LANES = 128
NEG = -1e30
VMEM_LIMIT = 56 * 1024 * 1024


def _nt_dot(a, b):
    return lax.dot_general(a, b, (((1,), (1,)), ((), ())), preferred_element_type=F32)


def _mm_kernel(x_ref, w_ref, o_ref):
    o_ref[...] = jnp.dot(x_ref[...].astype(BF16), w_ref[...],
                         preferred_element_type=F32).astype(o_ref.dtype)


def matmul(x, w, tm, tn, out_dtype=F32):
    m, k = x.shape
    n = w.shape[1]
    assert m % tm == 0 and n % tn == 0, (x.shape, w.shape, tm, tn)
    return pl.pallas_call(
        _mm_kernel,
        out_shape=jax.ShapeDtypeStruct((m, n), out_dtype),
        grid=(m // tm, n // tn),
        in_specs=[pl.BlockSpec((tm, k), lambda i, j: (i, 0)),
                  pl.BlockSpec((k, tn), lambda i, j: (0, j))],
        out_specs=pl.BlockSpec((tm, tn), lambda i, j: (i, j)),
        compiler_params=pltpu.CompilerParams(
            dimension_semantics=("parallel", "arbitrary"), vmem_limit_bytes=VMEM_LIMIT),
        name="matmul",
    )(x, w)


def _project_kernel(x_ref, w_ref, cos_ref, sa_ref, sb_ref,
                    rd_ref, rn_ref, rm_ref, rw_ref, mg_ref, ng_ref,
                    qd_ref, kd_ref, vd_ref, nq_ref, nqr_ref, ss_ref, ww_ref, mq_ref, mk_ref, mv_ref):
    xb = x_ref[...].astype(BF16)
    cos, sa, sb = cos_ref[...], sa_ref[...], sb_ref[...]

    def seg(a, b):
        return jnp.dot(xb, w_ref[:, a:b], preferred_element_type=F32)

    def rope(y):
        tiles = []
        for t in range(y.shape[1] // LANES):
            yt = y[:, t * LANES:(t + 1) * LANES]
            tiles.append(yt * cos + pltpu.roll(yt, LANES - HALF, 1) * sa + pltpu.roll(yt, HALF, 1) * sb)
        return tiles[0] if len(tiles) == 1 else jnp.concatenate(tiles, axis=1)

    first = lax.broadcasted_iota(jnp.int32, (x_ref.shape[0], LANES), 1) < HEAD_DIM
    qd_ref[...] = (rope(seg(C_DAQ, C_DAK)) * ATTN_SCALE).astype(BF16)
    k = rope(seg(C_DAK, C_DAV))
    v = seg(C_DAV, C_NQ)
    rd_ref[:, 0:C_DAV - C_DAK] = k
    rd_ref[:, C_DAV - C_DAK:] = v
    kd_ref[...] = k.astype(BF16)
    vd_ref[...] = v.astype(BF16)
    nq = seg(C_NQ, C_CK)
    nq_ref[...] = (nq * ATTN_SCALE).astype(BF16)
    nqr_ref[...] = (rope(nq) * ATTN_SCALE).astype(BF16)
    y = seg(C_CK, C_NG)
    ss = jnp.where(first, rope(y[:, LANES:2 * LANES]), y[:, LANES:2 * LANES])
    ww = jnp.where(first, rope(y[:, 2 * LANES:3 * LANES]), y[:, 2 * LANES:3 * LANES])
    rn_ref[:, 0:LANES] = y[:, 0:LANES]
    rn_ref[:, LANES:] = ss
    rw_ref[...] = ww
    ss_ref[...] = ss.astype(BF16)
    ww_ref[...] = ww.astype(BF16)
    ng_ref[...] = seg(C_NG, C_MQ)
    mq_ref[...] = (rope(seg(C_MQ, C_MK)) * ATTN_SCALE).astype(BF16)
    k = rope(seg(C_MK, C_MV))
    v = seg(C_MV, C_MG)
    rm_ref[:, 0:C_MV - C_MK] = k
    rm_ref[:, C_MV - C_MK:] = v
    mk_ref[...] = k.astype(BF16)
    mv_ref[...] = v.astype(BF16)
    for t in range(3):
        mg_ref[:, t * D_MODEL:(t + 1) * D_MODEL] = seg(C_MG + t * D_MODEL, C_MG + (t + 1) * D_MODEL)


def project_inputs(x, w_pad, cos, sin, tm=256):
    t = x.shape[0]
    zero = jnp.zeros_like(sin)
    cos_t = jnp.tile(cos, (1, LANES // HALF))
    sa = jnp.concatenate([-sin, zero, -sin, zero], axis=1)
    sb = jnp.concatenate([zero, sin, zero, sin], axis=1)
    names_f32 = (("rows_diff", 1024), ("rows_nsa", 256), ("rows_moba", 512), ("rows_win", 128),
                 ("merge_gate", 3 * D_MODEL), ("nsa_gate", LANES))
    names_bf16 = (("q_diff", 512), ("k_diff", 512), ("v_diff", 512), ("q_nsa", 256), ("q_nsa_rot", 256),
                  ("kv_slc", 128), ("kv_win", 128), ("q_moba", 256), ("k_moba", 256), ("v_moba", 256))
    row = lambda i: (i, 0)
    outs = pl.pallas_call(
        _project_kernel,
        out_shape=[jax.ShapeDtypeStruct((t, n), F32) for _, n in names_f32]
        + [jax.ShapeDtypeStruct((t, n), BF16) for _, n in names_bf16],
        grid=(t // tm,),
        in_specs=[pl.BlockSpec((tm, D_MODEL), row),
                  pl.BlockSpec(w_pad.shape, lambda i: (0, 0), pipeline_mode=pl.Buffered(1)),
                  pl.BlockSpec((tm, LANES), row), pl.BlockSpec((tm, LANES), row), pl.BlockSpec((tm, LANES), row)],
        out_specs=[pl.BlockSpec((tm, n), row) for _, n in names_f32 + names_bf16],
        compiler_params=pltpu.CompilerParams(dimension_semantics=("parallel",), vmem_limit_bytes=VMEM_LIMIT),
        name="project_inputs",
    )(x, w_pad, cos_t, sa, sb)
    return dict(zip([n for n, _ in names_f32 + names_bf16], outs))


def _online_update(s, v, m_ref, l_ref, acc_ref):
    tk = s.shape[1]
    dv = acc_ref.shape[-1]
    m_prev = m_ref[...]
    m_next = jnp.maximum(m_prev, jnp.max(s, axis=-1, keepdims=True))
    alpha = jnp.exp(m_prev - m_next)
    p = jnp.exp(s - jnp.concatenate([m_next] * (tk // LANES), axis=1))
    l_ref[...] = alpha * l_ref[...] + jnp.sum(p, axis=-1, keepdims=True)
    m_ref[...] = m_next
    acc_ref[...] = acc_ref[...] * alpha[:, :dv] + jnp.dot(p.astype(BF16), v, preferred_element_type=F32)


def _init_state(*refs):
    for m_ref, l_ref, acc_ref in zip(refs[0::3], refs[1::3], refs[2::3]):
        m_ref[...] = jnp.full(m_ref.shape, NEG, F32)
        l_ref[...] = jnp.zeros(l_ref.shape, F32)
        acc_ref[...] = jnp.zeros(acc_ref.shape, F32)


def _normalised(l_ref, acc_ref):
    dv = acc_ref.shape[-1]
    return acc_ref[...] / jnp.maximum(l_ref[...], 1e-30)[:, :dv]


def _diff_attn_kernel(q_ref, k_ref, v_ref, par_ref, o_ref,
                      m0, l0, a0, m1, l1, a1, *, tq, tk, out_scale):
    qi = pl.program_id(2)
    q = q_ref[...]
    lane = lax.broadcasted_iota(jnp.int32, q.shape, 1)
    zero = jnp.zeros_like(q)
    qa = jnp.where(lane < HEAD_DIM, q, zero)
    qb = jnp.where(lane >= HEAD_DIM, q, zero)
    q0 = qi * tq
    row = q0 + lax.broadcasted_iota(jnp.int32, (tq, tk), 0)
    col = lax.broadcasted_iota(jnp.int32, (tq, tk), 1)
    _init_state(m0, l0, a0, m1, l1, a1)

    def step(j, causal):
        start = pl.multiple_of(j * tk, tk)
        k = k_ref[pl.ds(start, tk), :]
        v = v_ref[pl.ds(start, tk), :]
        sa, sb = _nt_dot(qa, k), _nt_dot(qb, k)
        if causal:
            mask = (col + start) <= row
            sa, sb = jnp.where(mask, sa, NEG), jnp.where(mask, sb, NEG)
        _online_update(sa, v, m0, l0, a0)
        _online_update(sb, v, m1, l1, a1)

    n_full = (q0 + 1) // tk
    lax.fori_loop(0, n_full, lambda j, c: (step(j, False), c)[1], 0)
    lax.fori_loop(n_full, (q0 + tq + tk - 1) // tk, lambda j, c: (step(j, True), c)[1], 0)
    lam = par_ref[0:1, :]
    g = par_ref[1:2, :]
    o = _normalised(l0, a0) - lam * _normalised(l1, a1)
    o = o * lax.rsqrt(jnp.mean(o * o, axis=-1, keepdims=True) + RMS_EPS) * g
    o_ref[...] = (o * out_scale).astype(o_ref.dtype)


def diff_attention(q, k, v, n_batch, s, lam, subln_g, lambda_init, tq=256, tk=512):
    nq = s // tq
    par = jnp.concatenate([jnp.broadcast_to(lam.astype(F32), (1, LANES)),
                           subln_g.astype(F32).reshape(1, LANES),
                           jnp.zeros((6, LANES), F32)], 0)
    kern = functools.partial(_diff_attn_kernel, tq=tq, tk=tk, out_scale=1.0 - lambda_init)
    st = [pltpu.VMEM((tq, LANES), F32)] * 6
    return pl.pallas_call(
        kern,
        out_shape=jax.ShapeDtypeStruct((n_batch * s, DIFF_HEADS * LANES), F32),
        grid=(n_batch, DIFF_HEADS, nq),
        in_specs=[pl.BlockSpec((tq, LANES), lambda bi, h, i: (bi * nq + i, h)),
                  pl.BlockSpec((s, LANES), lambda bi, h, i: (bi, h)),
                  pl.BlockSpec((s, LANES), lambda bi, h, i: (bi, h)),
                  pl.BlockSpec((8, LANES), lambda bi, h, i: (0, 0))],
        out_specs=pl.BlockSpec((tq, LANES), lambda bi, h, i: (bi * nq + i, h)),
        scratch_shapes=st,
        compiler_params=pltpu.CompilerParams(
            dimension_semantics=("parallel", "parallel", "arbitrary"), vmem_limit_bytes=VMEM_LIMIT),
        name="diff_attention",
    )(q, k, v, par)


def _rank_select(score, n_candidates, top_n):
    lane = lax.broadcasted_iota(jnp.int32, score.shape, 1)
    rank = jnp.zeros(score.shape, F32)
    for c in range(n_candidates):
        col = score[:, c:c + 1]
        ahead = (col > score) | ((col == score) & (c < lane))
        rank = rank + jnp.where(ahead, 1.0, 0.0)
    return rank < top_n


def _stack_heads(q_ref, tq):
    first = lax.broadcasted_iota(jnp.int32, (tq, LANES), 1) < HEAD_DIM
    tiles = []
    for t in range(NSA_HEADS // 2):
        pair = q_ref[:, t * LANES:(t + 1) * LANES].astype(F32)
        tiles.append(jnp.where(first, pair, 0.0))
        tiles.append(jnp.where(first, pltpu.roll(pair, HEAD_DIM, 1), 0.0))
    return jnp.concatenate(tiles, axis=0).astype(BF16)


def _nsa_attn_kernel(qn_ref, qr_ref, cc_ref, ss_ref, ww_ref, gate_ref, ov_ref, e_ref, o_ref, m, l, acc,
                     *, tq, tk, tkw, n_cmp, n_slc):
    qi = pl.program_id(1)
    q0 = qi * tq
    nh = NSA_HEADS
    r = nh * tq
    qn = _stack_heads(qn_ref, tq)
    qr = _stack_heads(qr_ref, tq)

    cc = cc_ref[...]
    n_pad = cc.shape[0]
    qpos = q0 + lax.broadcasted_iota(jnp.int32, (tq, n_pad), 0)
    qpos = jnp.concatenate([qpos] * nh, axis=0)
    n_idx = lax.broadcasted_iota(jnp.int32, (r, n_pad), 1)
    ok = (n_idx * CMP_STRIDE + CMP_LEN - 1 <= qpos) & (n_idx < n_cmp)
    sc = jnp.where(ok, _nt_dot(qn, cc), NEG)
    pc = jnp.where(ok, jnp.exp(sc - jnp.max(sc, axis=-1, keepdims=True)), 0.0)
    pc = pc / jnp.maximum(jnp.sum(pc, axis=-1, keepdims=True), 1e-30)
    p_hi = pc.astype(BF16)
    p_lo = (pc - p_hi.astype(F32)).astype(BF16)
    o_cmp = jnp.dot(p_hi, cc, preferred_element_type=F32)
    ov = ov_ref[...]
    imp = jnp.dot(p_hi, ov, preferred_element_type=F32) + jnp.dot(p_lo, ov, preferred_element_type=F32)
    imp = imp[0:tq] + imp[tq:2 * tq] + imp[2 * tq:3 * tq] + imp[3 * tq:4 * tq]
    blk = lax.broadcasted_iota(jnp.int32, imp.shape, 1)
    cur = (q0 + lax.broadcasted_iota(jnp.int32, imp.shape, 0)) // SLC_LEN
    valid = blk <= cur
    forced = (blk == 0) | (blk == cur) | (blk == cur - 1)
    score = jnp.where(forced, jnp.inf, jnp.where(valid, imp, -jnp.inf))
    chosen = jnp.where(valid & _rank_select(score, n_slc, min(SLC_TOPN, n_slc)), 1.0, 0.0).astype(BF16)
    chosen = jnp.concatenate([chosen] * nh, axis=0)

    def sweep(kv_ref, tile, lo, hi, mask_fn):
        _init_state(m, l, acc)

        def body(j, c):
            start = pl.multiple_of(j * tile, tile)
            kv = kv_ref[pl.ds(start, tile), :]
            _online_update(jnp.where(mask_fn(j, start), _nt_dot(qr, kv), NEG), kv, m, l, acc)
            return c

        lax.fori_loop(lo, hi, body, 0)
        return _normalised(l, acc)

    row = jnp.concatenate([q0 + lax.broadcasted_iota(jnp.int32, (tq, tk), 0)] * nh, axis=0)
    col = lax.broadcasted_iota(jnp.int32, (r, tk), 1)
    o_slc = sweep(ss_ref, tk, 0, (q0 + tq + tk - 1) // tk,
                  lambda j, start: ((col + start) <= row)
                  & (jnp.dot(chosen, e_ref[j], preferred_element_type=F32) > 0.5))
    row_w = jnp.concatenate([q0 + lax.broadcasted_iota(jnp.int32, (tq, tkw), 0)] * nh, axis=0)
    col_w = lax.broadcasted_iota(jnp.int32, (r, tkw), 1)
    o_win = sweep(ww_ref, tkw, jnp.maximum(q0 - WINDOW + 1, 0) // tkw, (q0 + tq + tkw - 1) // tkw,
                  lambda j, start: ((col_w + start) <= row_w) & ((col_w + start) > row_w - WINDOW))

    g = jax.nn.sigmoid(gate_ref[...])
    first = lax.broadcasted_iota(jnp.int32, (tq, LANES), 1) < HEAD_DIM
    mixed = []
    for h in range(nh):
        rows = slice(h * tq, (h + 1) * tq)
        mixed.append(g[:, 3 * h:3 * h + 1] * o_cmp[rows] + g[:, 3 * h + 1:3 * h + 2] * o_slc[rows]
                     + g[:, 3 * h + 2:3 * h + 3] * o_win[rows])
    for t in range(nh // 2):
        o_ref[:, t * LANES:(t + 1) * LANES] = jnp.where(first, pltpu.roll(mixed[2 * t], HEAD_DIM, 1), mixed[2 * t + 1])


def _block_expander(n_chunks, n_rows, tk, block_len):
    key = jnp.arange(n_chunks)[:, None, None] * tk + jnp.arange(tk)[None, None, :]
    return (key // block_len == jnp.arange(n_rows)[None, :, None]).astype(BF16)


def nsa_attention(qn, qr, cc, ss, ww, gate, n_batch, s, tq=128, tk=512, tkw=256):
    nq = s // tq
    n_cmp = cc.shape[1]
    n_pad = -(-n_cmp // LANES) * LANES
    cc = jnp.pad(cc, ((0, 0), (0, n_pad - n_cmp), (0, 0)))
    n_slc = s // SLC_LEN
    assert n_slc <= LANES
    cmp_start = jnp.arange(n_pad) * CMP_STRIDE
    slc_start = jnp.arange(LANES) * SLC_LEN
    ov = ((cmp_start[:, None] <= slc_start[None, :] + SLC_LEN - 1)
          & (cmp_start[:, None] + CMP_LEN - 1 >= slc_start[None, :])).astype(BF16)
    expander = _block_expander(s // tk, LANES, tk, SLC_LEN)
    r = NSA_HEADS * tq
    row = lambda bi, i: (bi * nq + i, 0)
    return pl.pallas_call(
        functools.partial(_nsa_attn_kernel, tq=tq, tk=tk, tkw=tkw, n_cmp=n_cmp, n_slc=n_slc),
        out_shape=jax.ShapeDtypeStruct((n_batch * s, NSA_HEADS * HEAD_DIM), F32),
        grid=(n_batch, nq),
        in_specs=[pl.BlockSpec((tq, NSA_HEADS * HEAD_DIM), row),
                  pl.BlockSpec((tq, NSA_HEADS * HEAD_DIM), row),
                  pl.BlockSpec((None, n_pad, LANES), lambda bi, i: (bi, 0, 0)),
                  pl.BlockSpec((s, LANES), lambda bi, i: (bi, 0)),
                  pl.BlockSpec((s, LANES), lambda bi, i: (bi, 0)),
                  pl.BlockSpec((tq, LANES), row),
                  pl.BlockSpec(ov.shape, lambda bi, i: (0, 0)),
                  pl.BlockSpec(expander.shape, lambda bi, i: (0, 0, 0))],
        out_specs=pl.BlockSpec((tq, NSA_HEADS * HEAD_DIM), row),
        scratch_shapes=[pltpu.VMEM((r, LANES), F32), pltpu.VMEM((r, LANES), F32), pltpu.VMEM((r, LANES), F32)],
        compiler_params=pltpu.CompilerParams(
            dimension_semantics=("parallel", "arbitrary"), vmem_limit_bytes=VMEM_LIMIT),
        name="nsa_attention",
    )(qn, qr, cc, ss, ww, gate, ov, expander)


def _moba_attn_kernel(q_ref, k_ref, v_ref, km_ref, e_ref, o_ref, m0, l0, a0, m1, l1, a1, *, tq, tk, n_blk):
    qi = pl.program_id(2)
    q = q_ref[...]
    lane = lax.broadcasted_iota(jnp.int32, q.shape, 1)
    zero = jnp.zeros_like(q)
    qa = jnp.where(lane < HEAD_DIM, q, zero)
    qb = jnp.where(lane >= HEAD_DIM, q, zero)
    q0 = qi * tq
    row = q0 + lax.broadcasted_iota(jnp.int32, (tq, tk), 0)
    col = lax.broadcasted_iota(jnp.int32, (tq, tk), 1)

    km = km_ref[...]
    blk = lax.broadcasted_iota(jnp.int32, (tq, km.shape[0]), 1)
    cur = (q0 + lax.broadcasted_iota(jnp.int32, (tq, km.shape[0]), 0)) // MOBA_BLOCK
    earlier = blk < cur

    def block_mask(qh):
        gate = jnp.where(earlier, _nt_dot(qh, km), -jnp.inf)
        chosen = (earlier & _rank_select(gate, n_blk, min(MOBA_TOPK, n_blk))) | (blk == cur)
        return jnp.where(chosen, 1.0, 0.0).astype(BF16)

    bm0 = block_mask(qa)
    bm1 = block_mask(qb)
    _init_state(m0, l0, a0, m1, l1, a1)

    def body(j, carry):
        start = pl.multiple_of(j * tk, tk)
        k = k_ref[pl.ds(start, tk), :]
        v = v_ref[pl.ds(start, tk), :]
        causal = (col + start) <= row
        e = e_ref[j]
        mask0 = causal & (jnp.dot(bm0, e, preferred_element_type=F32) > 0.5)
        mask1 = causal & (jnp.dot(bm1, e, preferred_element_type=F32) > 0.5)
        _online_update(jnp.where(mask0, _nt_dot(qa, k), NEG), v, m0, l0, a0)
        _online_update(jnp.where(mask1, _nt_dot(qb, k), NEG), v, m1, l1, a1)
        return carry

    lax.fori_loop(0, (q0 + tq + tk - 1) // tk, body, 0)
    o_ref[...] = jnp.where(lane < HEAD_DIM, _normalised(l0, a0), _normalised(l1, a1)).astype(o_ref.dtype)


def moba_attention(q, k, v, k_mean, n_batch, s, tq=256, tk=512):
    nq = s // tq
    n_blk = k_mean.shape[1]
    n_pad = -(-n_blk // 16) * 16
    km = jnp.pad(k_mean, ((0, 0), (0, n_pad - n_blk), (0, 0))).astype(BF16)
    expander = _block_expander(s // tk, n_pad, tk, MOBA_BLOCK)
    st = [pltpu.VMEM((tq, LANES), F32)] * 6
    return pl.pallas_call(
        functools.partial(_moba_attn_kernel, tq=tq, tk=tk, n_blk=n_blk),
        out_shape=jax.ShapeDtypeStruct((n_batch * s, MOBA_HEADS * HEAD_DIM), F32),
        grid=(n_batch, MOBA_HEADS // 2, nq),
        in_specs=[pl.BlockSpec((tq, LANES), lambda bi, h, i: (bi * nq + i, h)),
                  pl.BlockSpec((s, LANES), lambda bi, h, i: (bi, h)),
                  pl.BlockSpec((s, LANES), lambda bi, h, i: (bi, h)),
                  pl.BlockSpec((None, n_pad, LANES), lambda bi, h, i: (bi, 0, h)),
                  pl.BlockSpec(expander.shape, lambda bi, h, i: (0, 0, 0))],
        out_specs=pl.BlockSpec((tq, LANES), lambda bi, h, i: (bi * nq + i, h)),
        scratch_shapes=st,
        compiler_params=pltpu.CompilerParams(
            dimension_semantics=("parallel", "parallel", "arbitrary"), vmem_limit_bytes=VMEM_LIMIT),
        name="moba_attention",
    )(q, k, v, km, expander)


def _layer_norm(z, g, b):
    mu = jnp.mean(z, axis=-1, keepdims=True)
    zc = z - mu
    var = jnp.mean(zc * zc, axis=-1, keepdims=True)
    return zc * lax.rsqrt(var + LN_EPS) * g + b


def _merge_kernel(x_ref, od_ref, on_ref, om_ref, g0_ref, g1_ref, g2_ref,
                  wd_ref, wn_ref, wm_ref, wo_ref, ln_ref, o_ref):
    def branch(o_r, w_r, g_r):
        y = jnp.dot(o_r[...].astype(BF16), w_r[...], preferred_element_type=F32)
        return jax.nn.sigmoid(g_r[...]) * y

    merged = branch(od_ref, wd_ref, g0_ref) + branch(on_ref, wn_ref, g1_ref) + branch(om_ref, wm_ref, g2_ref)
    y = jnp.dot(merged.astype(BF16), wo_ref[...], preferred_element_type=F32)
    z = DEEPNORM_ALPHA * x_ref[...] + y
    o_ref[...] = _layer_norm(z, ln_ref[0:1, :], ln_ref[1:2, :])


def merge_project_norm(x, o_diff, o_nsa, o_moba, merge_gate, w_d, w_n, w_m, w_o, ln_g, ln_b, tm=256):
    t = x.shape[0]
    ln = jnp.concatenate([ln_g.reshape(1, -1), ln_b.reshape(1, -1), jnp.zeros((6, D_MODEL), F32)], 0)
    gate_blk = 0
    proj = merge_gate
    row = lambda i: (i, 0)
    full = lambda i: (0, 0)
    return pl.pallas_call(
        _merge_kernel,
        out_shape=jax.ShapeDtypeStruct((t, D_MODEL), F32),
        grid=(t // tm,),
        in_specs=[pl.BlockSpec((tm, D_MODEL), row),
                  pl.BlockSpec((tm, o_diff.shape[1]), row),
                  pl.BlockSpec((tm, o_nsa.shape[1]), row),
                  pl.BlockSpec((tm, o_moba.shape[1]), row),
                  pl.BlockSpec((tm, D_MODEL), lambda i: (i, gate_blk)),
                  pl.BlockSpec((tm, D_MODEL), lambda i: (i, gate_blk + 1)),
                  pl.BlockSpec((tm, D_MODEL), lambda i: (i, gate_blk + 2)),
                  pl.BlockSpec(w_d.shape, full), pl.BlockSpec(w_n.shape, full),
                  pl.BlockSpec(w_m.shape, full), pl.BlockSpec(w_o.shape, full),
                  pl.BlockSpec((8, D_MODEL), full)],
        out_specs=pl.BlockSpec((tm, D_MODEL), row),
        compiler_params=pltpu.CompilerParams(
            dimension_semantics=("parallel",), vmem_limit_bytes=VMEM_LIMIT),
        name="merge_project_norm",
    )(x, o_diff, o_nsa, o_moba, proj, proj, proj, w_d, w_n, w_m, w_o, ln)


def _moe_kernel(x_ref, ghi_ref, glo_ref, e_ref, wg_ref, wu_ref, wd_ref, ln_ref, o_ref, acc_ref):
    f = pl.program_id(1)

    @pl.when(f == 0)
    def _():
        acc_ref[...] = jnp.zeros(acc_ref.shape, F32)

    xb = x_ref[...].astype(BF16)
    hg = jnp.dot(xb, wg_ref[...], preferred_element_type=F32)
    hu = jnp.dot(xb, wu_ref[...], preferred_element_type=F32)
    e = e_ref[...]
    gate = (jnp.dot(ghi_ref[...], e, preferred_element_type=F32)
            + jnp.dot(glo_ref[...], e, preferred_element_type=F32))
    h = jax.nn.silu(hg) * hu * gate
    acc_ref[...] += jnp.dot(h.astype(BF16), wd_ref[...], preferred_element_type=F32)

    @pl.when(f == pl.num_programs(1) - 1)
    def _():
        z = DEEPNORM_ALPHA * x_ref[...] + acc_ref[...]
        o_ref[...] = _layer_norm(z, ln_ref[0:1, :], ln_ref[1:2, :])


def moe_norm(x, gate, wg, wu, wd, ln_g, ln_b, tm=768, tf=768):
    t = x.shape[0]
    f_tot = wg.shape[1]
    n_col = gate.shape[1]
    ghi = gate.astype(BF16)
    glo = (gate - ghi.astype(F32)).astype(BF16)
    expander = (jnp.arange(f_tot)[None, :] // D_EXPERT == jnp.arange(n_col)[:, None]).astype(BF16)
    ln = jnp.concatenate([ln_g.reshape(1, -1), ln_b.reshape(1, -1), jnp.zeros((6, D_MODEL), F32)], 0)
    return pl.pallas_call(
        _moe_kernel,
        out_shape=jax.ShapeDtypeStruct((t, D_MODEL), F32),
        grid=(t // tm, f_tot // tf),
        in_specs=[pl.BlockSpec((tm, D_MODEL), lambda i, f: (i, 0)),
                  pl.BlockSpec((tm, n_col), lambda i, f: (i, 0)),
                  pl.BlockSpec((tm, n_col), lambda i, f: (i, 0)),
                  pl.BlockSpec((n_col, tf), lambda i, f: (0, f)),
                  pl.BlockSpec((D_MODEL, tf), lambda i, f: (0, f)),
                  pl.BlockSpec((D_MODEL, tf), lambda i, f: (0, f)),
                  pl.BlockSpec((tf, D_MODEL), lambda i, f: (f, 0)),
                  pl.BlockSpec((8, D_MODEL), lambda i, f: (0, 0))],
        out_specs=pl.BlockSpec((tm, D_MODEL), lambda i, f: (i, 0)),
        scratch_shapes=[pltpu.VMEM((tm, D_MODEL), F32)],
        compiler_params=pltpu.CompilerParams(
            dimension_semantics=("parallel", "arbitrary"), vmem_limit_bytes=VMEM_LIMIT),
        name="moe_norm",
    )(x, ghi, glo, expander, wg, wu, wd, ln)


TOK_PAD = 8
NEW_PAD = 16


def _softmax_two(s, sn):
    m = jnp.maximum(jnp.max(s, axis=-1, keepdims=True), jnp.max(sn, axis=-1, keepdims=True))
    p = jnp.exp(s - m)
    pn = jnp.exp(sn - m)
    inv = 1.0 / (jnp.sum(p, axis=-1, keepdims=True) + jnp.sum(pn, axis=-1, keepdims=True))
    return p * inv, pn * inv


def _new_row_mask(rows, n_new):
    t = lax.broadcasted_iota(jnp.int32, (rows, NEW_PAD), 0) & (TOK_PAD - 1)
    i = lax.broadcasted_iota(jnp.int32, (rows, NEW_PAD), 1)
    return (i <= t) & (i < n_new)


def _dec_diff_kernel(pt_ref, q_ref, new_ref, par_ref, *rest, n_pages, n_new, out_scale):
    pages = rest[:n_pages]
    o_ref, s_ref = rest[n_pages:]
    nqk = 2 * DIFF_HEADS * HEAD_DIM
    half = DIFF_HEADS * TOK_PAD
    q = q_ref[...]
    for j in range(n_pages):
        s_ref[:, j * PAGE_SIZE:(j + 1) * PAGE_SIZE] = _nt_dot(q, pages[j][:, 0:nqk].astype(BF16))
    sn = _nt_dot(q, new_ref[:, 0:nqk].astype(BF16))
    sn = jnp.where(_new_row_mask(2 * half, n_new), sn, NEG)
    p, pn = _softmax_two(s_ref[...], sn)
    lam = par_ref[0:1, 0:1]
    a = (p[0:half] - lam * p[half:2 * half]).astype(BF16)
    an = (pn[0:half] - lam * pn[half:2 * half]).astype(BF16)
    o = jnp.dot(an, new_ref[:, nqk:].astype(BF16), preferred_element_type=F32)
    for j in range(n_pages):
        o = o + jnp.dot(a[:, j * PAGE_SIZE:(j + 1) * PAGE_SIZE], pages[j][:, nqk:].astype(BF16),
                        preferred_element_type=F32)
    g = par_ref[1:2, :]
    for h in range(DIFF_HEADS):
        oh = o[h * TOK_PAD:(h + 1) * TOK_PAD, h * LANES:(h + 1) * LANES]
        oh = oh * lax.rsqrt(jnp.mean(oh * oh, axis=-1, keepdims=True) + RMS_EPS) * g
        o_ref[:, h * LANES:(h + 1) * LANES] = oh * out_scale


def _page_specs(layer, n_pages, width):
    return [pl.BlockSpec((None, None, PAGE_SIZE, width),
                         lambda b, pt, j=j: (layer, pt[b * n_pages + j], 0, 0)) for j in range(n_pages)]


def _per_seq(shape):
    return pl.BlockSpec((None,) + shape, lambda b, pt: (b,) + (0,) * len(shape))


def _shared(shape):
    return pl.BlockSpec(shape, lambda b, pt: (0,) * len(shape))


def _pad_rows(a, n):
    return jnp.pad(a, ((0, 0), (0, n - a.shape[1])) + ((0, 0),) * (a.ndim - 2))


def decode_diff_attention(layer, page_table, cache, da_q, rows_new, lam, subln_g, lambda_init):
    b, n_new, _ = da_q.shape
    n_pages = page_table.shape[1]
    q = da_q.astype(F32).reshape(b, n_new, DIFF_HEADS, 2, HEAD_DIM)
    q = _pad_rows(jnp.transpose(q, (0, 3, 2, 1, 4)).reshape(b * 2 * DIFF_HEADS, n_new, HEAD_DIM), TOK_PAD)
    q = q.reshape(b, 2, DIFF_HEADS, TOK_PAD, 1, HEAD_DIM)
    head = 2 * jnp.arange(DIFF_HEADS)[None, :] + jnp.arange(2)[:, None]
    place = (head[:, :, None] == jnp.arange(2 * DIFF_HEADS)).astype(F32)
    qbd = (q * place[None, :, :, None, :, None]).reshape(b, 2 * DIFF_HEADS * TOK_PAD, 2 * DIFF_HEADS * HEAD_DIM)
    par = jnp.concatenate([jnp.broadcast_to(lam.astype(F32), (1, LANES)), subln_g.astype(F32).reshape(1, LANES),
                           jnp.zeros((6, LANES), F32)], 0)
    rows = 2 * DIFF_HEADS * TOK_PAD
    width = cache.shape[-1]
    out = pl.pallas_call(
        functools.partial(_dec_diff_kernel, n_pages=n_pages, n_new=n_new, out_scale=1.0 - lambda_init),
        out_shape=jax.ShapeDtypeStruct((b, TOK_PAD, DIFF_HEADS * LANES), F32),
        grid_spec=pltpu.PrefetchScalarGridSpec(
            num_scalar_prefetch=1, grid=(b,),
            in_specs=[_per_seq((rows, 2 * DIFF_HEADS * HEAD_DIM)), _per_seq((NEW_PAD, width)), _shared((8, LANES))]
            + _page_specs(layer, n_pages, width),
            out_specs=_per_seq((TOK_PAD, DIFF_HEADS * LANES)),
            scratch_shapes=[pltpu.VMEM((rows, n_pages * PAGE_SIZE), F32)]),
        compiler_params=pltpu.CompilerParams(dimension_semantics=("arbitrary",), vmem_limit_bytes=VMEM_LIMIT),
        name="decode_diff_attention",
    )(page_table.reshape(-1), qbd.astype(BF16), _pad_rows(rows_new, NEW_PAD), par, *([cache] * n_pages))
    return out[:, :n_new]


def _dec_moba_kernel(pt_ref, q_ref, new_ref, *rest, n_pages, n_new, past_len):
    pages = rest[:n_pages]
    o_ref, s_ref, km_ref = rest[n_pages:]
    w = MOBA_HEADS * HEAD_DIM
    rows = MOBA_HEADS * TOK_PAD
    ppb = MOBA_BLOCK // PAGE_SIZE
    n_blk = n_pages // ppb
    q = q_ref[...]
    km_ref[...] = jnp.zeros(km_ref.shape, F32)
    for n in range(n_blk):
        tot = jnp.sum(pages[ppb * n][:, 0:w], axis=0, keepdims=True)
        for j in range(ppb * n + 1, ppb * (n + 1)):
            tot = tot + jnp.sum(pages[j][:, 0:w], axis=0, keepdims=True)
        km_ref[n:n + 1, :] = tot / MOBA_BLOCK
    gate = _nt_dot(q, km_ref[...].astype(BF16))
    lane = lax.broadcasted_iota(jnp.int32, gate.shape, 1)
    t = lax.broadcasted_iota(jnp.int32, gate.shape, 0) & (TOK_PAD - 1)
    earlier = (lane < (past_len + t) // MOBA_BLOCK) & (lane < n_blk)
    gate = jnp.where(earlier, gate, -jnp.inf)
    chosen = jnp.where(earlier & _rank_select(gate, n_blk, MOBA_TOPK), 1.0, 0.0)
    for j in range(n_pages):
        sj = _nt_dot(q, pages[j][:, 0:w].astype(BF16))
        n = j // ppb
        s_ref[:, j * PAGE_SIZE:(j + 1) * PAGE_SIZE] = jnp.where(chosen[:, n:n + 1] > 0.5, sj, NEG)
    sn = jnp.where(_new_row_mask(rows, n_new), _nt_dot(q, new_ref[:, 0:w].astype(BF16)), NEG)
    p, pn = _softmax_two(s_ref[...], sn)
    p = p.astype(BF16)
    o = jnp.dot(pn.astype(BF16), new_ref[:, w:].astype(BF16), preferred_element_type=F32)
    for j in range(n_pages):
        o = o + jnp.dot(p[:, j * PAGE_SIZE:(j + 1) * PAGE_SIZE], pages[j][:, w:].astype(BF16),
                        preferred_element_type=F32)
    for h in range(MOBA_HEADS):
        o_ref[:, h * HEAD_DIM:(h + 1) * HEAD_DIM] = o[h * TOK_PAD:(h + 1) * TOK_PAD, h * HEAD_DIM:(h + 1) * HEAD_DIM]


def _head_tiles(a, n_heads):
    b, n_new, _ = a.shape
    a = a.astype(F32).reshape(b, n_new, n_heads, HEAD_DIM)
    return _pad_rows(jnp.transpose(a, (0, 2, 1, 3)).reshape(b * n_heads, n_new, HEAD_DIM), TOK_PAD).reshape(
        b, n_heads, TOK_PAD, HEAD_DIM)


def decode_moba_attention(layer, page_table, cache, m_q, rows_new):
    b, n_new, _ = m_q.shape
    n_pages = page_table.shape[1]
    past_len = n_pages * PAGE_SIZE
    assert past_len % MOBA_BLOCK == 0 and n_new <= TOK_PAD and n_pages * PAGE_SIZE // MOBA_BLOCK <= NEW_PAD
    q = _head_tiles(m_q, MOBA_HEADS)
    place = jnp.eye(MOBA_HEADS, dtype=F32)
    q = (q[:, :, :, None, :] * place[None, :, None, :, None]).reshape(b, MOBA_HEADS * TOK_PAD, MOBA_HEADS * HEAD_DIM)
    rows = MOBA_HEADS * TOK_PAD
    width = cache.shape[-1]
    out = pl.pallas_call(
        functools.partial(_dec_moba_kernel, n_pages=n_pages, n_new=n_new, past_len=past_len),
        out_shape=jax.ShapeDtypeStruct((b, TOK_PAD, MOBA_HEADS * HEAD_DIM), F32),
        grid_spec=pltpu.PrefetchScalarGridSpec(
            num_scalar_prefetch=1, grid=(b,),
            in_specs=[_per_seq((rows, MOBA_HEADS * HEAD_DIM)), _per_seq((NEW_PAD, width))]
            + _page_specs(layer, n_pages, width),
            out_specs=_per_seq((TOK_PAD, MOBA_HEADS * HEAD_DIM)),
            scratch_shapes=[pltpu.VMEM((rows, past_len), F32), pltpu.VMEM((NEW_PAD, MOBA_HEADS * HEAD_DIM), F32)]),
        compiler_params=pltpu.CompilerParams(dimension_semantics=("arbitrary",), vmem_limit_bytes=VMEM_LIMIT),
        name="decode_moba_attention",
    )(page_table.reshape(-1), q.astype(BF16), _pad_rows(rows_new, NEW_PAD), *([cache] * n_pages))
    return out[:, :n_new]


def _dec_nsa_kernel(pt_ref, qn_ref, qr_ref, gate_ref, new_ref, wnew_ref, win_ref,
                    wc_ref, cb_ref, w2_ref, ov_ref, ex_ref, *rest, n_pages, n_new, past_len, win_pos0):
    pages = rest[:n_pages]
    o_ref, cmp_ref, slc_ref = rest[n_pages:]
    rows = NSA_HEADS * TOK_PAD
    n_chunk = past_len // CMP_STRIDE
    n_cmp = (past_len + n_new - CMP_LEN) // CMP_STRIDE + 1
    n_slc = -(-(past_len + n_new) // SLC_LEN)
    hid = cb_ref.shape[1] // 2
    for j in range(n_pages):
        cmp_ref[j * PAGE_SIZE:(j + 1) * PAGE_SIZE, :] = pages[j][:, 0:2 * HEAD_DIM]
        slc_ref[j * PAGE_SIZE:(j + 1) * PAGE_SIZE, :] = pages[j][:, 2 * HEAD_DIM:4 * HEAD_DIM]

    y = jnp.zeros((n_chunk, 4 * hid), F32)
    for r in range(CMP_STRIDE):
        xr = cmp_ref[pl.ds(r, n_chunk, stride=CMP_STRIDE), :].astype(BF16)
        y = y + jnp.dot(xr, wc_ref[r], preferred_element_type=F32)
    hk = y[:, 0:hid] + pltpu.roll(y[:, hid:2 * hid], n_chunk - 1, 0) + cb_ref[0:1, 0:hid]
    hv = y[:, 2 * hid:3 * hid] + pltpu.roll(y[:, 3 * hid:4 * hid], n_chunk - 1, 0) + cb_ref[0:1, hid:2 * hid]
    cc = jnp.dot(jax.nn.gelu(jnp.concatenate([hk, hv], axis=1)).astype(BF16), w2_ref[...],
                 preferred_element_type=F32).astype(BF16)

    t = lax.broadcasted_iota(jnp.int32, (rows, n_chunk), 0) & (TOK_PAD - 1)
    n_idx = lax.broadcasted_iota(jnp.int32, (rows, n_chunk), 1)
    ok = (n_idx * CMP_STRIDE + CMP_LEN - 1 <= past_len + t) & (n_idx < n_cmp)
    sc = jnp.where(ok, _nt_dot(qn_ref[...], cc), NEG)
    pc = jnp.exp(sc - jnp.max(sc, axis=-1, keepdims=True))
    pc = pc / jnp.sum(pc, axis=-1, keepdims=True)
    p_hi = pc.astype(BF16)
    p_lo = (pc - p_hi.astype(F32)).astype(BF16)
    o_cmp = jnp.dot(p_hi, cc, preferred_element_type=F32)[:, HEAD_DIM:]
    ov = ov_ref[...]
    imp = jnp.dot(p_hi, ov, preferred_element_type=F32) + jnp.dot(p_lo, ov, preferred_element_type=F32)
    imp = imp[0:TOK_PAD] + imp[TOK_PAD:2 * TOK_PAD] + imp[2 * TOK_PAD:3 * TOK_PAD] + imp[3 * TOK_PAD:4 * TOK_PAD]
    blk = lax.broadcasted_iota(jnp.int32, imp.shape, 1)
    cur = (past_len + lax.broadcasted_iota(jnp.int32, imp.shape, 0)) // SLC_LEN
    valid = blk <= cur
    forced = (blk == 0) | (blk == cur) | (blk == cur - 1)
    score = jnp.where(forced, jnp.inf, jnp.where(valid, imp, -jnp.inf))
    chosen = jnp.where(valid & _rank_select(score, n_slc, min(SLC_TOPN, n_slc)), 1.0, 0.0).astype(BF16)
    key_ok = jnp.dot(chosen, ex_ref[...], preferred_element_type=F32)
    key_ok = jnp.concatenate([key_ok] * NSA_HEADS, axis=0)
    new_mask = _new_row_mask(rows, n_new)

    qr = qr_ref[...]
    kv = slc_ref[...].astype(BF16)
    kv_new = new_ref[:, 2 * HEAD_DIM:4 * HEAD_DIM].astype(BF16)
    p, pn = _softmax_two(jnp.where(key_ok > 0.5, _nt_dot(qr, kv), NEG), jnp.where(new_mask, _nt_dot(qr, kv_new), NEG))
    o_slc = (jnp.dot(p.astype(BF16), kv, preferred_element_type=F32)
             + jnp.dot(pn.astype(BF16), kv_new, preferred_element_type=F32))[:, HEAD_DIM:]

    kv = win_ref[...].astype(BF16)
    kv_new = wnew_ref[...].astype(BF16)
    n_win = kv.shape[0]
    wpos = win_pos0 + lax.broadcasted_iota(jnp.int32, (rows, n_win), 1)
    qpos = past_len + (lax.broadcasted_iota(jnp.int32, (rows, n_win), 0) & (TOK_PAD - 1))
    ok = (wpos <= qpos) & (wpos > qpos - WINDOW)
    p, pn = _softmax_two(jnp.where(ok, _nt_dot(qr, kv), NEG), jnp.where(new_mask, _nt_dot(qr, kv_new), NEG))
    o_win = (jnp.dot(p.astype(BF16), kv, preferred_element_type=F32)
             + jnp.dot(pn.astype(BF16), kv_new, preferred_element_type=F32))[:, HEAD_DIM:]

    g = jax.nn.sigmoid(gate_ref[...])
    o = g[:, 0:1] * o_cmp + g[:, 1:2] * o_slc + g[:, 2:3] * o_win
    for h in range(NSA_HEADS):
        o_ref[:, h * HEAD_DIM:(h + 1) * HEAD_DIM] = o[h * TOK_PAD:(h + 1) * TOK_PAD]


def decode_nsa_attention(layer, page_table, cache, win_state, n_q, nq_r, n_gate, rows_new, rows_win,
                         cmp_pos, cmp_w1, cmp_w2):
    b, n_new, _ = n_q.shape
    n_pages = page_table.shape[1]
    past_len = n_pages * PAGE_SIZE
    n_win = win_state.shape[2]
    assert past_len % SLC_LEN == 0 and n_new < CMP_STRIDE and past_len >= CMP_LEN and n_new <= TOK_PAD
    n_chunk = past_len // CMP_STRIDE
    assert n_chunk == LANES, "compressed tokens are laid out on one lane tile"
    hid = cmp_w1.shape[-1]
    lane_pad = lambda a: jnp.pad(a, ((0, 0),) * (a.ndim - 1) + ((0, LANES - a.shape[-1]),))
    qn = lane_pad(_head_tiles(n_q, NSA_HEADS)).reshape(b, -1, LANES).astype(BF16)
    qr = lane_pad(_head_tiles(nq_r, NSA_HEADS)).reshape(b, -1, LANES).astype(BF16)
    g = jnp.transpose(n_gate[..., :N_GATE].reshape(b, n_new, NSA_HEADS, 3), (0, 2, 1, 3))
    g = lane_pad(_pad_rows(g.reshape(b * NSA_HEADS, n_new, 3), TOK_PAD)).reshape(b, -1, LANES)
    w1 = cmp_w1.reshape(2, 2, CMP_STRIDE, HEAD_DIM, hid)
    zero = jnp.zeros((CMP_STRIDE, HEAD_DIM, 2 * hid), F32)
    top = jnp.concatenate([w1[0, 0], w1[0, 1], zero], axis=-1)
    bot = jnp.concatenate([zero, w1[1, 0], w1[1, 1]], axis=-1)
    wc = jnp.concatenate([top, bot], axis=1).astype(BF16)
    bias = jnp.concatenate([cmp_pos[0].reshape(1, -1) @ cmp_w1[0], cmp_pos[1].reshape(1, -1) @ cmp_w1[1]], -1)
    cb = jnp.concatenate([bias, jnp.zeros((7, 2 * hid), F32)], 0)
    zw = jnp.zeros((hid, HEAD_DIM), F32)
    w2 = jnp.concatenate([jnp.concatenate([cmp_w2[0], zw], 1), jnp.concatenate([zw, cmp_w2[1]], 1)], 0).astype(BF16)
    cmp_start = jnp.arange(n_chunk) * CMP_STRIDE
    slc_start = jnp.arange(LANES) * SLC_LEN
    ov = ((cmp_start[:, None] <= slc_start[None, :] + SLC_LEN - 1)
          & (cmp_start[:, None] + CMP_LEN - 1 >= slc_start[None, :])).astype(BF16)
    ex = (jnp.arange(past_len)[None, :] // SLC_LEN == jnp.arange(LANES)[:, None]).astype(BF16)
    rows = NSA_HEADS * TOK_PAD
    width = cache.shape[-1]
    out = pl.pallas_call(
        functools.partial(_dec_nsa_kernel, n_pages=n_pages, n_new=n_new, past_len=past_len,
                          win_pos0=past_len - n_win),
        out_shape=jax.ShapeDtypeStruct((b, TOK_PAD, NSA_HEADS * HEAD_DIM), F32),
        grid_spec=pltpu.PrefetchScalarGridSpec(
            num_scalar_prefetch=1, grid=(b,),
            in_specs=[_per_seq((rows, LANES)), _per_seq((rows, LANES)), _per_seq((rows, LANES)),
                      _per_seq((NEW_PAD, width)), _per_seq((NEW_PAD, 2 * HEAD_DIM)),
                      pl.BlockSpec((None, None, n_win, 2 * HEAD_DIM), lambda bi, pt: (layer, bi, 0, 0)),
                      _shared(wc.shape), _shared(cb.shape), _shared(w2.shape), _shared(ov.shape), _shared(ex.shape)]
            + _page_specs(layer, n_pages, width),
            out_specs=_per_seq((TOK_PAD, NSA_HEADS * HEAD_DIM)),
            scratch_shapes=[pltpu.VMEM((past_len, 2 * HEAD_DIM), F32), pltpu.VMEM((past_len, 2 * HEAD_DIM), F32)]),
        compiler_params=pltpu.CompilerParams(dimension_semantics=("arbitrary",), vmem_limit_bytes=VMEM_LIMIT),
        name="decode_nsa_attention",
    )(page_table.reshape(-1), qn, qr, g, _pad_rows(rows_new, NEW_PAD), _pad_rows(rows_win, NEW_PAD), win_state,
      wc, cb, w2, ov, ex, *([cache] * n_pages))
    return out[:, :n_new]


def _compress(kv, pos_emb, w1, w2):
    b, l, _ = kv.shape
    n_chunk = l // CMP_STRIDE
    n_cmp = (l - CMP_LEN) // CMP_STRIDE + 1
    x = kv[:, :n_chunk * CMP_STRIDE].reshape(b * n_chunk, CMP_STRIDE * HEAD_DIM)
    half = CMP_STRIDE * HEAD_DIM
    w_cat = jnp.concatenate([w1[:half], w1[half:]], axis=1).astype(BF16)
    rows = x.shape[0]
    tm = 512 if rows % 512 == 0 else rows
    y = matmul(x, w_cat, tm, w_cat.shape[1]).reshape(b, n_chunk, 2, w1.shape[1])
    bias = pos_emb.reshape(1, -1) @ w1
    hid = y[:, :n_cmp, 0] + y[:, 1:n_cmp + 1, 1] + bias
    return jax.nn.gelu(hid) @ w2


def _route(t, router_w, router_b):
    s = jax.nn.sigmoid((t @ router_w).astype(F32))
    sb = s + router_b.astype(F32)
    grp = sb.reshape(-1, N_EXPERT_GROUPS, N_EXPERTS // N_EXPERT_GROUPS)
    grp_score = jnp.sum(lax.top_k(grp, 2)[0], -1)
    _, top_g = lax.top_k(grp_score, TOPK_GROUPS)
    gmask = jnp.sum(jax.nn.one_hot(top_g, N_EXPERT_GROUPS, dtype=F32), -2) > 0
    emask = jnp.repeat(gmask, N_EXPERTS // N_EXPERT_GROUPS, axis=-1)
    _, top_e = lax.top_k(jnp.where(emask, sb, -jnp.inf), TOP_K)
    w = jnp.take_along_axis(s, top_e, -1)
    w = w / jnp.sum(w, -1, keepdims=True) * ROUTED_SCALE
    return jnp.einsum('tk,tke->te', w, jax.nn.one_hot(top_e, N_EXPERTS, dtype=F32))


def kernel(x_prompt, x_sample, cache_diff, cache_nsa, cache_moba, state_nsa_win, page_table, w_in, diff_lambda, diff_subln, nsa_cmp_pos, nsa_cmp_w1, nsa_cmp_w2, w_br_diff, w_br_nsa, w_br_moba, w_out, ln1_g, ln1_b, ln2_g, ln2_b, router_w, router_b, exp_w_gate, exp_w_up, exp_w_down, sh_w_gate, sh_w_up, sh_w_down):
    bp, sp, _ = x_prompt.shape
    bs, ss, _ = x_sample.shape
    tp, ts = bp * sp, bs * ss
    n_pages = page_table.shape[1]
    past_len = n_pages * PAGE_SIZE
    pos_p = jnp.arange(sp, dtype=jnp.int32)
    pos_s = past_len + jnp.arange(ss, dtype=jnp.int32)
    win_pos0 = past_len - state_nsa_win.shape[2]
    pos = jnp.concatenate([jnp.tile(pos_p, bp), jnp.tile(pos_s, bs)])
    inv = ROPE_THETA ** (-jnp.arange(HALF, dtype=F32) / HALF)
    ang = pos.astype(F32)[:, None] * inv[None, :]
    cos, sin = jnp.cos(ang), jnp.sin(ang)

    x = jnp.concatenate([x_prompt.reshape(tp, D_MODEL), x_sample.reshape(ts, D_MODEL)], 0)
    st_p = [[], [], [], []]
    st_s = [[], [], [], []]
    for l in range(DEPTH):
        lambda_init = 0.8 - 0.6 * math.exp(-0.3 * l)
        w_pad = jnp.concatenate([w_in[l][:, :C_MQ_SRC],
                                 jnp.zeros((D_MODEL, C_MQ - C_MQ_SRC), F32),
                                 w_in[l][:, C_MQ_SRC:]], axis=1).astype(BF16)
        pj = project_inputs(x, w_pad, cos, sin)
        rows_diff, rows_nsa, rows_moba, rows_win = pj["rows_diff"], pj["rows_nsa"], pj["rows_moba"], pj["rows_win"]

        lp_ = diff_lambda[l].astype(F32)
        lam = jnp.exp(jnp.sum(lp_[0] * lp_[1])) - jnp.exp(jnp.sum(lp_[2] * lp_[3])) + lambda_init

        def pr(a):
            return a[:tp].reshape(bp, sp, a.shape[-1])

        o_diff_p = diff_attention(pj["q_diff"], pj["k_diff"], pj["v_diff"], bp, sp, lam, diff_subln[l], lambda_init)
        ck_c = _compress(pr(rows_nsa[:, 0:HEAD_DIM]), nsa_cmp_pos[l, 0], nsa_cmp_w1[l, 0], nsa_cmp_w2[l, 0])
        cv_c = _compress(pr(rows_nsa[:, HEAD_DIM:2 * HEAD_DIM]), nsa_cmp_pos[l, 1], nsa_cmp_w1[l, 1],
                         nsa_cmp_w2[l, 1])
        o_nsa_p = nsa_attention(pj["q_nsa"], pj["q_nsa_rot"], jnp.concatenate([ck_c, cv_c], -1).astype(BF16),
                                pj["kv_slc"], pj["kv_win"], pj["nsa_gate"], bp, sp)
        k_mean = jnp.mean(pr(rows_moba[:, 0:MOBA_HEADS * HEAD_DIM]).reshape(bp, sp // MOBA_BLOCK, MOBA_BLOCK, -1),
                          axis=2)
        o_moba_p = moba_attention(pj["q_moba"], pj["k_moba"], pj["v_moba"], k_mean, bp, sp)

        def sm(a):
            return a[tp:].reshape(bs, ss, a.shape[-1])

        o_diff_s = decode_diff_attention(l, page_table, cache_diff, sm(pj["q_diff"]), sm(rows_diff),
                                         lam, diff_subln[l], lambda_init)
        o_nsa_s = decode_nsa_attention(l, page_table, cache_nsa, state_nsa_win, sm(pj["q_nsa"]),
                                       sm(pj["q_nsa_rot"]), sm(pj["nsa_gate"]), sm(rows_nsa), sm(rows_win),
                                       nsa_cmp_pos[l], nsa_cmp_w1[l], nsa_cmp_w2[l])
        o_moba_s = decode_moba_attention(l, page_table, cache_moba, sm(pj["q_moba"]), sm(rows_moba))

        def both(a_p, a_s):
            return jnp.concatenate([a_p.reshape(tp, -1), a_s.reshape(ts, -1)], 0)

        x = merge_project_norm(x, both(o_diff_p, o_diff_s), both(o_nsa_p, o_nsa_s), both(o_moba_p, o_moba_s),
                               pj["merge_gate"], w_br_diff[l].astype(BF16), w_br_nsa[l].astype(BF16),
                               w_br_moba[l].astype(BF16), w_out[l].astype(BF16), ln1_g[l], ln1_b[l])

        gate = _route(x, router_w[l], router_b[l])
        gate = jnp.concatenate([gate, jnp.ones((tp + ts, D_SHARED // D_EXPERT), F32),
                                jnp.zeros((tp + ts, LANES - N_EXPERTS - D_SHARED // D_EXPERT), F32)], -1)
        wg = jnp.concatenate([jnp.transpose(exp_w_gate[l], (1, 0, 2)).reshape(D_MODEL, -1), sh_w_gate[l]], 1)
        wu = jnp.concatenate([jnp.transpose(exp_w_up[l], (1, 0, 2)).reshape(D_MODEL, -1), sh_w_up[l]], 1)
        wd = jnp.concatenate([exp_w_down[l].reshape(-1, D_MODEL), sh_w_down[l]], 0)
        x = moe_norm(x, gate, wg.astype(BF16), wu.astype(BF16), wd.astype(BF16), ln2_g[l], ln2_b[l])

        for lst, val in zip(st_p, (rows_diff, rows_nsa, rows_moba)):
            lst.append(pr(val))
        n_keep = min(WINDOW, sp)
        st_p[3].append(pr(rows_win)[:, sp - n_keep:])
        for lst, val in zip(st_s, (rows_diff, rows_nsa, rows_moba)):
            lst.append(sm(val))
        full_win = jnp.concatenate([state_nsa_win[l], sm(rows_win)], 1)
        n_keep = min(WINDOW, past_len + ss)
        st_s[3].append(full_win[:, full_win.shape[1] - n_keep:])

    outs_p = [jnp.stack(a, 0) for a in st_p]
    outs_s = [jnp.stack(a, 0) for a in st_s]
    return (x[:tp].reshape(bp, sp, D_MODEL), x[tp:].reshape(bs, ss, D_MODEL),
            outs_p[0], outs_s[0], outs_p[1], outs_s[1], outs_p[2], outs_s[2], outs_p[3], outs_s[3])
```

```python
import functools
import math

import jax
import jax.numpy as jnp
from jax import lax
from jax.experimental import pallas as pl
from jax.experimental.pallas import tpu as pltpu

F32 = jnp.float32
BF16 = jnp.bfloat16

D_MODEL = 1024
DEPTH = 2
PAGE_SIZE = 128
HEAD_DIM = 64
HALF = HEAD_DIM // 2
ATTN_SCALE = HEAD_DIM ** -0.5
ROPE_THETA = 10000.0
DIFF_HEADS = 4
NSA_HEADS = 4
CMP_LEN = 32
CMP_STRIDE = 16
SLC_LEN = 64
SLC_TOPN = 16
WINDOW = 512
MOBA_HEADS = 4
MOBA_BLOCK = 256
MOBA_TOPK = 3
N_EXPERTS = 64
N_EXPERT_GROUPS = 8
TOPK_GROUPS = 4
TOP_K = 6
D_EXPERT = 128
D_SHARED = 256
ROUTED_SCALE = 2.5
LN_EPS = 1e-5
RMS_EPS = 1e-5
DEEPNORM_ALPHA = (2 * DEPTH) ** 0.25

C_DAQ, C_DAK, C_DAV, C_NQ = 0, 512, 1024, 1536
C_CK, C_CV, C_SK, C_SV, C_WK, C_WV, C_NG = 1792, 1856, 1920, 1984, 2048, 2112, 2176
N_GATE = 3 * NSA_HEADS
C_MQ_SRC = C_NG + N_GATE
C_MQ, C_MK, C_MV, C_MG = 2304, 2560, 2816, 3072
N_IN_PAD = C_MG + 3 * D_MODEL

LANES = 128
NEG = -1e30
VMEM_LIMIT = 56 * 1024 * 1024


def _nt_dot(a, b):
    return lax.dot_general(a, b, (((1,), (1,)), ((), ())), preferred_element_type=F32)


def _mm_kernel(x_ref, w_ref, o_ref):
    o_ref[...] = jnp.dot(x_ref[...].astype(BF16), w_ref[...],
                         preferred_element_type=F32).astype(o_ref.dtype)


def matmul(x, w, tm, tn, out_dtype=F32):
    m, k = x.shape
    n = w.shape[1]
    assert m % tm == 0 and n % tn == 0, (x.shape, w.shape, tm, tn)
    return pl.pallas_call(
        _mm_kernel,
        out_shape=jax.ShapeDtypeStruct((m, n), out_dtype),
        grid=(m // tm, n // tn),
        in_specs=[pl.BlockSpec((tm, k), lambda i, j: (i, 0)),
                  pl.BlockSpec((k, tn), lambda i, j: (0, j))],
        out_specs=pl.BlockSpec((tm, tn), lambda i, j: (i, j)),
        compiler_params=pltpu.CompilerParams(
            dimension_semantics=("parallel", "arbitrary"), vmem_limit_bytes=VMEM_LIMIT),
        name="matmul",
    )(x, w)


def _project_kernel(x_ref, w_ref, cos_ref, sa_ref, sb_ref,
                    rd_ref, rn_ref, rm_ref, rw_ref, mg_ref, ng_ref,
                    qd_ref, kd_ref, vd_ref, nq_ref, nqr_ref, ss_ref, ww_ref, mq_ref, mk_ref, mv_ref):
    xb = x_ref[...].astype(BF16)
    cos, sa, sb = cos_ref[...], sa_ref[...], sb_ref[...]

    def seg(a, b):
        return jnp.dot(xb, w_ref[:, a:b], preferred_element_type=F32)

    def rope(y):
        tiles = []
        for t in range(y.shape[1] // LANES):
            yt = y[:, t * LANES:(t + 1) * LANES]
            tiles.append(yt * cos + pltpu.roll(yt, LANES - HALF, 1) * sa + pltpu.roll(yt, HALF, 1) * sb)
        return tiles[0] if len(tiles) == 1 else jnp.concatenate(tiles, axis=1)

    first = lax.broadcasted_iota(jnp.int32, (x_ref.shape[0], LANES), 1) < HEAD_DIM
    qd_ref[...] = (rope(seg(C_DAQ, C_DAK)) * ATTN_SCALE).astype(BF16)
    k = rope(seg(C_DAK, C_DAV))
    v = seg(C_DAV, C_NQ)
    rd_ref[:, 0:C_DAV - C_DAK] = k
    rd_ref[:, C_DAV - C_DAK:] = v
    kd_ref[...] = k.astype(BF16)
    vd_ref[...] = v.astype(BF16)
    nq = seg(C_NQ, C_CK)
    nq_ref[...] = (nq * ATTN_SCALE).astype(BF16)
    nqr_ref[...] = (rope(nq) * ATTN_SCALE).astype(BF16)
    y = seg(C_CK, C_NG)
    ss = jnp.where(first, rope(y[:, LANES:2 * LANES]), y[:, LANES:2 * LANES])
    ww = jnp.where(first, rope(y[:, 2 * LANES:3 * LANES]), y[:, 2 * LANES:3 * LANES])
    rn_ref[:, 0:LANES] = y[:, 0:LANES]
    rn_ref[:, LANES:] = ss
    rw_ref[...] = ww
    ss_ref[...] = ss.astype(BF16)
    ww_ref[...] = ww.astype(BF16)
    ng_ref[...] = seg(C_NG, C_MQ)
    mq_ref[...] = (rope(seg(C_MQ, C_MK)) * ATTN_SCALE).astype(BF16)
    k = rope(seg(C_MK, C_MV))
    v = seg(C_MV, C_MG)
    rm_ref[:, 0:C_MV - C_MK] = k
    rm_ref[:, C_MV - C_MK:] = v
    mk_ref[...] = k.astype(BF16)
    mv_ref[...] = v.astype(BF16)
    for t in range(3):
        mg_ref[:, t * D_MODEL:(t + 1) * D_MODEL] = seg(C_MG + t * D_MODEL, C_MG + (t + 1) * D_MODEL)


def project_inputs(x, w_pad, cos, sin, tm=256):
    t = x.shape[0]
    zero = jnp.zeros_like(sin)
    cos_t = jnp.tile(cos, (1, LANES // HALF))
    sa = jnp.concatenate([-sin, zero, -sin, zero], axis=1)
    sb = jnp.concatenate([zero, sin, zero, sin], axis=1)
    names_f32 = (("rows_diff", 1024), ("rows_nsa", 256), ("rows_moba", 512), ("rows_win", 128),
                 ("merge_gate", 3 * D_MODEL), ("nsa_gate", LANES))
    names_bf16 = (("q_diff", 512), ("k_diff", 512), ("v_diff", 512), ("q_nsa", 256), ("q_nsa_rot", 256),
                  ("kv_slc", 128), ("kv_win", 128), ("q_moba", 256), ("k_moba", 256), ("v_moba", 256))
    row = lambda i: (i, 0)
    outs = pl.pallas_call(
        _project_kernel,
        out_shape=[jax.ShapeDtypeStruct((t, n), F32) for _, n in names_f32]
        + [jax.ShapeDtypeStruct((t, n), BF16) for _, n in names_bf16],
        grid=(t // tm,),
        in_specs=[pl.BlockSpec((tm, D_MODEL), row),
                  pl.BlockSpec(w_pad.shape, lambda i: (0, 0), pipeline_mode=pl.Buffered(1)),
                  pl.BlockSpec((tm, LANES), row), pl.BlockSpec((tm, LANES), row), pl.BlockSpec((tm, LANES), row)],
        out_specs=[pl.BlockSpec((tm, n), row) for _, n in names_f32 + names_bf16],
        compiler_params=pltpu.CompilerParams(dimension_semantics=("parallel",), vmem_limit_bytes=VMEM_LIMIT),
        name="project_inputs",
    )(x, w_pad, cos_t, sa, sb)
    return dict(zip([n for n, _ in names_f32 + names_bf16], outs))


def _online_update(s, v, m_ref, l_ref, acc_ref):
    tk = s.shape[1]
    dv = acc_ref.shape[-1]
    m_prev = m_ref[...]
    m_next = jnp.maximum(m_prev, jnp.max(s, axis=-1, keepdims=True))
    alpha = jnp.exp(m_prev - m_next)
    p = jnp.exp(s - jnp.concatenate([m_next] * (tk // LANES), axis=1))
    l_ref[...] = alpha * l_ref[...] + jnp.sum(p, axis=-1, keepdims=True)
    m_ref[...] = m_next
    acc_ref[...] = acc_ref[...] * alpha[:, :dv] + jnp.dot(p.astype(BF16), v, preferred_element_type=F32)


def _init_state(*refs):
    for m_ref, l_ref, acc_ref in zip(refs[0::3], refs[1::3], refs[2::3]):
        m_ref[...] = jnp.full(m_ref.shape, NEG, F32)
        l_ref[...] = jnp.zeros(l_ref.shape, F32)
        acc_ref[...] = jnp.zeros(acc_ref.shape, F32)


def _normalised(l_ref, acc_ref):
    dv = acc_ref.shape[-1]
    return acc_ref[...] / jnp.maximum(l_ref[...], 1e-30)[:, :dv]


def _diff_attn_kernel(q_ref, k_ref, v_ref, par_ref, o_ref,
                      m0, l0, a0, m1, l1, a1, *, tq, tk, out_scale):
    qi = pl.program_id(2)
    q = q_ref[...]
    lane = lax.broadcasted_iota(jnp.int32, q.shape, 1)
    zero = jnp.zeros_like(q)
    qa = jnp.where(lane < HEAD_DIM, q, zero)
    qb = jnp.where(lane >= HEAD_DIM, q, zero)
    q0 = qi * tq
    row = q0 + lax.broadcasted_iota(jnp.int32, (tq, tk), 0)
    col = lax.broadcasted_iota(jnp.int32, (tq, tk), 1)
    _init_state(m0, l0, a0, m1, l1, a1)

    def step(j, causal):
        start = pl.multiple_of(j * tk, tk)
        k = k_ref[pl.ds(start, tk), :]
        v = v_ref[pl.ds(start, tk), :]
        sa, sb = _nt_dot(qa, k), _nt_dot(qb, k)
        if causal:
            mask = (col + start) <= row
            sa, sb = jnp.where(mask, sa, NEG), jnp.where(mask, sb, NEG)
        _online_update(sa, v, m0, l0, a0)
        _online_update(sb, v, m1, l1, a1)

    n_full = (q0 + 1) // tk
    lax.fori_loop(0, n_full, lambda j, c: (step(j, False), c)[1], 0)
    lax.fori_loop(n_full, (q0 + tq + tk - 1) // tk, lambda j, c: (step(j, True), c)[1], 0)
    lam = par_ref[0:1, :]
    g = par_ref[1:2, :]
    o = _normalised(l0, a0) - lam * _normalised(l1, a1)
    o = o * lax.rsqrt(jnp.mean(o * o, axis=-1, keepdims=True) + RMS_EPS) * g
    o_ref[...] = (o * out_scale).astype(o_ref.dtype)


def diff_attention(q, k, v, n_batch, s, lam, subln_g, lambda_init, tq=256, tk=512):
    nq = s // tq
    par = jnp.concatenate([jnp.broadcast_to(lam.astype(F32), (1, LANES)),
                           subln_g.astype(F32).reshape(1, LANES),
                           jnp.zeros((6, LANES), F32)], 0)
    kern = functools.partial(_diff_attn_kernel, tq=tq, tk=tk, out_scale=1.0 - lambda_init)
    st = [pltpu.VMEM((tq, LANES), F32)] * 6
    return pl.pallas_call(
        kern,
        out_shape=jax.ShapeDtypeStruct((n_batch * s, DIFF_HEADS * LANES), F32),
        grid=(n_batch, DIFF_HEADS, nq),
        in_specs=[pl.BlockSpec((tq, LANES), lambda bi, h, i: (bi * nq + i, h)),
                  pl.BlockSpec((s, LANES), lambda bi, h, i: (bi, h)),
                  pl.BlockSpec((s, LANES), lambda bi, h, i: (bi, h)),
                  pl.BlockSpec((8, LANES), lambda bi, h, i: (0, 0))],
        out_specs=pl.BlockSpec((tq, LANES), lambda bi, h, i: (bi * nq + i, h)),
        scratch_shapes=st,
        compiler_params=pltpu.CompilerParams(
            dimension_semantics=("parallel", "parallel", "arbitrary"), vmem_limit_bytes=VMEM_LIMIT),
        name="diff_attention",
    )(q, k, v, par)


def _rank_select(score, n_candidates, top_n):
    lane = lax.broadcasted_iota(jnp.int32, score.shape, 1)
    rank = jnp.zeros(score.shape, F32)
    for c in range(n_candidates):
        col = score[:, c:c + 1]
        ahead = (col > score) | ((col == score) & (c < lane))
        rank = rank + jnp.where(ahead, 1.0, 0.0)
    return rank < top_n


def _rank_select_rows(score, n_candidates, top_n):
    idx = lax.broadcasted_iota(jnp.int32, score.shape, 0)
    rank = jnp.zeros(score.shape, F32)
    for c in range(n_candidates):
        cand = score[c:c + 1]
        ahead = (cand > score) | ((cand == score) & (c < idx))
        rank = rank + jnp.where(ahead, 1.0, 0.0)
    return rank < top_n


def _rows_to_lanes(x_t):
    n, q = x_t.shape
    if n < LANES:
        x_t = jnp.concatenate([x_t, jnp.zeros((LANES - n, q), F32)], axis=0)
    return jnp.concatenate([x_t[:, c * LANES:(c + 1) * LANES].T for c in range(q // LANES)], axis=0)


def _stack_heads(q_ref, tq):
    first = lax.broadcasted_iota(jnp.int32, (tq, LANES), 1) < HEAD_DIM
    tiles = []
    for t in range(NSA_HEADS // 2):
        pair = q_ref[:, t * LANES:(t + 1) * LANES].astype(F32)
        tiles.append(jnp.where(first, pair, 0.0))
        tiles.append(jnp.where(first, pltpu.roll(pair, HEAD_DIM, 1), 0.0))
    return jnp.concatenate(tiles, axis=0).astype(BF16)


def _nsa_attn_kernel(qn_ref, qr_ref, cc_ref, ss_ref, ww_ref, gate_ref, ov_ref, e_ref, o_ref, m, l, acc,
                     *, tq, tk, tkw, n_cmp, n_slc):
    qi = pl.program_id(1)
    q0 = qi * tq
    nh = NSA_HEADS
    r = nh * tq
    qn = _stack_heads(qn_ref, tq)
    qr = _stack_heads(qr_ref, tq)

    cc = cc_ref[...]
    n_pad = cc.shape[0]
    qpos = q0 + lax.broadcasted_iota(jnp.int32, (tq, n_pad), 0)
    qpos = jnp.concatenate([qpos] * nh, axis=0)
    n_idx = lax.broadcasted_iota(jnp.int32, (r, n_pad), 1)
    ok = (n_idx * CMP_STRIDE + CMP_LEN - 1 <= qpos) & (n_idx < n_cmp)
    sc = jnp.where(ok, _nt_dot(qn, cc), NEG)
    pc = jnp.where(ok, jnp.exp(sc - jnp.max(sc, axis=-1, keepdims=True)), 0.0)
    pc = pc / jnp.maximum(jnp.sum(pc, axis=-1, keepdims=True), 1e-30)
    p_hi = pc.astype(BF16)
    p_lo = (pc - p_hi.astype(F32)).astype(BF16)
    o_cmp = jnp.dot(p_hi, cc, preferred_element_type=F32)
    ovt = ov_ref[...]
    imp = _nt_dot(ovt, p_hi) + _nt_dot(ovt, p_lo)
    imp = imp[:, 0:tq] + imp[:, tq:2 * tq] + imp[:, 2 * tq:3 * tq] + imp[:, 3 * tq:4 * tq]
    imp = imp[0:-(-n_slc // 8) * 8]
    blk = lax.broadcasted_iota(jnp.int32, imp.shape, 0)
    cur = (q0 + lax.broadcasted_iota(jnp.int32, imp.shape, 1)) // SLC_LEN
    valid = blk <= cur
    forced = (blk == 0) | (blk == cur) | (blk == cur - 1)
    score = jnp.where(forced, jnp.inf, jnp.where(valid, imp, -jnp.inf))
    chosen = jnp.where(valid & _rank_select_rows(score, n_slc, min(SLC_TOPN, n_slc)), 1.0, 0.0)
    chosen = _rows_to_lanes(chosen).astype(BF16)
    chosen = jnp.concatenate([chosen] * nh, axis=0)

    def sweep(kv_ref, tile, lo, hi, mask_fn):
        _init_state(m, l, acc)

        def body(j, c):
            start = pl.multiple_of(j * tile, tile)
            kv = kv_ref[pl.ds(start, tile), :]
            _online_update(jnp.where(mask_fn(j, start), _nt_dot(qr, kv), NEG), kv, m, l, acc)
            return c

        lax.fori_loop(lo, hi, body, 0)
        return _normalised(l, acc)

    row = jnp.concatenate([q0 + lax.broadcasted_iota(jnp.int32, (tq, tk), 0)] * nh, axis=0)
    col = lax.broadcasted_iota(jnp.int32, (r, tk), 1)
    o_slc = sweep(ss_ref, tk, 0, (q0 + tq + tk - 1) // tk,
                  lambda j, start: ((col + start) <= row)
                  & (jnp.dot(chosen, e_ref[j], preferred_element_type=F32) > 0.5))
    row_w = jnp.concatenate([q0 + lax.broadcasted_iota(jnp.int32, (tq, tkw), 0)] * nh, axis=0)
    col_w = lax.broadcasted_iota(jnp.int32, (r, tkw), 1)
    o_win = sweep(ww_ref, tkw, jnp.maximum(q0 - WINDOW + 1, 0) // tkw, (q0 + tq + tkw - 1) // tkw,
                  lambda j, start: ((col_w + start) <= row_w) & ((col_w + start) > row_w - WINDOW))

    g = jax.nn.sigmoid(gate_ref[...])
    first = lax.broadcasted_iota(jnp.int32, (tq, LANES), 1) < HEAD_DIM
    mixed = []
    for h in range(nh):
        rows = slice(h * tq, (h + 1) * tq)
        mixed.append(g[:, 3 * h:3 * h + 1] * o_cmp[rows] + g[:, 3 * h + 1:3 * h + 2] * o_slc[rows]
                     + g[:, 3 * h + 2:3 * h + 3] * o_win[rows])
    for t in range(nh // 2):
        o_ref[:, t * LANES:(t + 1) * LANES] = jnp.where(first, pltpu.roll(mixed[2 * t], HEAD_DIM, 1), mixed[2 * t + 1])


def _block_expander(n_chunks, n_rows, tk, block_len):
    key = jnp.arange(n_chunks)[:, None, None] * tk + jnp.arange(tk)[None, None, :]
    return (key // block_len == jnp.arange(n_rows)[None, :, None]).astype(BF16)


def nsa_attention(qn, qr, cc, ss, ww, gate, n_batch, s, tq=128, tk=512, tkw=256):
    nq = s // tq
    n_cmp = cc.shape[1]
    n_pad = -(-n_cmp // LANES) * LANES
    cc = jnp.pad(cc, ((0, 0), (0, n_pad - n_cmp), (0, 0)))
    n_slc = s // SLC_LEN
    assert n_slc <= LANES
    cmp_start = jnp.arange(n_pad) * CMP_STRIDE
    slc_start = jnp.arange(LANES) * SLC_LEN
    ov = ((cmp_start[None, :] <= slc_start[:, None] + SLC_LEN - 1)
          & (cmp_start[None, :] + CMP_LEN - 1 >= slc_start[:, None])).astype(BF16)
    expander = _block_expander(s // tk, LANES, tk, SLC_LEN)
    r = NSA_HEADS * tq
    row = lambda bi, i: (bi * nq + i, 0)
    return pl.pallas_call(
        functools.partial(_nsa_attn_kernel, tq=tq, tk=tk, tkw=tkw, n_cmp=n_cmp, n_slc=n_slc),
        out_shape=jax.ShapeDtypeStruct((n_batch * s, NSA_HEADS * HEAD_DIM), F32),
        grid=(n_batch, nq),
        in_specs=[pl.BlockSpec((tq, NSA_HEADS * HEAD_DIM), row),
                  pl.BlockSpec((tq, NSA_HEADS * HEAD_DIM), row),
                  pl.BlockSpec((None, n_pad, LANES), lambda bi, i: (bi, 0, 0)),
                  pl.BlockSpec((s, LANES), lambda bi, i: (bi, 0)),
                  pl.BlockSpec((s, LANES), lambda bi, i: (bi, 0)),
                  pl.BlockSpec((tq, LANES), row),
                  pl.BlockSpec(ov.shape, lambda bi, i: (0, 0)),
                  pl.BlockSpec(expander.shape, lambda bi, i: (0, 0, 0))],
        out_specs=pl.BlockSpec((tq, NSA_HEADS * HEAD_DIM), row),
        scratch_shapes=[pltpu.VMEM((r, LANES), F32), pltpu.VMEM((r, LANES), F32), pltpu.VMEM((r, LANES), F32)],
        compiler_params=pltpu.CompilerParams(
            dimension_semantics=("parallel", "arbitrary"), vmem_limit_bytes=VMEM_LIMIT),
        name="nsa_attention",
    )(qn, qr, cc, ss, ww, gate, ov, expander)


def _moba_attn_kernel(q_ref, k_ref, v_ref, km_ref, e_ref, o_ref, m0, l0, a0, m1, l1, a1, *, tq, tk, n_blk):
    qi = pl.program_id(2)
    q = q_ref[...]
    lane = lax.broadcasted_iota(jnp.int32, q.shape, 1)
    zero = jnp.zeros_like(q)
    qa = jnp.where(lane < HEAD_DIM, q, zero)
    qb = jnp.where(lane >= HEAD_DIM, q, zero)
    q0 = qi * tq
    row = q0 + lax.broadcasted_iota(jnp.int32, (tq, tk), 0)
    col = lax.broadcasted_iota(jnp.int32, (tq, tk), 1)

    km = km_ref[...]
    blk = lax.broadcasted_iota(jnp.int32, (km.shape[0], tq), 0)
    cur = (q0 + lax.broadcasted_iota(jnp.int32, (km.shape[0], tq), 1)) // MOBA_BLOCK
    earlier = blk < cur

    def block_mask(qh):
        gate = jnp.where(earlier, _nt_dot(km, qh), -jnp.inf)
        chosen = (earlier & _rank_select_rows(gate, n_blk, min(MOBA_TOPK, n_blk))) | (blk == cur)
        return _rows_to_lanes(jnp.where(chosen, 1.0, 0.0)).astype(BF16)

    bm0 = block_mask(qa)
    bm1 = block_mask(qb)
    _init_state(m0, l0, a0, m1, l1, a1)

    def body(j, carry):
        start = pl.multiple_of(j * tk, tk)
        k = k_ref[pl.ds(start, tk), :]
        v = v_ref[pl.ds(start, tk), :]
        causal = (col + start) <= row
        e = e_ref[j]
        mask0 = causal & (jnp.dot(bm0, e, preferred_element_type=F32) > 0.5)
        mask1 = causal & (jnp.dot(bm1, e, preferred_element_type=F32) > 0.5)
        _online_update(jnp.where(mask0, _nt_dot(qa, k), NEG), v, m0, l0, a0)
        _online_update(jnp.where(mask1, _nt_dot(qb, k), NEG), v, m1, l1, a1)
        return carry

    lax.fori_loop(0, (q0 + tq + tk - 1) // tk, body, 0)
    o_ref[...] = jnp.where(lane < HEAD_DIM, _normalised(l0, a0), _normalised(l1, a1)).astype(o_ref.dtype)


def moba_attention(q, k, v, k_mean, n_batch, s, tq=256, tk=512):
    nq = s // tq
    n_blk = k_mean.shape[1]
    n_pad = -(-n_blk // 16) * 16
    km = jnp.pad(k_mean, ((0, 0), (0, n_pad - n_blk), (0, 0))).astype(BF16)
    expander = _block_expander(s // tk, LANES, tk, MOBA_BLOCK)
    st = [pltpu.VMEM((tq, LANES), F32)] * 6
    return pl.pallas_call(
        functools.partial(_moba_attn_kernel, tq=tq, tk=tk, n_blk=n_blk),
        out_shape=jax.ShapeDtypeStruct((n_batch * s, MOBA_HEADS * HEAD_DIM), F32),
        grid=(n_batch, MOBA_HEADS // 2, nq),
        in_specs=[pl.BlockSpec((tq, LANES), lambda bi, h, i: (bi * nq + i, h)),
                  pl.BlockSpec((s, LANES), lambda bi, h, i: (bi, h)),
                  pl.BlockSpec((s, LANES), lambda bi, h, i: (bi, h)),
                  pl.BlockSpec((None, n_pad, LANES), lambda bi, h, i: (bi, 0, h)),
                  pl.BlockSpec(expander.shape, lambda bi, h, i: (0, 0, 0))],
        out_specs=pl.BlockSpec((tq, LANES), lambda bi, h, i: (bi * nq + i, h)),
        scratch_shapes=st,
        compiler_params=pltpu.CompilerParams(
            dimension_semantics=("parallel", "parallel", "arbitrary"), vmem_limit_bytes=VMEM_LIMIT),
        name="moba_attention",
    )(q, k, v, km, expander)


def _layer_norm(z, g, b):
    mu = jnp.mean(z, axis=-1, keepdims=True)
    zc = z - mu
    var = jnp.mean(zc * zc, axis=-1, keepdims=True)
    return zc * lax.rsqrt(var + LN_EPS) * g + b


def _split_bf16(a):
    hi = a.astype(BF16)
    return hi, (a - hi.astype(F32)).astype(BF16)


def _route_experts(xn, rwh_ref, rwl_ref, rb_ref):
    tm = xn.shape[0]
    group = N_EXPERTS // N_EXPERT_GROUPS
    xh, xl = _split_bf16(xn)
    rwh = rwh_ref[...]
    logits = _nt_dot(rwh, xh) + _nt_dot(rwh, xl) + _nt_dot(rwl_ref[...], xh)
    s = jax.nn.sigmoid(logits)
    sb = s + jnp.concatenate([rb_ref[...]] * (tm // LANES), axis=1)
    grp = sb.reshape(N_EXPERT_GROUPS, group, tm)
    member = lax.broadcasted_iota(jnp.int32, grp.shape, 1).astype(F32)
    m1 = jnp.max(grp, axis=1, keepdims=True)
    first = jnp.min(jnp.where(grp == m1, member, float(group)), axis=1, keepdims=True)
    m2 = jnp.max(jnp.where(member == first, -jnp.inf, grp), axis=1, keepdims=True)
    group_ok = _rank_select_rows(m1 + m2, N_EXPERT_GROUPS, TOPK_GROUPS)
    cand = jnp.where(group_ok, grp, -jnp.inf).reshape(N_EXPERTS, tm)
    w = jnp.where(_rank_select_rows(cand, N_EXPERTS, TOP_K), s, 0.0)
    w = w / jnp.sum(w, axis=0, keepdims=True) * ROUTED_SCALE
    gate = _rows_to_lanes(w)
    lane = lax.broadcasted_iota(jnp.int32, gate.shape, 1)
    shared = (lane >= N_EXPERTS) & (lane < N_EXPERTS + D_SHARED // D_EXPERT)
    return jnp.where(shared, 1.0, gate)


def _merge_kernel(x_ref, od_ref, on_ref, om_ref, g0_ref, g1_ref, g2_ref,
                  wd_ref, wn_ref, wm_ref, wo_ref, ln_ref, rwh_ref, rwl_ref, rb_ref, o_ref, ghi_ref, glo_ref):
    def branch(o_r, w_r, g_r):
        y = jnp.dot(o_r[...].astype(BF16), w_r[...], preferred_element_type=F32)
        return jax.nn.sigmoid(g_r[...]) * y

    merged = branch(od_ref, wd_ref, g0_ref) + branch(on_ref, wn_ref, g1_ref) + branch(om_ref, wm_ref, g2_ref)
    y = jnp.dot(merged.astype(BF16), wo_ref[...], preferred_element_type=F32)
    z = DEEPNORM_ALPHA * x_ref[...] + y
    xn = _layer_norm(z, ln_ref[0:1, :], ln_ref[1:2, :])
    o_ref[...] = xn
    ghi_ref[...], glo_ref[...] = _split_bf16(_route_experts(xn, rwh_ref, rwl_ref, rb_ref))


def merge_project_norm(x, o_diff, o_nsa, o_moba, merge_gate, w_d, w_n, w_m, w_o, ln_g, ln_b,
                       router_w, router_b, tm=256):
    t = x.shape[0]
    ln = jnp.concatenate([ln_g.reshape(1, -1), ln_b.reshape(1, -1), jnp.zeros((6, D_MODEL), F32)], 0)
    rwh, rwl = _split_bf16(router_w.astype(F32).T)
    rb = jnp.broadcast_to(router_b.astype(F32)[:, None], (N_EXPERTS, LANES))
    row = lambda i: (i, 0)
    full = lambda i: (0, 0)
    return pl.pallas_call(
        _merge_kernel,
        out_shape=[jax.ShapeDtypeStruct((t, D_MODEL), F32), jax.ShapeDtypeStruct((t, LANES), BF16),
                   jax.ShapeDtypeStruct((t, LANES), BF16)],
        grid=(t // tm,),
        in_specs=[pl.BlockSpec((tm, D_MODEL), row),
                  pl.BlockSpec((tm, o_diff.shape[1]), row),
                  pl.BlockSpec((tm, o_nsa.shape[1]), row),
                  pl.BlockSpec((tm, o_moba.shape[1]), row),
                  pl.BlockSpec((tm, D_MODEL), lambda i: (i, 0)),
                  pl.BlockSpec((tm, D_MODEL), lambda i: (i, 1)),
                  pl.BlockSpec((tm, D_MODEL), lambda i: (i, 2)),
                  pl.BlockSpec(w_d.shape, full), pl.BlockSpec(w_n.shape, full),
                  pl.BlockSpec(w_m.shape, full), pl.BlockSpec(w_o.shape, full),
                  pl.BlockSpec((8, D_MODEL), full),
                  pl.BlockSpec(rwh.shape, full), pl.BlockSpec(rwl.shape, full), pl.BlockSpec(rb.shape, full)],
        out_specs=[pl.BlockSpec((tm, D_MODEL), row), pl.BlockSpec((tm, LANES), row), pl.BlockSpec((tm, LANES), row)],
        compiler_params=pltpu.CompilerParams(
            dimension_semantics=("parallel",), vmem_limit_bytes=VMEM_LIMIT),
        name="merge_project_norm",
    )(x, o_diff, o_nsa, o_moba, merge_gate, merge_gate, merge_gate, w_d, w_n, w_m, w_o, ln, rwh, rwl, rb)


def _moe_kernel(x_ref, ghi_ref, glo_ref, e_ref, wg_ref, wu_ref, wd_ref, ln_ref, o_ref, acc_ref):
    f = pl.program_id(1)

    @pl.when(f == 0)
    def _():
        acc_ref[...] = jnp.zeros(acc_ref.shape, F32)

    xb = x_ref[...].astype(BF16)
    n_e = wg_ref.shape[0]
    wg = jnp.concatenate([wg_ref[j] for j in range(n_e)], axis=1)
    wu = jnp.concatenate([wu_ref[j] for j in range(n_e)], axis=1)
    hg = jnp.dot(xb, wg, preferred_element_type=F32)
    hu = jnp.dot(xb, wu, preferred_element_type=F32)
    e = e_ref[...]
    gate = (jnp.dot(ghi_ref[...], e, preferred_element_type=F32)
            + jnp.dot(glo_ref[...], e, preferred_element_type=F32))
    h = jax.nn.silu(hg) * hu * gate
    acc_ref[...] += jnp.dot(h.astype(BF16), wd_ref[...], preferred_element_type=F32)

    @pl.when(f == pl.num_programs(1) - 1)
    def _():
        z = DEEPNORM_ALPHA * x_ref[...] + acc_ref[...]
        o_ref[...] = _layer_norm(z, ln_ref[0:1, :], ln_ref[1:2, :])


def moe_norm(x, ghi, glo, wg, wu, wd, ln_g, ln_b, tm=768, experts_per_step=6):
    t = x.shape[0]
    f_tot = wd.shape[0]
    tf = experts_per_step * D_EXPERT
    n_col = ghi.shape[1]
    assert wg.shape[0] % experts_per_step == 0 and t % tm == 0
    expander = (jnp.arange(f_tot)[None, :] // D_EXPERT == jnp.arange(n_col)[:, None]).astype(BF16)
    ln = jnp.concatenate([ln_g.reshape(1, -1), ln_b.reshape(1, -1), jnp.zeros((6, D_MODEL), F32)], 0)
    return pl.pallas_call(
        _moe_kernel,
        out_shape=jax.ShapeDtypeStruct((t, D_MODEL), F32),
        grid=(t // tm, f_tot // tf),
        in_specs=[pl.BlockSpec((tm, D_MODEL), lambda i, f: (i, 0)),
                  pl.BlockSpec((tm, n_col), lambda i, f: (i, 0)),
                  pl.BlockSpec((tm, n_col), lambda i, f: (i, 0)),
                  pl.BlockSpec((n_col, tf), lambda i, f: (0, f)),
                  pl.BlockSpec((experts_per_step, D_MODEL, D_EXPERT), lambda i, f: (f, 0, 0)),
                  pl.BlockSpec((experts_per_step, D_MODEL, D_EXPERT), lambda i, f: (f, 0, 0)),
                  pl.BlockSpec((tf, D_MODEL), lambda i, f: (f, 0)),
                  pl.BlockSpec((8, D_MODEL), lambda i, f: (0, 0))],
        out_specs=pl.BlockSpec((tm, D_MODEL), lambda i, f: (i, 0)),
        scratch_shapes=[pltpu.VMEM((tm, D_MODEL), F32)],
        compiler_params=pltpu.CompilerParams(
            dimension_semantics=("parallel", "arbitrary"), vmem_limit_bytes=VMEM_LIMIT),
        name="moe_norm",
    )(x, ghi, glo, expander, wg, wu, wd, ln)


TOK_PAD = 8
NEW_PAD = 16


def _softmax_two(s, sn):
    m = jnp.maximum(jnp.max(s, axis=-1, keepdims=True), jnp.max(sn, axis=-1, keepdims=True))
    p = jnp.exp(s - m)
    pn = jnp.exp(sn - m)
    inv = 1.0 / (jnp.sum(p, axis=-1, keepdims=True) + jnp.sum(pn, axis=-1, keepdims=True))
    return p * inv, pn * inv


def _new_row_mask(rows, n_new):
    t = lax.broadcasted_iota(jnp.int32, (rows, NEW_PAD), 0) & (TOK_PAD - 1)
    i = lax.broadcasted_iota(jnp.int32, (rows, NEW_PAD), 1)
    return (i <= t) & (i < n_new)


def _dec_diff_kernel(pt_ref, q_ref, new_ref, par_ref, *rest, n_pages, n_new, out_scale):
    pages = rest[:n_pages]
    o_ref, s_ref = rest[n_pages:]
    nqk = 2 * DIFF_HEADS * HEAD_DIM
    half = DIFF_HEADS * TOK_PAD
    q = q_ref[...]
    for j in range(n_pages):
        s_ref[:, j * PAGE_SIZE:(j + 1) * PAGE_SIZE] = _nt_dot(q, pages[j][:, 0:nqk].astype(BF16))
    sn = _nt_dot(q, new_ref[:, 0:nqk].astype(BF16))
    sn = jnp.where(_new_row_mask(2 * half, n_new), sn, NEG)
    p, pn = _softmax_two(s_ref[...], sn)
    lam = par_ref[0:1, 0:1]
    a = (p[0:half] - lam * p[half:2 * half]).astype(BF16)
    an = (pn[0:half] - lam * pn[half:2 * half]).astype(BF16)
    o = jnp.dot(an, new_ref[:, nqk:].astype(BF16), preferred_element_type=F32)
    for j in range(n_pages):
        o = o + jnp.dot(a[:, j * PAGE_SIZE:(j + 1) * PAGE_SIZE], pages[j][:, nqk:].astype(BF16),
                        preferred_element_type=F32)
    g = par_ref[1:2, :]
    for h in range(DIFF_HEADS):
        oh = o[h * TOK_PAD:(h + 1) * TOK_PAD, h * LANES:(h + 1) * LANES]
        oh = oh * lax.rsqrt(jnp.mean(oh * oh, axis=-1, keepdims=True) + RMS_EPS) * g
        o_ref[:, h * LANES:(h + 1) * LANES] = oh * out_scale


def _page_specs(layer, n_pages, width):
    return [pl.BlockSpec((None, None, PAGE_SIZE, width),
                         lambda b, pt, j=j: (layer, pt[b * n_pages + j], 0, 0)) for j in range(n_pages)]


def _per_seq(shape):
    return pl.BlockSpec((None,) + shape, lambda b, pt: (b,) + (0,) * len(shape))


def _shared(shape):
    return pl.BlockSpec(shape, lambda b, pt: (0,) * len(shape))


def _pad_rows(a, n):
    return jnp.pad(a, ((0, 0), (0, n - a.shape[1])) + ((0, 0),) * (a.ndim - 2))


def decode_diff_attention(layer, page_table, cache, da_q, rows_new, lam, subln_g, lambda_init):
    b, n_new, _ = da_q.shape
    n_pages = page_table.shape[1]
    q = da_q.astype(F32).reshape(b, n_new, DIFF_HEADS, 2, HEAD_DIM)
    q = _pad_rows(jnp.transpose(q, (0, 3, 2, 1, 4)).reshape(b * 2 * DIFF_HEADS, n_new, HEAD_DIM), TOK_PAD)
    q = q.reshape(b, 2, DIFF_HEADS, TOK_PAD, 1, HEAD_DIM)
    head = 2 * jnp.arange(DIFF_HEADS)[None, :] + jnp.arange(2)[:, None]
    place = (head[:, :, None] == jnp.arange(2 * DIFF_HEADS)).astype(F32)
    qbd = (q * place[None, :, :, None, :, None]).reshape(b, 2 * DIFF_HEADS * TOK_PAD, 2 * DIFF_HEADS * HEAD_DIM)
    par = jnp.concatenate([jnp.broadcast_to(lam.astype(F32), (1, LANES)), subln_g.astype(F32).reshape(1, LANES),
                           jnp.zeros((6, LANES), F32)], 0)
    rows = 2 * DIFF_HEADS * TOK_PAD
    width = cache.shape[-1]
    out = pl.pallas_call(
        functools.partial(_dec_diff_kernel, n_pages=n_pages, n_new=n_new, out_scale=1.0 - lambda_init),
        out_shape=jax.ShapeDtypeStruct((b, TOK_PAD, DIFF_HEADS * LANES), F32),
        grid_spec=pltpu.PrefetchScalarGridSpec(
            num_scalar_prefetch=1, grid=(b,),
            in_specs=[_per_seq((rows, 2 * DIFF_HEADS * HEAD_DIM)), _per_seq((NEW_PAD, width)), _shared((8, LANES))]
            + _page_specs(layer, n_pages, width),
            out_specs=_per_seq((TOK_PAD, DIFF_HEADS * LANES)),
            scratch_shapes=[pltpu.VMEM((rows, n_pages * PAGE_SIZE), F32)]),
        compiler_params=pltpu.CompilerParams(dimension_semantics=("arbitrary",), vmem_limit_bytes=VMEM_LIMIT),
        name="decode_diff_attention",
    )(page_table.reshape(-1), qbd.astype(BF16), _pad_rows(rows_new, NEW_PAD), par, *([cache] * n_pages))
    return out[:, :n_new]


def _dec_moba_kernel(pt_ref, q_ref, new_ref, *rest, n_pages, n_new, past_len):
    pages = rest[:n_pages]
    o_ref, s_ref, km_ref = rest[n_pages:]
    w = MOBA_HEADS * HEAD_DIM
    rows = MOBA_HEADS * TOK_PAD
    ppb = MOBA_BLOCK // PAGE_SIZE
    n_blk = n_pages // ppb
    q = q_ref[...]
    km_ref[...] = jnp.zeros(km_ref.shape, F32)
    for n in range(n_blk):
        tot = jnp.sum(pages[ppb * n][:, 0:w], axis=0, keepdims=True)
        for j in range(ppb * n + 1, ppb * (n + 1)):
            tot = tot + jnp.sum(pages[j][:, 0:w], axis=0, keepdims=True)
        km_ref[n:n + 1, :] = tot / MOBA_BLOCK
    gate = _nt_dot(q, km_ref[...].astype(BF16))
    lane = lax.broadcasted_iota(jnp.int32, gate.shape, 1)
    t = lax.broadcasted_iota(jnp.int32, gate.shape, 0) & (TOK_PAD - 1)
    earlier = (lane < (past_len + t) // MOBA_BLOCK) & (lane < n_blk)
    gate = jnp.where(earlier, gate, -jnp.inf)
    chosen = jnp.where(earlier & _rank_select(gate, n_blk, MOBA_TOPK), 1.0, 0.0)
    for j in range(n_pages):
        sj = _nt_dot(q, pages[j][:, 0:w].astype(BF16))
        n = j // ppb
        s_ref[:, j * PAGE_SIZE:(j + 1) * PAGE_SIZE] = jnp.where(chosen[:, n:n + 1] > 0.5, sj, NEG)
    sn = jnp.where(_new_row_mask(rows, n_new), _nt_dot(q, new_ref[:, 0:w].astype(BF16)), NEG)
    p, pn = _softmax_two(s_ref[...], sn)
    p = p.astype(BF16)
    o = jnp.dot(pn.astype(BF16), new_ref[:, w:].astype(BF16), preferred_element_type=F32)
    for j in range(n_pages):
        o = o + jnp.dot(p[:, j * PAGE_SIZE:(j + 1) * PAGE_SIZE], pages[j][:, w:].astype(BF16),
                        preferred_element_type=F32)
    for h in range(MOBA_HEADS):
        o_ref[:, h * HEAD_DIM:(h + 1) * HEAD_DIM] = o[h * TOK_PAD:(h + 1) * TOK_PAD, h * HEAD_DIM:(h + 1) * HEAD_DIM]


def _head_tiles(a, n_heads):
    b, n_new, _ = a.shape
    a = a.astype(F32).reshape(b, n_new, n_heads, HEAD_DIM)
    return _pad_rows(jnp.transpose(a, (0, 2, 1, 3)).reshape(b * n_heads, n_new, HEAD_DIM), TOK_PAD).reshape(
        b, n_heads, TOK_PAD, HEAD_DIM)


def decode_moba_attention(layer, page_table, cache, m_q, rows_new):
    b, n_new, _ = m_q.shape
    n_pages = page_table.shape[1]
    past_len = n_pages * PAGE_SIZE
    assert past_len % MOBA_BLOCK == 0 and n_new <= TOK_PAD and n_pages * PAGE_SIZE // MOBA_BLOCK <= NEW_PAD
    q = _head_tiles(m_q, MOBA_HEADS)
    place = jnp.eye(MOBA_HEADS, dtype=F32)
    q = (q[:, :, :, None, :] * place[None, :, None, :, None]).reshape(b, MOBA_HEADS * TOK_PAD, MOBA_HEADS * HEAD_DIM)
    rows = MOBA_HEADS * TOK_PAD
    width = cache.shape[-1]
    out = pl.pallas_call(
        functools.partial(_dec_moba_kernel, n_pages=n_pages, n_new=n_new, past_len=past_len),
        out_shape=jax.ShapeDtypeStruct((b, TOK_PAD, MOBA_HEADS * HEAD_DIM), F32),
        grid_spec=pltpu.PrefetchScalarGridSpec(
            num_scalar_prefetch=1, grid=(b,),
            in_specs=[_per_seq((rows, MOBA_HEADS * HEAD_DIM)), _per_seq((NEW_PAD, width))]
            + _page_specs(layer, n_pages, width),
            out_specs=_per_seq((TOK_PAD, MOBA_HEADS * HEAD_DIM)),
            scratch_shapes=[pltpu.VMEM((rows, past_len), F32), pltpu.VMEM((NEW_PAD, MOBA_HEADS * HEAD_DIM), F32)]),
        compiler_params=pltpu.CompilerParams(dimension_semantics=("arbitrary",), vmem_limit_bytes=VMEM_LIMIT),
        name="decode_moba_attention",
    )(page_table.reshape(-1), q.astype(BF16), _pad_rows(rows_new, NEW_PAD), *([cache] * n_pages))
    return out[:, :n_new]


def _dec_nsa_kernel(pt_ref, qn_ref, qr_ref, gate_ref, new_ref, wnew_ref, win_ref,
                    wc_ref, cb_ref, w2_ref, ov_ref, ex_ref, *rest, n_pages, n_new, past_len, win_pos0):
    pages = rest[:n_pages]
    o_ref, cmp_ref, slc_ref = rest[n_pages:]
    rows = NSA_HEADS * TOK_PAD
    n_chunk = past_len // CMP_STRIDE
    n_cmp = (past_len + n_new - CMP_LEN) // CMP_STRIDE + 1
    n_slc = -(-(past_len + n_new) // SLC_LEN)
    hid = cb_ref.shape[1] // 2
    for j in range(n_pages):
        cmp_ref[j * PAGE_SIZE:(j + 1) * PAGE_SIZE, :] = pages[j][:, 0:2 * HEAD_DIM]
        slc_ref[j * PAGE_SIZE:(j + 1) * PAGE_SIZE, :] = pages[j][:, 2 * HEAD_DIM:4 * HEAD_DIM]

    y = jnp.zeros((n_chunk, 4 * hid), F32)
    for r in range(CMP_STRIDE):
        xr = cmp_ref[pl.ds(r, n_chunk, stride=CMP_STRIDE), :].astype(BF16)
        y = y + jnp.dot(xr, wc_ref[r], preferred_element_type=F32)
    hk = y[:, 0:hid] + pltpu.roll(y[:, hid:2 * hid], n_chunk - 1, 0) + cb_ref[0:1, 0:hid]
    hv = y[:, 2 * hid:3 * hid] + pltpu.roll(y[:, 3 * hid:4 * hid], n_chunk - 1, 0) + cb_ref[0:1, hid:2 * hid]
    cc = jnp.dot(jax.nn.gelu(jnp.concatenate([hk, hv], axis=1)).astype(BF16), w2_ref[...],
                 preferred_element_type=F32).astype(BF16)

    t = lax.broadcasted_iota(jnp.int32, (rows, n_chunk), 0) & (TOK_PAD - 1)
    n_idx = lax.broadcasted_iota(jnp.int32, (rows, n_chunk), 1)
    ok = (n_idx * CMP_STRIDE + CMP_LEN - 1 <= past_len + t) & (n_idx < n_cmp)
    sc = jnp.where(ok, _nt_dot(qn_ref[...], cc), NEG)
    pc = jnp.exp(sc - jnp.max(sc, axis=-1, keepdims=True))
    pc = pc / jnp.sum(pc, axis=-1, keepdims=True)
    p_hi = pc.astype(BF16)
    p_lo = (pc - p_hi.astype(F32)).astype(BF16)
    o_cmp = jnp.dot(p_hi, cc, preferred_element_type=F32)[:, HEAD_DIM:]
    ov = ov_ref[...]
    imp = jnp.dot(p_hi, ov, preferred_element_type=F32) + jnp.dot(p_lo, ov, preferred_element_type=F32)
    imp = imp[0:TOK_PAD] + imp[TOK_PAD:2 * TOK_PAD] + imp[2 * TOK_PAD:3 * TOK_PAD] + imp[3 * TOK_PAD:4 * TOK_PAD]
    blk = lax.broadcasted_iota(jnp.int32, imp.shape, 1)
    cur = (past_len + lax.broadcasted_iota(jnp.int32, imp.shape, 0)) // SLC_LEN
    valid = blk <= cur
    forced = (blk == 0) | (blk == cur) | (blk == cur - 1)
    score = jnp.where(forced, jnp.inf, jnp.where(valid, imp, -jnp.inf))
    chosen = jnp.where(valid & _rank_select(score, n_slc, min(SLC_TOPN, n_slc)), 1.0, 0.0).astype(BF16)
    key_ok = jnp.dot(chosen, ex_ref[...], preferred_element_type=F32)
    key_ok = jnp.concatenate([key_ok] * NSA_HEADS, axis=0)
    new_mask = _new_row_mask(rows, n_new)

    qr = qr_ref[...]
    kv = slc_ref[...].astype(BF16)
    kv_new = new_ref[:, 2 * HEAD_DIM:4 * HEAD_DIM].astype(BF16)
    p, pn = _softmax_two(jnp.where(key_ok > 0.5, _nt_dot(qr, kv), NEG), jnp.where(new_mask, _nt_dot(qr, kv_new), NEG))
    o_slc = (jnp.dot(p.astype(BF16), kv, preferred_element_type=F32)
             + jnp.dot(pn.astype(BF16), kv_new, preferred_element_type=F32))[:, HEAD_DIM:]

    kv = win_ref[...].astype(BF16)
    kv_new = wnew_ref[...].astype(BF16)
    n_win = kv.shape[0]
    wpos = win_pos0 + lax.broadcasted_iota(jnp.int32, (rows, n_win), 1)
    qpos = past_len + (lax.broadcasted_iota(jnp.int32, (rows, n_win), 0) & (TOK_PAD - 1))
    ok = (wpos <= qpos) & (wpos > qpos - WINDOW)
    p, pn = _softmax_two(jnp.where(ok, _nt_dot(qr, kv), NEG), jnp.where(new_mask, _nt_dot(qr, kv_new), NEG))
    o_win = (jnp.dot(p.astype(BF16), kv, preferred_element_type=F32)
             + jnp.dot(pn.astype(BF16), kv_new, preferred_element_type=F32))[:, HEAD_DIM:]

    g = jax.nn.sigmoid(gate_ref[...])
    o = g[:, 0:1] * o_cmp + g[:, 1:2] * o_slc + g[:, 2:3] * o_win
    for h in range(NSA_HEADS):
        o_ref[:, h * HEAD_DIM:(h + 1) * HEAD_DIM] = o[h * TOK_PAD:(h + 1) * TOK_PAD]


def decode_nsa_attention(layer, page_table, cache, win_state, n_q, nq_r, n_gate, rows_new, rows_win,
                         cmp_pos, cmp_w1, cmp_w2):
    b, n_new, _ = n_q.shape
    n_pages = page_table.shape[1]
    past_len = n_pages * PAGE_SIZE
    n_win = win_state.shape[2]
    assert past_len % SLC_LEN == 0 and n_new < CMP_STRIDE and past_len >= CMP_LEN and n_new <= TOK_PAD
    n_chunk = past_len // CMP_STRIDE
    assert n_chunk == LANES, "compressed tokens are laid out on one lane tile"
    hid = cmp_w1.shape[-1]
    lane_pad = lambda a: jnp.pad(a, ((0, 0),) * (a.ndim - 1) + ((0, LANES - a.shape[-1]),))
    qn = lane_pad(_head_tiles(n_q, NSA_HEADS)).reshape(b, -1, LANES).astype(BF16)
    qr = lane_pad(_head_tiles(nq_r, NSA_HEADS)).reshape(b, -1, LANES).astype(BF16)
    g = jnp.transpose(n_gate[..., :N_GATE].reshape(b, n_new, NSA_HEADS, 3), (0, 2, 1, 3))
    g = lane_pad(_pad_rows(g.reshape(b * NSA_HEADS, n_new, 3), TOK_PAD)).reshape(b, -1, LANES)
    w1 = cmp_w1.reshape(2, 2, CMP_STRIDE, HEAD_DIM, hid)
    zero = jnp.zeros((CMP_STRIDE, HEAD_DIM, 2 * hid), F32)
    top = jnp.concatenate([w1[0, 0], w1[0, 1], zero], axis=-1)
    bot = jnp.concatenate([zero, w1[1, 0], w1[1, 1]], axis=-1)
    wc = jnp.concatenate([top, bot], axis=1).astype(BF16)
    bias = jnp.concatenate([cmp_pos[0].reshape(1, -1) @ cmp_w1[0], cmp_pos[1].reshape(1, -1) @ cmp_w1[1]], -1)
    cb = jnp.concatenate([bias, jnp.zeros((7, 2 * hid), F32)], 0)
    zw = jnp.zeros((hid, HEAD_DIM), F32)
    w2 = jnp.concatenate([jnp.concatenate([cmp_w2[0], zw], 1), jnp.concatenate([zw, cmp_w2[1]], 1)], 0).astype(BF16)
    cmp_start = jnp.arange(n_chunk) * CMP_STRIDE
    slc_start = jnp.arange(LANES) * SLC_LEN
    ov = ((cmp_start[:, None] <= slc_start[None, :] + SLC_LEN - 1)
          & (cmp_start[:, None] + CMP_LEN - 1 >= slc_start[None, :])).astype(BF16)
    ex = (jnp.arange(past_len)[None, :] // SLC_LEN == jnp.arange(LANES)[:, None]).astype(BF16)
    rows = NSA_HEADS * TOK_PAD
    width = cache.shape[-1]
    out = pl.pallas_call(
        functools.partial(_dec_nsa_kernel, n_pages=n_pages, n_new=n_new, past_len=past_len,
                          win_pos0=past_len - n_win),
        out_shape=jax.ShapeDtypeStruct((b, TOK_PAD, NSA_HEADS * HEAD_DIM), F32),
        grid_spec=pltpu.PrefetchScalarGridSpec(
            num_scalar_prefetch=1, grid=(b,),
            in_specs=[_per_seq((rows, LANES)), _per_seq((rows, LANES)), _per_seq((rows, LANES)),
                      _per_seq((NEW_PAD, width)), _per_seq((NEW_PAD, 2 * HEAD_DIM)),
                      pl.BlockSpec((None, None, n_win, 2 * HEAD_DIM), lambda bi, pt: (layer, bi, 0, 0)),
                      _shared(wc.shape), _shared(cb.shape), _shared(w2.shape), _shared(ov.shape), _shared(ex.shape)]
            + _page_specs(layer, n_pages, width),
            out_specs=_per_seq((TOK_PAD, NSA_HEADS * HEAD_DIM)),
            scratch_shapes=[pltpu.VMEM((past_len, 2 * HEAD_DIM), F32), pltpu.VMEM((past_len, 2 * HEAD_DIM), F32)]),
        compiler_params=pltpu.CompilerParams(dimension_semantics=("arbitrary",), vmem_limit_bytes=VMEM_LIMIT),
        name="decode_nsa_attention",
    )(page_table.reshape(-1), qn, qr, g, _pad_rows(rows_new, NEW_PAD), _pad_rows(rows_win, NEW_PAD), win_state,
      wc, cb, w2, ov, ex, *([cache] * n_pages))
    return out[:, :n_new]


def _compress(kv, pos_emb, w1, w2):
    b, l, _ = kv.shape
    n_chunk = l // CMP_STRIDE
    n_cmp = (l - CMP_LEN) // CMP_STRIDE + 1
    x = kv[:, :n_chunk * CMP_STRIDE].reshape(b * n_chunk, CMP_STRIDE * HEAD_DIM)
    half = CMP_STRIDE * HEAD_DIM
    w_cat = jnp.concatenate([w1[:half], w1[half:]], axis=1).astype(BF16)
    rows = x.shape[0]
    tm = 512 if rows % 512 == 0 else rows
    y = matmul(x, w_cat, tm, w_cat.shape[1]).reshape(b, n_chunk, 2, w1.shape[1])
    bias = pos_emb.reshape(1, -1) @ w1
    hid = y[:, :n_cmp, 0] + y[:, 1:n_cmp + 1, 1] + bias
    return jax.nn.gelu(hid) @ w2


def kernel(x_prompt, x_sample, cache_diff, cache_nsa, cache_moba, state_nsa_win, page_table, w_in, diff_lambda, diff_subln, nsa_cmp_pos, nsa_cmp_w1, nsa_cmp_w2, w_br_diff, w_br_nsa, w_br_moba, w_out, ln1_g, ln1_b, ln2_g, ln2_b, router_w, router_b, exp_w_gate, exp_w_up, exp_w_down, sh_w_gate, sh_w_up, sh_w_down):
    bp, sp, _ = x_prompt.shape
    bs, ss, _ = x_sample.shape
    tp, ts = bp * sp, bs * ss
    n_pages = page_table.shape[1]
    past_len = n_pages * PAGE_SIZE
    pos_p = jnp.arange(sp, dtype=jnp.int32)
    pos_s = past_len + jnp.arange(ss, dtype=jnp.int32)
    pos = jnp.concatenate([jnp.tile(pos_p, bp), jnp.tile(pos_s, bs)])
    inv = ROPE_THETA ** (-jnp.arange(HALF, dtype=F32) / HALF)
    ang = pos.astype(F32)[:, None] * inv[None, :]
    cos, sin = jnp.cos(ang), jnp.sin(ang)

    x = jnp.concatenate([x_prompt.reshape(tp, D_MODEL), x_sample.reshape(ts, D_MODEL)], 0)
    st_p = [[], [], [], []]
    st_s = [[], [], [], []]
    for l in range(DEPTH):
        lambda_init = 0.8 - 0.6 * math.exp(-0.3 * l)
        w_pad = jnp.concatenate([w_in[l][:, :C_MQ_SRC],
                                 jnp.zeros((D_MODEL, C_MQ - C_MQ_SRC), F32),
                                 w_in[l][:, C_MQ_SRC:]], axis=1).astype(BF16)
        pj = project_inputs(x, w_pad, cos, sin)
        rows_diff, rows_nsa, rows_moba, rows_win = pj["rows_diff"], pj["rows_nsa"], pj["rows_moba"], pj["rows_win"]

        lp_ = diff_lambda[l].astype(F32)
        lam = jnp.exp(jnp.sum(lp_[0] * lp_[1])) - jnp.exp(jnp.sum(lp_[2] * lp_[3])) + lambda_init

        def pr(a):
            return a[:tp].reshape(bp, sp, a.shape[-1])

        o_diff_p = diff_attention(pj["q_diff"], pj["k_diff"], pj["v_diff"], bp, sp, lam, diff_subln[l], lambda_init)
        ck_c = _compress(pr(rows_nsa[:, 0:HEAD_DIM]), nsa_cmp_pos[l, 0], nsa_cmp_w1[l, 0], nsa_cmp_w2[l, 0])
        cv_c = _compress(pr(rows_nsa[:, HEAD_DIM:2 * HEAD_DIM]), nsa_cmp_pos[l, 1], nsa_cmp_w1[l, 1],
                         nsa_cmp_w2[l, 1])
        o_nsa_p = nsa_attention(pj["q_nsa"], pj["q_nsa_rot"], jnp.concatenate([ck_c, cv_c], -1).astype(BF16),
                                pj["kv_slc"], pj["kv_win"], pj["nsa_gate"], bp, sp)
        k_mean = jnp.mean(pr(rows_moba[:, 0:MOBA_HEADS * HEAD_DIM]).reshape(bp, sp // MOBA_BLOCK, MOBA_BLOCK, -1),
                          axis=2)
        o_moba_p = moba_attention(pj["q_moba"], pj["k_moba"], pj["v_moba"], k_mean, bp, sp)

        def sm(a):
            return a[tp:].reshape(bs, ss, a.shape[-1])

        o_diff_s = decode_diff_attention(l, page_table, cache_diff, sm(pj["q_diff"]), sm(rows_diff),
                                         lam, diff_subln[l], lambda_init)
        o_nsa_s = decode_nsa_attention(l, page_table, cache_nsa, state_nsa_win, sm(pj["q_nsa"]),
                                       sm(pj["q_nsa_rot"]), sm(pj["nsa_gate"]), sm(rows_nsa), sm(rows_win),
                                       nsa_cmp_pos[l], nsa_cmp_w1[l], nsa_cmp_w2[l])
        o_moba_s = decode_moba_attention(l, page_table, cache_moba, sm(pj["q_moba"]), sm(rows_moba))

        def both(a_p, a_s):
            return jnp.concatenate([a_p.reshape(tp, -1), a_s.reshape(ts, -1)], 0)

        x, ghi, glo = merge_project_norm(
            x, both(o_diff_p, o_diff_s), both(o_nsa_p, o_nsa_s), both(o_moba_p, o_moba_s),
            pj["merge_gate"], w_br_diff[l].astype(BF16), w_br_nsa[l].astype(BF16),
            w_br_moba[l].astype(BF16), w_out[l].astype(BF16), ln1_g[l], ln1_b[l], router_w[l], router_b[l])

        def with_shared(w_exp, w_sh):
            w_sh = jnp.transpose(w_sh.reshape(D_MODEL, D_SHARED // D_EXPERT, D_EXPERT), (1, 0, 2))
            return jnp.concatenate([w_exp, w_sh], 0).astype(BF16)

        wd = jnp.concatenate([exp_w_down[l].reshape(-1, D_MODEL), sh_w_down[l]], 0).astype(BF16)
        x = moe_norm(x, ghi, glo, with_shared(exp_w_gate[l], sh_w_gate[l]), with_shared(exp_w_up[l], sh_w_up[l]),
                     wd, ln2_g[l], ln2_b[l])

        for lst, val in zip(st_p, (rows_diff, rows_nsa, rows_moba)):
            lst.append(pr(val))
        n_keep = min(WINDOW, sp)
        st_p[3].append(pr(rows_win)[:, sp - n_keep:])
        for lst, val in zip(st_s, (rows_diff, rows_nsa, rows_moba)):
            lst.append(sm(val))
        full_win = jnp.concatenate([state_nsa_win[l], sm(rows_win)], 1)
        n_keep = min(WINDOW, past_len + ss)
        st_s[3].append(full_win[:, full_win.shape[1] - n_keep:])

    outs_p = [jnp.stack(a, 0) for a in st_p]
    outs_s = [jnp.stack(a, 0) for a in st_s]
    return (x[:tp].reshape(bp, sp, D_MODEL), x[tp:].reshape(bs, ss, D_MODEL),
            outs_p[0], outs_s[0], outs_p[1], outs_s[1], outs_p[2], outs_s[2], outs_p[3], outs_s[3])
```

```python
import functools
import math

import jax
import jax.numpy as jnp
from jax import lax
from jax.experimental import pallas as pl
from jax.experimental.pallas import tpu as pltpu

F32 = jnp.float32
BF16 = jnp.bfloat16

D_MODEL = 1024
DEPTH = 2
PAGE_SIZE = 128
HEAD_DIM = 64
HALF = HEAD_DIM // 2
ATTN_SCALE = HEAD_DIM ** -0.5
ROPE_THETA = 10000.0
DIFF_HEADS = 4
NSA_HEADS = 4
CMP_LEN = 32
CMP_STRIDE = 16
SLC_LEN = 64
SLC_TOPN = 16
WINDOW = 512
MOBA_HEADS = 4
MOBA_BLOCK = 256
MOBA_TOPK = 3
N_EXPERTS = 64
N_EXPERT_GROUPS = 8
TOPK_GROUPS = 4
TOP_K = 6
D_EXPERT = 128
D_SHARED = 256
ROUTED_SCALE = 2.5
LN_EPS = 1e-5
RMS_EPS = 1e-5
DEEPNORM_ALPHA = (2 * DEPTH) ** 0.25

C_DAQ, C_DAK, C_DAV, C_NQ = 0, 512, 1024, 1536
C_CK, C_CV, C_SK, C_SV, C_WK, C_WV, C_NG = 1792, 1856, 1920, 1984, 2048, 2112, 2176
N_GATE = 3 * NSA_HEADS
C_MQ_SRC = C_NG + N_GATE
C_MQ, C_MK, C_MV, C_MG = 2304, 2560, 2816, 3072
N_IN_PAD = C_MG + 3 * D_MODEL

LANES = 128
NEG = -1e30
VMEM_LIMIT = 56 * 1024 * 1024

def _nt_dot(a, b):
    return lax.dot_general(a, b, (((1,), (1,)), ((), ())), preferred_element_type=F32)


def _mm_kernel(x_ref, w_ref, o_ref):
    o_ref[...] = jnp.dot(x_ref[...].astype(BF16), w_ref[...],
                         preferred_element_type=F32).astype(o_ref.dtype)


def matmul(x, w, tm, tn, out_dtype=F32):
    m, k = x.shape
    n = w.shape[1]
    assert m % tm == 0 and n % tn == 0, (x.shape, w.shape, tm, tn)
    return pl.pallas_call(
        _mm_kernel,
        out_shape=jax.ShapeDtypeStruct((m, n), out_dtype),
        grid=(m // tm, n // tn),
        in_specs=[pl.BlockSpec((tm, k), lambda i, j: (i, 0)),
                  pl.BlockSpec((k, tn), lambda i, j: (0, j))],
        out_specs=pl.BlockSpec((tm, tn), lambda i, j: (i, j)),
        compiler_params=pltpu.CompilerParams(
            dimension_semantics=("parallel", "arbitrary"), vmem_limit_bytes=VMEM_LIMIT),
        name="matmul",
    )(x, w)


def _repack_kernel(lo_ref, hi_ref, o_ref, *, first_moved_tile, shift):
    j = pl.program_id(0)
    lane = lax.broadcasted_iota(jnp.int32, lo_ref.shape, 1)
    hi = hi_ref[...]
    kept = jnp.where((j < first_moved_tile - 1) | (lane < LANES - shift), hi, 0.0)
    moved = jnp.where(lane < shift, pltpu.roll(lo_ref[...], shift, 1), pltpu.roll(hi, shift, 1))
    o_ref[...] = jnp.where(j < first_moved_tile, kept, moved).astype(o_ref.dtype)


def repack_input_weight(w):
    d, n_src = w.shape
    shift = C_MQ - C_MQ_SRC
    last = (n_src - 1) // LANES
    assert C_MQ % LANES == 0 and 0 < shift < LANES and n_src + shift == N_IN_PAD
    return pl.pallas_call(
        functools.partial(_repack_kernel, first_moved_tile=C_MQ // LANES, shift=shift),
        out_shape=jax.ShapeDtypeStruct((d, N_IN_PAD), BF16),
        grid=(N_IN_PAD // LANES,),
        in_specs=[pl.BlockSpec((d, LANES), lambda j: (0, jnp.maximum(j - 1, 0))),
                  pl.BlockSpec((d, LANES), lambda j: (0, jnp.minimum(j, last)))],
        out_specs=pl.BlockSpec((d, LANES), lambda j: (0, j)),
        compiler_params=pltpu.CompilerParams(dimension_semantics=("parallel",), vmem_limit_bytes=VMEM_LIMIT),
        name="repack_input_weight",
    )(w, w)


def _project_kernel(x_ref, w_ref, cos_ref, sa_ref, sb_ref,
                    rd_ref, rn_ref, rm_ref, rw_ref, mg_ref, ng_ref,
                    qd_ref, kd_ref, vd_ref, nq_ref, nqr_ref, ss_ref, ww_ref, mq_ref, mk_ref, mv_ref):
    xb = x_ref[...].astype(BF16)
    cos, sa, sb = cos_ref[...], sa_ref[...], sb_ref[...]

    def seg(a, b):
        return jnp.dot(xb, w_ref[:, a:b], preferred_element_type=F32)

    def rope(y):
        tiles = []
        for t in range(y.shape[1] // LANES):
            yt = y[:, t * LANES:(t + 1) * LANES]
            tiles.append(yt * cos + pltpu.roll(yt, LANES - HALF, 1) * sa + pltpu.roll(yt, HALF, 1) * sb)
        return tiles[0] if len(tiles) == 1 else jnp.concatenate(tiles, axis=1)

    first = lax.broadcasted_iota(jnp.int32, (x_ref.shape[0], LANES), 1) < HEAD_DIM
    qd_ref[...] = (rope(seg(C_DAQ, C_DAK)) * ATTN_SCALE).astype(BF16)
    k = rope(seg(C_DAK, C_DAV))
    v = seg(C_DAV, C_NQ)
    rd_ref[:, 0:C_DAV - C_DAK] = k
    rd_ref[:, C_DAV - C_DAK:] = v
    kd_ref[...] = k.astype(BF16)
    vd_ref[...] = v.astype(BF16)
    nq = seg(C_NQ, C_CK)
    nq_ref[...] = (nq * ATTN_SCALE).astype(BF16)
    nqr_ref[...] = (rope(nq) * ATTN_SCALE).astype(BF16)
    y = seg(C_CK, C_NG)
    ss = jnp.where(first, rope(y[:, LANES:2 * LANES]), y[:, LANES:2 * LANES])
    ww = jnp.where(first, rope(y[:, 2 * LANES:3 * LANES]), y[:, 2 * LANES:3 * LANES])
    rn_ref[:, 0:LANES] = y[:, 0:LANES]
    rn_ref[:, LANES:] = ss
    rw_ref[...] = ww
    ss_ref[...] = ss.astype(BF16)
    ww_ref[...] = ww.astype(BF16)
    ng_ref[...] = seg(C_NG, C_MQ)
    mq_ref[...] = (rope(seg(C_MQ, C_MK)) * ATTN_SCALE).astype(BF16)
    k = rope(seg(C_MK, C_MV))
    v = seg(C_MV, C_MG)
    rm_ref[:, 0:C_MV - C_MK] = k
    rm_ref[:, C_MV - C_MK:] = v
    mk_ref[...] = k.astype(BF16)
    mv_ref[...] = v.astype(BF16)
    for t in range(3):
        mg_ref[:, t * D_MODEL:(t + 1) * D_MODEL] = seg(C_MG + t * D_MODEL, C_MG + (t + 1) * D_MODEL)


def project_inputs(x, w_pad, cos, sin, tm=256):
    t = x.shape[0]
    zero = jnp.zeros_like(sin)
    cos_t = jnp.tile(cos, (1, LANES // HALF))
    sa = jnp.concatenate([-sin, zero, -sin, zero], axis=1)
    sb = jnp.concatenate([zero, sin, zero, sin], axis=1)
    names_f32 = (("rows_diff", 1024), ("rows_nsa", 256), ("rows_moba", 512), ("rows_win", 128),
                 ("merge_gate", 3 * D_MODEL), ("nsa_gate", LANES))
    names_bf16 = (("q_diff", 512), ("k_diff", 512), ("v_diff", 512), ("q_nsa", 256), ("q_nsa_rot", 256),
                  ("kv_slc", 128), ("kv_win", 128), ("q_moba", 256), ("k_moba", 256), ("v_moba", 256))
    row = lambda i: (i, 0)
    outs = pl.pallas_call(
        _project_kernel,
        out_shape=[jax.ShapeDtypeStruct((t, n), F32) for _, n in names_f32]
        + [jax.ShapeDtypeStruct((t, n), BF16) for _, n in names_bf16],
        grid=(t // tm,),
        in_specs=[pl.BlockSpec((tm, D_MODEL), row),
                  pl.BlockSpec(w_pad.shape, lambda i: (0, 0), pipeline_mode=pl.Buffered(1)),
                  pl.BlockSpec((tm, LANES), row), pl.BlockSpec((tm, LANES), row), pl.BlockSpec((tm, LANES), row)],
        out_specs=[pl.BlockSpec((tm, n), row) for _, n in names_f32 + names_bf16],
        compiler_params=pltpu.CompilerParams(dimension_semantics=("parallel",), vmem_limit_bytes=VMEM_LIMIT),
        name="project_inputs",
    )(x, w_pad, cos_t, sa, sb)
    return dict(zip([n for n, _ in names_f32 + names_bf16], outs))


def _online_update(s, v, m_ref, l_ref, acc_ref):
    tk = s.shape[1]
    dv = acc_ref.shape[-1]
    m_prev = m_ref[...]
    m_next = jnp.maximum(m_prev, jnp.max(s, axis=-1, keepdims=True))
    alpha = jnp.exp(m_prev - m_next)
    p = jnp.exp(s - jnp.concatenate([m_next] * (tk // LANES), axis=1))
    l_ref[...] = alpha * l_ref[...] + jnp.sum(p, axis=-1, keepdims=True)
    m_ref[...] = m_next
    acc_ref[...] = acc_ref[...] * alpha[:, :dv] + jnp.dot(p.astype(BF16), v, preferred_element_type=F32)


def _init_state(*refs):
    for m_ref, l_ref, acc_ref in zip(refs[0::3], refs[1::3], refs[2::3]):
        m_ref[...] = jnp.full(m_ref.shape, NEG, F32)
        l_ref[...] = jnp.zeros(l_ref.shape, F32)
        acc_ref[...] = jnp.zeros(acc_ref.shape, F32)


def _normalised(l_ref, acc_ref):
    dv = acc_ref.shape[-1]
    return acc_ref[...] / jnp.maximum(l_ref[...], 1e-30)[:, :dv]


def _diff_attn_kernel(q_ref, k_ref, v_ref, par_ref, o_ref,
                      m0, l0, a0, m1, l1, a1, *, tq, tk, out_scale):
    qi = pl.program_id(2)
    q = q_ref[...]
    lane = lax.broadcasted_iota(jnp.int32, q.shape, 1)
    zero = jnp.zeros_like(q)
    qa = jnp.where(lane < HEAD_DIM, q, zero)
    qb = jnp.where(lane >= HEAD_DIM, q, zero)
    q0 = qi * tq
    row = q0 + lax.broadcasted_iota(jnp.int32, (tq, tk), 0)
    col = lax.broadcasted_iota(jnp.int32, (tq, tk), 1)
    _init_state(m0, l0, a0, m1, l1, a1)

    def step(j, causal):
        start = pl.multiple_of(j * tk, tk)
        k = k_ref[pl.ds(start, tk), :]
        v = v_ref[pl.ds(start, tk), :]
        sa, sb = _nt_dot(qa, k), _nt_dot(qb, k)
        if causal:
            mask = (col + start) <= row
            sa, sb = jnp.where(mask, sa, NEG), jnp.where(mask, sb, NEG)
        _online_update(sa, v, m0, l0, a0)
        _online_update(sb, v, m1, l1, a1)

    n_full = (q0 + 1) // tk
    lax.fori_loop(0, n_full, lambda j, c: (step(j, False), c)[1], 0)
    lax.fori_loop(n_full, (q0 + tq + tk - 1) // tk, lambda j, c: (step(j, True), c)[1], 0)
    lam = par_ref[0:1, :]
    g = par_ref[1:2, :]
    o = _normalised(l0, a0) - lam * _normalised(l1, a1)
    o = o * lax.rsqrt(jnp.mean(o * o, axis=-1, keepdims=True) + RMS_EPS) * g
    o_ref[...] = (o * out_scale).astype(o_ref.dtype)


def diff_attention(q, k, v, n_batch, s, lam, subln_g, lambda_init, tq=256, tk=512):
    nq = s // tq
    par = jnp.concatenate([jnp.broadcast_to(lam.astype(F32), (1, LANES)),
                           subln_g.astype(F32).reshape(1, LANES),
                           jnp.zeros((6, LANES), F32)], 0)
    kern = functools.partial(_diff_attn_kernel, tq=tq, tk=tk, out_scale=1.0 - lambda_init)
    st = [pltpu.VMEM((tq, LANES), F32)] * 6
    return pl.pallas_call(
        kern,
        out_shape=jax.ShapeDtypeStruct((n_batch * s, DIFF_HEADS * LANES), F32),
        grid=(n_batch, DIFF_HEADS, nq),
        in_specs=[pl.BlockSpec((tq, LANES), lambda bi, h, i: (bi * nq + i, h)),
                  pl.BlockSpec((s, LANES), lambda bi, h, i: (bi, h)),
                  pl.BlockSpec((s, LANES), lambda bi, h, i: (bi, h)),
                  pl.BlockSpec((8, LANES), lambda bi, h, i: (0, 0))],
        out_specs=pl.BlockSpec((tq, LANES), lambda bi, h, i: (bi * nq + i, h)),
        scratch_shapes=st,
        compiler_params=pltpu.CompilerParams(
            dimension_semantics=("parallel", "parallel", "arbitrary"), vmem_limit_bytes=VMEM_LIMIT),
        name="diff_attention",
    )(q, k, v, par)


def _rank_select(score, n_candidates, top_n):
    lane = lax.broadcasted_iota(jnp.int32, score.shape, 1)
    rank = jnp.zeros(score.shape, F32)
    for c in range(n_candidates):
        col = score[:, c:c + 1]
        ahead = (col > score) | ((col == score) & (c < lane))
        rank = rank + jnp.where(ahead, 1.0, 0.0)
    return rank < top_n


def _rank_select_rows(score, n_candidates, top_n):
    idx = lax.broadcasted_iota(jnp.int32, score.shape, 0)
    rank = jnp.zeros(score.shape, F32)
    for c in range(n_candidates):
        cand = score[c:c + 1]
        ahead = (cand > score) | ((cand == score) & (c < idx))
        rank = rank + jnp.where(ahead, 1.0, 0.0)
    return rank < top_n


def _rows_to_lanes(x_t):
    n, q = x_t.shape
    if n < LANES:
        x_t = jnp.concatenate([x_t, jnp.zeros((LANES - n, q), F32)], axis=0)
    return jnp.concatenate([x_t[:, c * LANES:(c + 1) * LANES].T for c in range(q // LANES)], axis=0)


def _stack_heads(q_ref, tq):
    first = lax.broadcasted_iota(jnp.int32, (tq, LANES), 1) < HEAD_DIM
    tiles = []
    for t in range(NSA_HEADS // 2):
        pair = q_ref[:, t * LANES:(t + 1) * LANES].astype(F32)
        tiles.append(jnp.where(first, pair, 0.0))
        tiles.append(jnp.where(first, pltpu.roll(pair, HEAD_DIM, 1), 0.0))
    return jnp.concatenate(tiles, axis=0).astype(BF16)


def _nsa_attn_kernel(qn_ref, qr_ref, cc_ref, ss_ref, ww_ref, gate_ref, ov_ref, e_ref, o_ref, m, l, acc,
                     *, tq, tk, tkw, n_cmp, n_slc):
    qi = pl.program_id(1)
    q0 = qi * tq
    nh = NSA_HEADS
    r = nh * tq
    qn = _stack_heads(qn_ref, tq)
    qr = _stack_heads(qr_ref, tq)

    cc = cc_ref[...]
    n_pad = cc.shape[0]
    qpos = q0 + lax.broadcasted_iota(jnp.int32, (tq, n_pad), 0)
    qpos = jnp.concatenate([qpos] * nh, axis=0)
    n_idx = lax.broadcasted_iota(jnp.int32, (r, n_pad), 1)
    ok = (n_idx * CMP_STRIDE + CMP_LEN - 1 <= qpos) & (n_idx < n_cmp)
    sc = jnp.where(ok, _nt_dot(qn, cc), NEG)
    pc = jnp.where(ok, jnp.exp(sc - jnp.max(sc, axis=-1, keepdims=True)), 0.0)
    pc = pc / jnp.maximum(jnp.sum(pc, axis=-1, keepdims=True), 1e-30)
    p_hi = pc.astype(BF16)
    p_lo = (pc - p_hi.astype(F32)).astype(BF16)
    o_cmp = jnp.dot(p_hi, cc, preferred_element_type=F32)
    ovt = ov_ref[...]
    imp = _nt_dot(ovt, p_hi) + _nt_dot(ovt, p_lo)
    imp = imp[:, 0:tq] + imp[:, tq:2 * tq] + imp[:, 2 * tq:3 * tq] + imp[:, 3 * tq:4 * tq]
    imp = imp[0:-(-n_slc // 8) * 8]
    blk = lax.broadcasted_iota(jnp.int32, imp.shape, 0)
    cur = (q0 + lax.broadcasted_iota(jnp.int32, imp.shape, 1)) // SLC_LEN
    valid = blk <= cur
    forced = (blk == 0) | (blk == cur) | (blk == cur - 1)
    score = jnp.where(forced, jnp.inf, jnp.where(valid, imp, -jnp.inf))
    chosen = jnp.where(valid & _rank_select_rows(score, n_slc, min(SLC_TOPN, n_slc)), 1.0, 0.0)
    chosen = _rows_to_lanes(chosen).astype(BF16)
    chosen = jnp.concatenate([chosen] * nh, axis=0)

    def sweep(kv_ref, tile, lo, hi, mask_fn):
        _init_state(m, l, acc)

        def body(j, c):
            start = pl.multiple_of(j * tile, tile)
            kv = kv_ref[pl.ds(start, tile), :]
            _online_update(jnp.where(mask_fn(j, start), _nt_dot(qr, kv), NEG), kv, m, l, acc)
            return c

        lax.fori_loop(lo, hi, body, 0)
        return _normalised(l, acc)

    row = jnp.concatenate([q0 + lax.broadcasted_iota(jnp.int32, (tq, tk), 0)] * nh, axis=0)
    col = lax.broadcasted_iota(jnp.int32, (r, tk), 1)
    o_slc = sweep(ss_ref, tk, 0, (q0 + tq + tk - 1) // tk,
                  lambda j, start: ((col + start) <= row)
                  & (jnp.dot(chosen, e_ref[j], preferred_element_type=F32) > 0.5))
    row_w = jnp.concatenate([q0 + lax.broadcasted_iota(jnp.int32, (tq, tkw), 0)] * nh, axis=0)
    col_w = lax.broadcasted_iota(jnp.int32, (r, tkw), 1)
    o_win = sweep(ww_ref, tkw, jnp.maximum(q0 - WINDOW + 1, 0) // tkw, (q0 + tq + tkw - 1) // tkw,
                  lambda j, start: ((col_w + start) <= row_w) & ((col_w + start) > row_w - WINDOW))

    g = jax.nn.sigmoid(gate_ref[...])
    first = lax.broadcasted_iota(jnp.int32, (tq, LANES), 1) < HEAD_DIM
    mixed = []
    for h in range(nh):
        rows = slice(h * tq, (h + 1) * tq)
        mixed.append(g[:, 3 * h:3 * h + 1] * o_cmp[rows] + g[:, 3 * h + 1:3 * h + 2] * o_slc[rows]
                     + g[:, 3 * h + 2:3 * h + 3] * o_win[rows])
    for t in range(nh // 2):
        o_ref[:, t * LANES:(t + 1) * LANES] = jnp.where(first, pltpu.roll(mixed[2 * t], HEAD_DIM, 1), mixed[2 * t + 1])


def _block_expander(n_chunks, n_rows, tk, block_len):
    key = jnp.arange(n_chunks)[:, None, None] * tk + jnp.arange(tk)[None, None, :]
    return (key // block_len == jnp.arange(n_rows)[None, :, None]).astype(BF16)


def nsa_attention(qn, qr, cc, ss, ww, gate, n_batch, s, tq=128, tk=512, tkw=256):
    nq = s // tq
    n_cmp = cc.shape[1]
    n_pad = -(-n_cmp // LANES) * LANES
    cc = jnp.pad(cc, ((0, 0), (0, n_pad - n_cmp), (0, 0)))
    n_slc = s // SLC_LEN
    assert n_slc <= LANES
    cmp_start = jnp.arange(n_pad) * CMP_STRIDE
    slc_start = jnp.arange(LANES) * SLC_LEN
    ov = ((cmp_start[None, :] <= slc_start[:, None] + SLC_LEN - 1)
          & (cmp_start[None, :] + CMP_LEN - 1 >= slc_start[:, None])).astype(BF16)
    expander = _block_expander(s // tk, LANES, tk, SLC_LEN)
    r = NSA_HEADS * tq
    row = lambda bi, i: (bi * nq + i, 0)
    return pl.pallas_call(
        functools.partial(_nsa_attn_kernel, tq=tq, tk=tk, tkw=tkw, n_cmp=n_cmp, n_slc=n_slc),
        out_shape=jax.ShapeDtypeStruct((n_batch * s, NSA_HEADS * HEAD_DIM), F32),
        grid=(n_batch, nq),
        in_specs=[pl.BlockSpec((tq, NSA_HEADS * HEAD_DIM), row),
                  pl.BlockSpec((tq, NSA_HEADS * HEAD_DIM), row),
                  pl.BlockSpec((None, n_pad, LANES), lambda bi, i: (bi, 0, 0)),
                  pl.BlockSpec((s, LANES), lambda bi, i: (bi, 0)),
                  pl.BlockSpec((s, LANES), lambda bi, i: (bi, 0)),
                  pl.BlockSpec((tq, LANES), row),
                  pl.BlockSpec(ov.shape, lambda bi, i: (0, 0)),
                  pl.BlockSpec(expander.shape, lambda bi, i: (0, 0, 0))],
        out_specs=pl.BlockSpec((tq, NSA_HEADS * HEAD_DIM), row),
        scratch_shapes=[pltpu.VMEM((r, LANES), F32), pltpu.VMEM((r, LANES), F32), pltpu.VMEM((r, LANES), F32)],
        compiler_params=pltpu.CompilerParams(
            dimension_semantics=("parallel", "arbitrary"), vmem_limit_bytes=VMEM_LIMIT),
        name="nsa_attention",
    )(qn, qr, cc, ss, ww, gate, ov, expander)


def _moba_attn_kernel(q_ref, k_ref, v_ref, km_ref, e_ref, o_ref, m0, l0, a0, m1, l1, a1, *, tq, tk, n_blk):
    qi = pl.program_id(2)
    q = q_ref[...]
    lane = lax.broadcasted_iota(jnp.int32, q.shape, 1)
    zero = jnp.zeros_like(q)
    qa = jnp.where(lane < HEAD_DIM, q, zero)
    qb = jnp.where(lane >= HEAD_DIM, q, zero)
    q0 = qi * tq
    row = q0 + lax.broadcasted_iota(jnp.int32, (tq, tk), 0)
    col = lax.broadcasted_iota(jnp.int32, (tq, tk), 1)

    km = km_ref[...]
    blk = lax.broadcasted_iota(jnp.int32, (km.shape[0], tq), 0)
    cur = (q0 + lax.broadcasted_iota(jnp.int32, (km.shape[0], tq), 1)) // MOBA_BLOCK
    earlier = blk < cur

    def block_mask(qh):
        gate = jnp.where(earlier, _nt_dot(km, qh), -jnp.inf)
        chosen = (earlier & _rank_select_rows(gate, n_blk, min(MOBA_TOPK, n_blk))) | (blk == cur)
        return _rows_to_lanes(jnp.where(chosen, 1.0, 0.0)).astype(BF16)

    bm0 = block_mask(qa)
    bm1 = block_mask(qb)
    _init_state(m0, l0, a0, m1, l1, a1)

    def body(j, carry):
        start = pl.multiple_of(j * tk, tk)
        k = k_ref[pl.ds(start, tk), :]
        v = v_ref[pl.ds(start, tk), :]
        causal = (col + start) <= row
        e = e_ref[j]
        mask0 = causal & (jnp.dot(bm0, e, preferred_element_type=F32) > 0.5)
        mask1 = causal & (jnp.dot(bm1, e, preferred_element_type=F32) > 0.5)
        _online_update(jnp.where(mask0, _nt_dot(qa, k), NEG), v, m0, l0, a0)
        _online_update(jnp.where(mask1, _nt_dot(qb, k), NEG), v, m1, l1, a1)
        return carry

    lax.fori_loop(0, (q0 + tq + tk - 1) // tk, body, 0)
    o_ref[...] = jnp.where(lane < HEAD_DIM, _normalised(l0, a0), _normalised(l1, a1)).astype(o_ref.dtype)


def moba_attention(q, k, v, k_mean, n_batch, s, tq=256, tk=512):
    nq = s // tq
    n_blk = k_mean.shape[1]
    n_pad = -(-n_blk // 16) * 16
    km = jnp.pad(k_mean, ((0, 0), (0, n_pad - n_blk), (0, 0))).astype(BF16)
    expander = _block_expander(s // tk, LANES, tk, MOBA_BLOCK)
    st = [pltpu.VMEM((tq, LANES), F32)] * 6
    return pl.pallas_call(
        functools.partial(_moba_attn_kernel, tq=tq, tk=tk, n_blk=n_blk),
        out_shape=jax.ShapeDtypeStruct((n_batch * s, MOBA_HEADS * HEAD_DIM), F32),
        grid=(n_batch, MOBA_HEADS // 2, nq),
        in_specs=[pl.BlockSpec((tq, LANES), lambda bi, h, i: (bi * nq + i, h)),
                  pl.BlockSpec((s, LANES), lambda bi, h, i: (bi, h)),
                  pl.BlockSpec((s, LANES), lambda bi, h, i: (bi, h)),
                  pl.BlockSpec((None, n_pad, LANES), lambda bi, h, i: (bi, 0, h)),
                  pl.BlockSpec(expander.shape, lambda bi, h, i: (0, 0, 0))],
        out_specs=pl.BlockSpec((tq, LANES), lambda bi, h, i: (bi * nq + i, h)),
        scratch_shapes=st,
        compiler_params=pltpu.CompilerParams(
            dimension_semantics=("parallel", "parallel", "arbitrary"), vmem_limit_bytes=VMEM_LIMIT),
        name="moba_attention",
    )(q, k, v, km, expander)


def _layer_norm(z, g, b):
    mu = jnp.mean(z, axis=-1, keepdims=True)
    zc = z - mu
    var = jnp.mean(zc * zc, axis=-1, keepdims=True)
    return zc * lax.rsqrt(var + LN_EPS) * g + b


def _split_bf16(a):
    hi = a.astype(BF16)
    return hi, (a - hi.astype(F32)).astype(BF16)


def _route_experts(xn, rwh_ref, rwl_ref, rb_ref):
    tm = xn.shape[0]
    group = N_EXPERTS // N_EXPERT_GROUPS
    xh, xl = _split_bf16(xn)
    rwh = rwh_ref[...]
    logits = _nt_dot(rwh, xh) + _nt_dot(rwh, xl) + _nt_dot(rwl_ref[...], xh)
    s = jax.nn.sigmoid(logits)
    sb = s + jnp.concatenate([rb_ref[...]] * (tm // LANES), axis=1)
    grp = sb.reshape(N_EXPERT_GROUPS, group, tm)
    member = lax.broadcasted_iota(jnp.int32, grp.shape, 1).astype(F32)
    m1 = jnp.max(grp, axis=1, keepdims=True)
    first = jnp.min(jnp.where(grp == m1, member, float(group)), axis=1, keepdims=True)
    m2 = jnp.max(jnp.where(member == first, -jnp.inf, grp), axis=1, keepdims=True)
    group_ok = _rank_select_rows(m1 + m2, N_EXPERT_GROUPS, TOPK_GROUPS)
    cand = jnp.where(group_ok, grp, -jnp.inf).reshape(N_EXPERTS, tm)
    w = jnp.where(_rank_select_rows(cand, N_EXPERTS, TOP_K), s, 0.0)
    w = w / jnp.sum(w, axis=0, keepdims=True) * ROUTED_SCALE
    gate = _rows_to_lanes(w)
    hi = gate.astype(BF16).astype(F32)
    lane = lax.broadcasted_iota(jnp.int32, gate.shape, 1)
    return jnp.where(lane < N_EXPERTS, hi, pltpu.roll(gate - hi, N_EXPERTS, 1)).astype(BF16)


def _merge_kernel(x_ref, od_ref, on_ref, om_ref, g0_ref, g1_ref, g2_ref,
                  wd_ref, wn_ref, wm_ref, wo_ref, ln_ref, rwh_ref, rwl_ref, rb_ref, o_ref, gate_ref):
    def branch(o_r, w_r, g_r):
        y = jnp.dot(o_r[...].astype(BF16), w_r[...], preferred_element_type=F32)
        return jax.nn.sigmoid(g_r[...]) * y

    merged = branch(od_ref, wd_ref, g0_ref) + branch(on_ref, wn_ref, g1_ref) + branch(om_ref, wm_ref, g2_ref)
    y = jnp.dot(merged.astype(BF16), wo_ref[...], preferred_element_type=F32)
    z = DEEPNORM_ALPHA * x_ref[...] + y
    xn = _layer_norm(z, ln_ref[0:1, :], ln_ref[1:2, :])
    o_ref[...] = xn
    gate_ref[...] = _route_experts(xn, rwh_ref, rwl_ref, rb_ref)


def merge_project_norm(x, o_diff, o_nsa, o_moba, merge_gate, w_d, w_n, w_m, w_o, ln_g, ln_b,
                       router_w, router_b, tm=256):
    assert 2 * N_EXPERTS == LANES
    t = x.shape[0]
    ln = jnp.concatenate([ln_g.reshape(1, -1), ln_b.reshape(1, -1), jnp.zeros((6, D_MODEL), F32)], 0)
    rwh, rwl = _split_bf16(router_w.astype(F32).T)
    rb = jnp.broadcast_to(router_b.astype(F32)[:, None], (N_EXPERTS, LANES))
    row = lambda i: (i, 0)
    full = lambda i: (0, 0)
    return pl.pallas_call(
        _merge_kernel,
        out_shape=[jax.ShapeDtypeStruct((t, D_MODEL), F32), jax.ShapeDtypeStruct((t, LANES), BF16)],
        grid=(t // tm,),
        in_specs=[pl.BlockSpec((tm, D_MODEL), row),
                  pl.BlockSpec((tm, o_diff.shape[1]), row),
                  pl.BlockSpec((tm, o_nsa.shape[1]), row),
                  pl.BlockSpec((tm, o_moba.shape[1]), row),
                  pl.BlockSpec((tm, D_MODEL), lambda i: (i, 0)),
                  pl.BlockSpec((tm, D_MODEL), lambda i: (i, 1)),
                  pl.BlockSpec((tm, D_MODEL), lambda i: (i, 2)),
                  pl.BlockSpec(w_d.shape, full), pl.BlockSpec(w_n.shape, full),
                  pl.BlockSpec(w_m.shape, full), pl.BlockSpec(w_o.shape, full),
                  pl.BlockSpec((8, D_MODEL), full),
                  pl.BlockSpec(rwh.shape, full), pl.BlockSpec(rwl.shape, full), pl.BlockSpec(rb.shape, full)],
        out_specs=[pl.BlockSpec((tm, D_MODEL), row), pl.BlockSpec((tm, LANES), row)],
        compiler_params=pltpu.CompilerParams(
            dimension_semantics=("parallel",), vmem_limit_bytes=VMEM_LIMIT),
        name="merge_project_norm",
    )(x, o_diff, o_nsa, o_moba, merge_gate, merge_gate, merge_gate, w_d, w_n, w_m, w_o, ln, rwh, rwl, rb)


def _moe_kernel(x_ref, gate_ref, pick_ref, wg_ref, wu_ref, wd_ref, ln_ref, o_ref, acc_ref):
    f = pl.program_id(1)

    @pl.when(f == 0)
    def _():
        acc_ref[...] = jnp.zeros(acc_ref.shape, F32)

    n_e = wg_ref.shape[0]
    wg = jnp.concatenate([wg_ref[j] for j in range(n_e)], axis=1)
    wu = jnp.concatenate([wu_ref[j] for j in range(n_e)], axis=1)
    xb = x_ref[...].astype(BF16)
    hg = jnp.dot(xb, wg, preferred_element_type=F32)
    hu = jnp.dot(xb, wu, preferred_element_type=F32)
    w = jnp.dot(gate_ref[...], pick_ref[...], preferred_element_type=F32)
    lane = lax.broadcasted_iota(jnp.int32, w.shape, 1)
    w = jnp.where((f * n_e + lane >= N_EXPERTS) & (lane < n_e), 1.0, w)
    h = jax.nn.silu(hg) * hu
    h = jnp.concatenate([h[:, j * D_EXPERT:(j + 1) * D_EXPERT] * w[:, j:j + 1] for j in range(n_e)], axis=1)
    acc_ref[...] += jnp.dot(h.astype(BF16), wd_ref[...], preferred_element_type=F32)

    @pl.when(f == pl.num_programs(1) - 1)
    def _():
        z = DEEPNORM_ALPHA * x_ref[...] + acc_ref[...]
        o_ref[...] = _layer_norm(z, ln_ref[0:1, :], ln_ref[1:2, :])


def moe_norm(x, gate, wg, wu, wd, ln_g, ln_b, tm=768, experts_per_step=6):
    t = x.shape[0]
    f_tot = wd.shape[0]
    tf = experts_per_step * D_EXPERT
    n_col = gate.shape[1]
    n_steps = wg.shape[0] // experts_per_step
    assert wg.shape[0] % experts_per_step == 0 and t % tm == 0
    expert = jnp.arange(n_steps)[:, None, None] * experts_per_step + jnp.arange(LANES)[None, None, :]
    src = jnp.arange(n_col)[None, :, None] % N_EXPERTS
    pick = ((src == expert) & (jnp.arange(LANES)[None, None, :] < experts_per_step)).astype(BF16)
    ln = jnp.concatenate([ln_g.reshape(1, -1), ln_b.reshape(1, -1), jnp.zeros((6, D_MODEL), F32)], 0)
    return pl.pallas_call(
        _moe_kernel,
        out_shape=jax.ShapeDtypeStruct((t, D_MODEL), F32),
        grid=(t // tm, f_tot // tf),
        in_specs=[pl.BlockSpec((tm, D_MODEL), lambda i, f: (i, 0)),
                  pl.BlockSpec((tm, n_col), lambda i, f: (i, 0)),
                  pl.BlockSpec((None, n_col, LANES), lambda i, f: (f, 0, 0)),
                  pl.BlockSpec((experts_per_step, D_MODEL, D_EXPERT), lambda i, f: (f, 0, 0)),
                  pl.BlockSpec((experts_per_step, D_MODEL, D_EXPERT), lambda i, f: (f, 0, 0)),
                  pl.BlockSpec((tf, D_MODEL), lambda i, f: (f, 0)),
                  pl.BlockSpec((8, D_MODEL), lambda i, f: (0, 0))],
        out_specs=pl.BlockSpec((tm, D_MODEL), lambda i, f: (i, 0)),
        scratch_shapes=[pltpu.VMEM((tm, D_MODEL), F32)],
        compiler_params=pltpu.CompilerParams(
            dimension_semantics=("parallel", "arbitrary"), vmem_limit_bytes=VMEM_LIMIT),
        name="moe_norm",
    )(x, gate, pick, wg, wu, wd, ln)


TOK_PAD = 8
NEW_PAD = 16


def _softmax_two(s, sn):
    m = jnp.maximum(jnp.max(s, axis=-1, keepdims=True), jnp.max(sn, axis=-1, keepdims=True))
    p = jnp.exp(s - m)
    pn = jnp.exp(sn - m)
    inv = 1.0 / (jnp.sum(p, axis=-1, keepdims=True) + jnp.sum(pn, axis=-1, keepdims=True))
    return p * inv, pn * inv


def _new_row_mask(rows, n_new):
    t = lax.broadcasted_iota(jnp.int32, (rows, NEW_PAD), 0) & (TOK_PAD - 1)
    i = lax.broadcasted_iota(jnp.int32, (rows, NEW_PAD), 1)
    return (i <= t) & (i < n_new)


def _dec_diff_kernel(pt_ref, q_ref, new_ref, par_ref, *rest, n_pages, n_new, out_scale):
    pages = rest[:n_pages]
    o_ref, s_ref = rest[n_pages:]
    nqk = 2 * DIFF_HEADS * HEAD_DIM
    half = DIFF_HEADS * TOK_PAD
    q = q_ref[...]
    for j in range(n_pages):
        s_ref[:, j * PAGE_SIZE:(j + 1) * PAGE_SIZE] = _nt_dot(q, pages[j][:, 0:nqk].astype(BF16))
    sn = _nt_dot(q, new_ref[:, 0:nqk].astype(BF16))
    sn = jnp.where(_new_row_mask(2 * half, n_new), sn, NEG)
    p, pn = _softmax_two(s_ref[...], sn)
    lam = par_ref[0:1, 0:1]
    a = (p[0:half] - lam * p[half:2 * half]).astype(BF16)
    an = (pn[0:half] - lam * pn[half:2 * half]).astype(BF16)
    o = jnp.dot(an, new_ref[:, nqk:].astype(BF16), preferred_element_type=F32)
    for j in range(n_pages):
        o = o + jnp.dot(a[:, j * PAGE_SIZE:(j + 1) * PAGE_SIZE], pages[j][:, nqk:].astype(BF16),
                        preferred_element_type=F32)
    g = par_ref[1:2, :]
    for h in range(DIFF_HEADS):
        oh = o[h * TOK_PAD:(h + 1) * TOK_PAD, h * LANES:(h + 1) * LANES]
        oh = oh * lax.rsqrt(jnp.mean(oh * oh, axis=-1, keepdims=True) + RMS_EPS) * g
        o_ref[:, h * LANES:(h + 1) * LANES] = oh * out_scale


def _page_specs(layer, n_pages, width):
    return [pl.BlockSpec((None, None, PAGE_SIZE, width),
                         lambda b, pt, j=j: (layer, pt[b * n_pages + j], 0, 0)) for j in range(n_pages)]


def _per_seq(shape):
    return pl.BlockSpec((None,) + shape, lambda b, pt: (b,) + (0,) * len(shape))


def _shared(shape):
    return pl.BlockSpec(shape, lambda b, pt: (0,) * len(shape))


def _pad_rows(a, n):
    return jnp.pad(a, ((0, 0), (0, n - a.shape[1])) + ((0, 0),) * (a.ndim - 2))


def decode_diff_attention(layer, page_table, cache, da_q, rows_new, lam, subln_g, lambda_init):
    b, n_new, _ = da_q.shape
    n_pages = page_table.shape[1]
    q = da_q.astype(F32).reshape(b, n_new, DIFF_HEADS, 2, HEAD_DIM)
    q = _pad_rows(jnp.transpose(q, (0, 3, 2, 1, 4)).reshape(b * 2 * DIFF_HEADS, n_new, HEAD_DIM), TOK_PAD)
    q = q.reshape(b, 2, DIFF_HEADS, TOK_PAD, 1, HEAD_DIM)
    head = 2 * jnp.arange(DIFF_HEADS)[None, :] + jnp.arange(2)[:, None]
    place = (head[:, :, None] == jnp.arange(2 * DIFF_HEADS)).astype(F32)
    qbd = (q * place[None, :, :, None, :, None]).reshape(b, 2 * DIFF_HEADS * TOK_PAD, 2 * DIFF_HEADS * HEAD_DIM)
    par = jnp.concatenate([jnp.broadcast_to(lam.astype(F32), (1, LANES)), subln_g.astype(F32).reshape(1, LANES),
                           jnp.zeros((6, LANES), F32)], 0)
    rows = 2 * DIFF_HEADS * TOK_PAD
    width = cache.shape[-1]
    out = pl.pallas_call(
        functools.partial(_dec_diff_kernel, n_pages=n_pages, n_new=n_new, out_scale=1.0 - lambda_init),
        out_shape=jax.ShapeDtypeStruct((b, TOK_PAD, DIFF_HEADS * LANES), F32),
        grid_spec=pltpu.PrefetchScalarGridSpec(
            num_scalar_prefetch=1, grid=(b,),
            in_specs=[_per_seq((rows, 2 * DIFF_HEADS * HEAD_DIM)), _per_seq((NEW_PAD, width)), _shared((8, LANES))]
            + _page_specs(layer, n_pages, width),
            out_specs=_per_seq((TOK_PAD, DIFF_HEADS * LANES)),
            scratch_shapes=[pltpu.VMEM((rows, n_pages * PAGE_SIZE), F32)]),
        compiler_params=pltpu.CompilerParams(dimension_semantics=("arbitrary",), vmem_limit_bytes=VMEM_LIMIT),
        name="decode_diff_attention",
    )(page_table.reshape(-1), qbd.astype(BF16), _pad_rows(rows_new, NEW_PAD), par, *([cache] * n_pages))
    return out[:, :n_new]


def _dec_moba_kernel(pt_ref, q_ref, new_ref, *rest, n_pages, n_new, past_len):
    pages = rest[:n_pages]
    o_ref, s_ref, km_ref = rest[n_pages:]
    w = MOBA_HEADS * HEAD_DIM
    rows = MOBA_HEADS * TOK_PAD
    ppb = MOBA_BLOCK // PAGE_SIZE
    n_blk = n_pages // ppb
    q = q_ref[...]
    km_ref[...] = jnp.zeros(km_ref.shape, F32)
    for n in range(n_blk):
        tot = jnp.sum(pages[ppb * n][:, 0:w], axis=0, keepdims=True)
        for j in range(ppb * n + 1, ppb * (n + 1)):
            tot = tot + jnp.sum(pages[j][:, 0:w], axis=0, keepdims=True)
        km_ref[n:n + 1, :] = tot / MOBA_BLOCK
    gate = _nt_dot(q, km_ref[...].astype(BF16))
    lane = lax.broadcasted_iota(jnp.int32, gate.shape, 1)
    t = lax.broadcasted_iota(jnp.int32, gate.shape, 0) & (TOK_PAD - 1)
    earlier = (lane < (past_len + t) // MOBA_BLOCK) & (lane < n_blk)
    gate = jnp.where(earlier, gate, -jnp.inf)
    chosen = jnp.where(earlier & _rank_select(gate, n_blk, MOBA_TOPK), 1.0, 0.0)
    for j in range(n_pages):
        sj = _nt_dot(q, pages[j][:, 0:w].astype(BF16))
        n = j // ppb
        s_ref[:, j * PAGE_SIZE:(j + 1) * PAGE_SIZE] = jnp.where(chosen[:, n:n + 1] > 0.5, sj, NEG)
    sn = jnp.where(_new_row_mask(rows, n_new), _nt_dot(q, new_ref[:, 0:w].astype(BF16)), NEG)
    p, pn = _softmax_two(s_ref[...], sn)
    p = p.astype(BF16)
    o = jnp.dot(pn.astype(BF16), new_ref[:, w:].astype(BF16), preferred_element_type=F32)
    for j in range(n_pages):
        o = o + jnp.dot(p[:, j * PAGE_SIZE:(j + 1) * PAGE_SIZE], pages[j][:, w:].astype(BF16),
                        preferred_element_type=F32)
    for h in range(MOBA_HEADS):
        o_ref[:, h * HEAD_DIM:(h + 1) * HEAD_DIM] = o[h * TOK_PAD:(h + 1) * TOK_PAD, h * HEAD_DIM:(h + 1) * HEAD_DIM]


def _head_tiles(a, n_heads):
    b, n_new, _ = a.shape
    a = a.astype(F32).reshape(b, n_new, n_heads, HEAD_DIM)
    return _pad_rows(jnp.transpose(a, (0, 2, 1, 3)).reshape(b * n_heads, n_new, HEAD_DIM), TOK_PAD).reshape(
        b, n_heads, TOK_PAD, HEAD_DIM)


def decode_moba_attention(layer, page_table, cache, m_q, rows_new):
    b, n_new, _ = m_q.shape
    n_pages = page_table.shape[1]
    past_len = n_pages * PAGE_SIZE
    assert past_len % MOBA_BLOCK == 0 and n_new <= TOK_PAD and n_pages * PAGE_SIZE // MOBA_BLOCK <= NEW_PAD
    q = _head_tiles(m_q, MOBA_HEADS)
    place = jnp.eye(MOBA_HEADS, dtype=F32)
    q = (q[:, :, :, None, :] * place[None, :, None, :, None]).reshape(b, MOBA_HEADS * TOK_PAD, MOBA_HEADS * HEAD_DIM)
    rows = MOBA_HEADS * TOK_PAD
    width = cache.shape[-1]
    out = pl.pallas_call(
        functools.partial(_dec_moba_kernel, n_pages=n_pages, n_new=n_new, past_len=past_len),
        out_shape=jax.ShapeDtypeStruct((b, TOK_PAD, MOBA_HEADS * HEAD_DIM), F32),
        grid_spec=pltpu.PrefetchScalarGridSpec(
            num_scalar_prefetch=1, grid=(b,),
            in_specs=[_per_seq((rows, MOBA_HEADS * HEAD_DIM)), _per_seq((NEW_PAD, width))]
            + _page_specs(layer, n_pages, width),
            out_specs=_per_seq((TOK_PAD, MOBA_HEADS * HEAD_DIM)),
            scratch_shapes=[pltpu.VMEM((rows, past_len), F32), pltpu.VMEM((NEW_PAD, MOBA_HEADS * HEAD_DIM), F32)]),
        compiler_params=pltpu.CompilerParams(dimension_semantics=("arbitrary",), vmem_limit_bytes=VMEM_LIMIT),
        name="decode_moba_attention",
    )(page_table.reshape(-1), q.astype(BF16), _pad_rows(rows_new, NEW_PAD), *([cache] * n_pages))
    return out[:, :n_new]


def _dec_nsa_kernel(pt_ref, qn_ref, qr_ref, gate_ref, new_ref, wnew_ref, win_ref,
                    wc_ref, cb_ref, w2_ref, ov_ref, ex_ref, *rest, n_pages, n_new, past_len, win_pos0):
    pages = rest[:n_pages]
    o_ref, cmp_ref, slc_ref = rest[n_pages:]
    rows = NSA_HEADS * TOK_PAD
    n_chunk = past_len // CMP_STRIDE
    n_cmp = (past_len + n_new - CMP_LEN) // CMP_STRIDE + 1
    n_slc = -(-(past_len + n_new) // SLC_LEN)
    hid = cb_ref.shape[1] // 2
    for j in range(n_pages):
        cmp_ref[j * PAGE_SIZE:(j + 1) * PAGE_SIZE, :] = pages[j][:, 0:2 * HEAD_DIM]
        slc_ref[j * PAGE_SIZE:(j + 1) * PAGE_SIZE, :] = pages[j][:, 2 * HEAD_DIM:4 * HEAD_DIM]

    y = jnp.zeros((n_chunk, 4 * hid), F32)
    for r in range(CMP_STRIDE):
        xr = cmp_ref[pl.ds(r, n_chunk, stride=CMP_STRIDE), :].astype(BF16)
        y = y + jnp.dot(xr, wc_ref[r], preferred_element_type=F32)
    hk = y[:, 0:hid] + pltpu.roll(y[:, hid:2 * hid], n_chunk - 1, 0) + cb_ref[0:1, 0:hid]
    hv = y[:, 2 * hid:3 * hid] + pltpu.roll(y[:, 3 * hid:4 * hid], n_chunk - 1, 0) + cb_ref[0:1, hid:2 * hid]
    cc = jnp.dot(jax.nn.gelu(jnp.concatenate([hk, hv], axis=1)).astype(BF16), w2_ref[...],
                 preferred_element_type=F32).astype(BF16)

    t = lax.broadcasted_iota(jnp.int32, (rows, n_chunk), 0) & (TOK_PAD - 1)
    n_idx = lax.broadcasted_iota(jnp.int32, (rows, n_chunk), 1)
    ok = (n_idx * CMP_STRIDE + CMP_LEN - 1 <= past_len + t) & (n_idx < n_cmp)
    sc = jnp.where(ok, _nt_dot(qn_ref[...], cc), NEG)
    pc = jnp.exp(sc - jnp.max(sc, axis=-1, keepdims=True))
    pc = pc / jnp.sum(pc, axis=-1, keepdims=True)
    p_hi = pc.astype(BF16)
    p_lo = (pc - p_hi.astype(F32)).astype(BF16)
    o_cmp = jnp.dot(p_hi, cc, preferred_element_type=F32)[:, HEAD_DIM:]
    ov = ov_ref[...]
    imp = jnp.dot(p_hi, ov, preferred_element_type=F32) + jnp.dot(p_lo, ov, preferred_element_type=F32)
    imp = imp[0:TOK_PAD] + imp[TOK_PAD:2 * TOK_PAD] + imp[2 * TOK_PAD:3 * TOK_PAD] + imp[3 * TOK_PAD:4 * TOK_PAD]
    blk = lax.broadcasted_iota(jnp.int32, imp.shape, 1)
    cur = (past_len + lax.broadcasted_iota(jnp.int32, imp.shape, 0)) // SLC_LEN
    valid = blk <= cur
    forced = (blk == 0) | (blk == cur) | (blk == cur - 1)
    score = jnp.where(forced, jnp.inf, jnp.where(valid, imp, -jnp.inf))
    chosen = jnp.where(valid & _rank_select(score, n_slc, min(SLC_TOPN, n_slc)), 1.0, 0.0).astype(BF16)
    key_ok = jnp.dot(chosen, ex_ref[...], preferred_element_type=F32)
    key_ok = jnp.concatenate([key_ok] * NSA_HEADS, axis=0)
    new_mask = _new_row_mask(rows, n_new)

    qr = qr_ref[...]
    kv = slc_ref[...].astype(BF16)
    kv_new = new_ref[:, 2 * HEAD_DIM:4 * HEAD_DIM].astype(BF16)
    p, pn = _softmax_two(jnp.where(key_ok > 0.5, _nt_dot(qr, kv), NEG), jnp.where(new_mask, _nt_dot(qr, kv_new), NEG))
    o_slc = (jnp.dot(p.astype(BF16), kv, preferred_element_type=F32)
             + jnp.dot(pn.astype(BF16), kv_new, preferred_element_type=F32))[:, HEAD_DIM:]

    kv = win_ref[...].astype(BF16)
    kv_new = wnew_ref[...].astype(BF16)
    n_win = kv.shape[0]
    wpos = win_pos0 + lax.broadcasted_iota(jnp.int32, (rows, n_win), 1)
    qpos = past_len + (lax.broadcasted_iota(jnp.int32, (rows, n_win), 0) & (TOK_PAD - 1))
    ok = (wpos <= qpos) & (wpos > qpos - WINDOW)
    p, pn = _softmax_two(jnp.where(ok, _nt_dot(qr, kv), NEG), jnp.where(new_mask, _nt_dot(qr, kv_new), NEG))
    o_win = (jnp.dot(p.astype(BF16), kv, preferred_element_type=F32)
             + jnp.dot(pn.astype(BF16), kv_new, preferred_element_type=F32))[:, HEAD_DIM:]

    g = jax.nn.sigmoid(gate_ref[...])
    o = g[:, 0:1] * o_cmp + g[:, 1:2] * o_slc + g[:, 2:3] * o_win
    for h in range(NSA_HEADS):
        o_ref[:, h * HEAD_DIM:(h + 1) * HEAD_DIM] = o[h * TOK_PAD:(h + 1) * TOK_PAD]


def decode_nsa_attention(layer, page_table, cache, win_state, n_q, nq_r, n_gate, rows_new, rows_win,
                         cmp_pos, cmp_w1, cmp_w2):
    b, n_new, _ = n_q.shape
    n_pages = page_table.shape[1]
    past_len = n_pages * PAGE_SIZE
    n_win = win_state.shape[2]
    assert past_len % SLC_LEN == 0 and n_new < CMP_STRIDE and past_len >= CMP_LEN and n_new <= TOK_PAD
    n_chunk = past_len // CMP_STRIDE
    assert n_chunk == LANES, "compressed tokens are laid out on one lane tile"
    hid = cmp_w1.shape[-1]
    lane_pad = lambda a: jnp.pad(a, ((0, 0),) * (a.ndim - 1) + ((0, LANES - a.shape[-1]),))
    qn = lane_pad(_head_tiles(n_q, NSA_HEADS)).reshape(b, -1, LANES).astype(BF16)
    qr = lane_pad(_head_tiles(nq_r, NSA_HEADS)).reshape(b, -1, LANES).astype(BF16)
    g = jnp.transpose(n_gate[..., :N_GATE].reshape(b, n_new, NSA_HEADS, 3), (0, 2, 1, 3))
    g = lane_pad(_pad_rows(g.reshape(b * NSA_HEADS, n_new, 3), TOK_PAD)).reshape(b, -1, LANES)
    w1 = cmp_w1.reshape(2, 2, CMP_STRIDE, HEAD_DIM, hid)
    zero = jnp.zeros((CMP_STRIDE, HEAD_DIM, 2 * hid), F32)
    top = jnp.concatenate([w1[0, 0], w1[0, 1], zero], axis=-1)
    bot = jnp.concatenate([zero, w1[1, 0], w1[1, 1]], axis=-1)
    wc = jnp.concatenate([top, bot], axis=1).astype(BF16)
    bias = jnp.concatenate([cmp_pos[0].reshape(1, -1) @ cmp_w1[0], cmp_pos[1].reshape(1, -1) @ cmp_w1[1]], -1)
    cb = jnp.concatenate([bias, jnp.zeros((7, 2 * hid), F32)], 0)
    zw = jnp.zeros((hid, HEAD_DIM), F32)
    w2 = jnp.concatenate([jnp.concatenate([cmp_w2[0], zw], 1), jnp.concatenate([zw, cmp_w2[1]], 1)], 0).astype(BF16)
    cmp_start = jnp.arange(n_chunk) * CMP_STRIDE
    slc_start = jnp.arange(LANES) * SLC_LEN
    ov = ((cmp_start[:, None] <= slc_start[None, :] + SLC_LEN - 1)
          & (cmp_start[:, None] + CMP_LEN - 1 >= slc_start[None, :])).astype(BF16)
    ex = (jnp.arange(past_len)[None, :] // SLC_LEN == jnp.arange(LANES)[:, None]).astype(BF16)
    rows = NSA_HEADS * TOK_PAD
    width = cache.shape[-1]
    out = pl.pallas_call(
        functools.partial(_dec_nsa_kernel, n_pages=n_pages, n_new=n_new, past_len=past_len,
                          win_pos0=past_len - n_win),
        out_shape=jax.ShapeDtypeStruct((b, TOK_PAD, NSA_HEADS * HEAD_DIM), F32),
        grid_spec=pltpu.PrefetchScalarGridSpec(
            num_scalar_prefetch=1, grid=(b,),
            in_specs=[_per_seq((rows, LANES)), _per_seq((rows, LANES)), _per_seq((rows, LANES)),
                      _per_seq((NEW_PAD, width)), _per_seq((NEW_PAD, 2 * HEAD_DIM)),
                      pl.BlockSpec((None, None, n_win, 2 * HEAD_DIM), lambda bi, pt: (layer, bi, 0, 0)),
                      _shared(wc.shape), _shared(cb.shape), _shared(w2.shape), _shared(ov.shape), _shared(ex.shape)]
            + _page_specs(layer, n_pages, width),
            out_specs=_per_seq((TOK_PAD, NSA_HEADS * HEAD_DIM)),
            scratch_shapes=[pltpu.VMEM((past_len, 2 * HEAD_DIM), F32), pltpu.VMEM((past_len, 2 * HEAD_DIM), F32)]),
        compiler_params=pltpu.CompilerParams(dimension_semantics=("arbitrary",), vmem_limit_bytes=VMEM_LIMIT),
        name="decode_nsa_attention",
    )(page_table.reshape(-1), qn, qr, g, _pad_rows(rows_new, NEW_PAD), _pad_rows(rows_win, NEW_PAD), win_state,
      wc, cb, w2, ov, ex, *([cache] * n_pages))
    return out[:, :n_new]


def _compress(kv, pos_emb, w1, w2):
    b, l, _ = kv.shape
    n_chunk = l // CMP_STRIDE
    n_cmp = (l - CMP_LEN) // CMP_STRIDE + 1
    x = kv[:, :n_chunk * CMP_STRIDE].reshape(b * n_chunk, CMP_STRIDE * HEAD_DIM)
    half = CMP_STRIDE * HEAD_DIM
    w_cat = jnp.concatenate([w1[:half], w1[half:]], axis=1).astype(BF16)
    rows = x.shape[0]
    tm = 512 if rows % 512 == 0 else rows
    y = matmul(x, w_cat, tm, w_cat.shape[1]).reshape(b, n_chunk, 2, w1.shape[1])
    bias = pos_emb.reshape(1, -1) @ w1
    hid = y[:, :n_cmp, 0] + y[:, 1:n_cmp + 1, 1] + bias
    return jax.nn.gelu(hid) @ w2


def kernel(x_prompt, x_sample, cache_diff, cache_nsa, cache_moba, state_nsa_win, page_table, w_in, diff_lambda, diff_subln, nsa_cmp_pos, nsa_cmp_w1, nsa_cmp_w2, w_br_diff, w_br_nsa, w_br_moba, w_out, ln1_g, ln1_b, ln2_g, ln2_b, router_w, router_b, exp_w_gate, exp_w_up, exp_w_down, sh_w_gate, sh_w_up, sh_w_down):
    bp, sp, _ = x_prompt.shape
    bs, ss, _ = x_sample.shape
    tp, ts = bp * sp, bs * ss
    n_pages = page_table.shape[1]
    past_len = n_pages * PAGE_SIZE
    pos_p = jnp.arange(sp, dtype=jnp.int32)
    pos_s = past_len + jnp.arange(ss, dtype=jnp.int32)
    pos = jnp.concatenate([jnp.tile(pos_p, bp), jnp.tile(pos_s, bs)])
    inv = ROPE_THETA ** (-jnp.arange(HALF, dtype=F32) / HALF)
    ang = pos.astype(F32)[:, None] * inv[None, :]
    cos, sin = jnp.cos(ang), jnp.sin(ang)

    x = jnp.concatenate([x_prompt.reshape(tp, D_MODEL), x_sample.reshape(ts, D_MODEL)], 0)
    st_p = [[], [], [], []]
    st_s = [[], [], [], []]
    for l in range(DEPTH):
        lambda_init = 0.8 - 0.6 * math.exp(-0.3 * l)
        pj = project_inputs(x, repack_input_weight(w_in[l]), cos, sin)
        rows_diff, rows_nsa, rows_moba, rows_win = pj["rows_diff"], pj["rows_nsa"], pj["rows_moba"], pj["rows_win"]

        lp_ = diff_lambda[l].astype(F32)
        lam = jnp.exp(jnp.sum(lp_[0] * lp_[1])) - jnp.exp(jnp.sum(lp_[2] * lp_[3])) + lambda_init

        def pr(a):
            return a[:tp].reshape(bp, sp, a.shape[-1])

        o_diff_p = diff_attention(pj["q_diff"], pj["k_diff"], pj["v_diff"], bp, sp, lam, diff_subln[l], lambda_init)
        ck_c = _compress(pr(rows_nsa[:, 0:HEAD_DIM]), nsa_cmp_pos[l, 0], nsa_cmp_w1[l, 0], nsa_cmp_w2[l, 0])
        cv_c = _compress(pr(rows_nsa[:, HEAD_DIM:2 * HEAD_DIM]), nsa_cmp_pos[l, 1], nsa_cmp_w1[l, 1],
                         nsa_cmp_w2[l, 1])
        o_nsa_p = nsa_attention(pj["q_nsa"], pj["q_nsa_rot"], jnp.concatenate([ck_c, cv_c], -1).astype(BF16),
                                pj["kv_slc"], pj["kv_win"], pj["nsa_gate"], bp, sp)
        k_mean = jnp.mean(pr(rows_moba[:, 0:MOBA_HEADS * HEAD_DIM]).reshape(bp, sp // MOBA_BLOCK, MOBA_BLOCK, -1),
                          axis=2)
        o_moba_p = moba_attention(pj["q_moba"], pj["k_moba"], pj["v_moba"], k_mean, bp, sp)

        def sm(a):
            return a[tp:].reshape(bs, ss, a.shape[-1])

        o_diff_s = decode_diff_attention(l, page_table, cache_diff, sm(pj["q_diff"]), sm(rows_diff),
                                         lam, diff_subln[l], lambda_init)
        o_nsa_s = decode_nsa_attention(l, page_table, cache_nsa, state_nsa_win, sm(pj["q_nsa"]),
                                       sm(pj["q_nsa_rot"]), sm(pj["nsa_gate"]), sm(rows_nsa), sm(rows_win),
                                       nsa_cmp_pos[l], nsa_cmp_w1[l], nsa_cmp_w2[l])
        o_moba_s = decode_moba_attention(l, page_table, cache_moba, sm(pj["q_moba"]), sm(rows_moba))

        def both(a_p, a_s):
            return jnp.concatenate([a_p.reshape(tp, -1), a_s.reshape(ts, -1)], 0)

        x, expert_gate = merge_project_norm(
            x, both(o_diff_p, o_diff_s), both(o_nsa_p, o_nsa_s), both(o_moba_p, o_moba_s),
            pj["merge_gate"], w_br_diff[l].astype(BF16), w_br_nsa[l].astype(BF16),
            w_br_moba[l].astype(BF16), w_out[l].astype(BF16), ln1_g[l], ln1_b[l], router_w[l], router_b[l])

        def with_shared(w_exp, w_sh):
            w_sh = jnp.transpose(w_sh.reshape(D_MODEL, D_SHARED // D_EXPERT, D_EXPERT), (1, 0, 2))
            return jnp.concatenate([w_exp, w_sh], 0).astype(BF16)

        wd = jnp.concatenate([exp_w_down[l].reshape(-1, D_MODEL), sh_w_down[l]], 0).astype(BF16)
        x = moe_norm(x, expert_gate, with_shared(exp_w_gate[l], sh_w_gate[l]), with_shared(exp_w_up[l], sh_w_up[l]),
                     wd, ln2_g[l], ln2_b[l])

        for lst, val in zip(st_p, (rows_diff, rows_nsa, rows_moba)):
            lst.append(pr(val))
        n_keep = min(WINDOW, sp)
        st_p[3].append(pr(rows_win)[:, sp - n_keep:])
        for lst, val in zip(st_s, (rows_diff, rows_nsa, rows_moba)):
            lst.append(sm(val))
        full_win = jnp.concatenate([state_nsa_win[l], sm(rows_win)], 1)
        n_keep = min(WINDOW, past_len + ss)
        st_s[3].append(full_win[:, full_win.shape[1] - n_keep:])

    outs_p = [jnp.stack(a, 0) for a in st_p]
    outs_s = [jnp.stack(a, 0) for a in st_s]
    return (x[:tp].reshape(bp, sp, D_MODEL), x[tp:].reshape(bs, ss, D_MODEL),
            outs_p[0], outs_s[0], outs_p[1], outs_s[1], outs_p[2], outs_s[2], outs_p[3], outs_s[3])
```

```python
import functools
import math

import jax
import jax.numpy as jnp
from jax import lax
from jax.experimental import pallas as pl
from jax.experimental.pallas import tpu as pltpu

F32 = jnp.float32
BF16 = jnp.bfloat16

D_MODEL = 1024
DEPTH = 2
PAGE_SIZE = 128
HEAD_DIM = 64
HALF = HEAD_DIM // 2
ATTN_SCALE = HEAD_DIM ** -0.5
ROPE_THETA = 10000.0
DIFF_HEADS = 4
NSA_HEADS = 4
CMP_LEN = 32
CMP_STRIDE = 16
SLC_LEN = 64
SLC_TOPN = 16
WINDOW = 512
MOBA_HEADS = 4
MOBA_BLOCK = 256
MOBA_TOPK = 3
N_EXPERTS = 64
N_EXPERT_GROUPS = 8
TOPK_GROUPS = 4
TOP_K = 6
D_EXPERT = 128
D_SHARED = 256
ROUTED_SCALE = 2.5
LN_EPS = 1e-5
RMS_EPS = 1e-5
DEEPNORM_ALPHA = (2 * DEPTH) ** 0.25

C_DAQ, C_DAK, C_DAV, C_NQ = 0, 512, 1024, 1536
C_CK, C_CV, C_SK, C_SV, C_WK, C_WV, C_NG = 1792, 1856, 1920, 1984, 2048, 2112, 2176
N_GATE = 3 * NSA_HEADS
C_MQ_SRC = C_NG + N_GATE
C_MQ, C_MK, C_MV, C_MG = 2304, 2560, 2816, 3072
N_IN_PAD = C_MG + 3 * D_MODEL

LANES = 128
NEG = -1e30
VMEM_LIMIT = 56 * 1024 * 1024
MOE_EXPERTS_PER_STEP = 11

def _nt_dot(a, b):
    return lax.dot_general(a, b, (((1,), (1,)), ((), ())), preferred_element_type=F32)


def _mm_kernel(x_ref, w_ref, o_ref):
    o_ref[...] = jnp.dot(x_ref[...].astype(BF16), w_ref[...],
                         preferred_element_type=F32).astype(o_ref.dtype)


def matmul(x, w, tm, tn, out_dtype=F32):
    m, k = x.shape
    n = w.shape[1]
    assert m % tm == 0 and n % tn == 0, (x.shape, w.shape, tm, tn)
    return pl.pallas_call(
        _mm_kernel,
        out_shape=jax.ShapeDtypeStruct((m, n), out_dtype),
        grid=(m // tm, n // tn),
        in_specs=[pl.BlockSpec((tm, k), lambda i, j: (i, 0)),
                  pl.BlockSpec((k, tn), lambda i, j: (0, j))],
        out_specs=pl.BlockSpec((tm, tn), lambda i, j: (i, j)),
        compiler_params=pltpu.CompilerParams(
            dimension_semantics=("parallel", "arbitrary"), vmem_limit_bytes=VMEM_LIMIT),
        name="matmul",
    )(x, w)


def _repack_kernel(lo_ref, hi_ref, o_ref, *, first_moved_tile, shift):
    j = pl.program_id(0)
    lane = lax.broadcasted_iota(jnp.int32, lo_ref.shape, 1)
    hi = hi_ref[...]
    kept = jnp.where((j < first_moved_tile - 1) | (lane < LANES - shift), hi, 0.0)
    moved = jnp.where(lane < shift, pltpu.roll(lo_ref[...], shift, 1), pltpu.roll(hi, shift, 1))
    o_ref[...] = jnp.where(j < first_moved_tile, kept, moved).astype(o_ref.dtype)


def repack_input_weight(w):
    d, n_src = w.shape
    shift = C_MQ - C_MQ_SRC
    last = (n_src - 1) // LANES
    assert C_MQ % LANES == 0 and 0 < shift < LANES and n_src + shift == N_IN_PAD
    return pl.pallas_call(
        functools.partial(_repack_kernel, first_moved_tile=C_MQ // LANES, shift=shift),
        out_shape=jax.ShapeDtypeStruct((d, N_IN_PAD), BF16),
        grid=(N_IN_PAD // LANES,),
        in_specs=[pl.BlockSpec((d, LANES), lambda j: (0, jnp.maximum(j - 1, 0))),
                  pl.BlockSpec((d, LANES), lambda j: (0, jnp.minimum(j, last)))],
        out_specs=pl.BlockSpec((d, LANES), lambda j: (0, j)),
        compiler_params=pltpu.CompilerParams(dimension_semantics=("parallel",), vmem_limit_bytes=VMEM_LIMIT),
        name="repack_input_weight",
    )(w, w)


def _project_kernel(x_ref, w_ref, cos_ref, sa_ref, sb_ref,
                    rd_ref, rn_ref, rm_ref, rw_ref, mg_ref, ng_ref,
                    qd_ref, kd_ref, vd_ref, nq_ref, nqr_ref, ss_ref, ww_ref, mq_ref, mk_ref, mv_ref):
    xb = x_ref[...].astype(BF16)
    cos, sa, sb = cos_ref[...], sa_ref[...], sb_ref[...]

    def seg(a, b):
        return jnp.dot(xb, w_ref[:, a:b], preferred_element_type=F32)

    def rope(y):
        tiles = []
        for t in range(y.shape[1] // LANES):
            yt = y[:, t * LANES:(t + 1) * LANES]
            tiles.append(yt * cos + pltpu.roll(yt, LANES - HALF, 1) * sa + pltpu.roll(yt, HALF, 1) * sb)
        return tiles[0] if len(tiles) == 1 else jnp.concatenate(tiles, axis=1)

    first = lax.broadcasted_iota(jnp.int32, (x_ref.shape[0], LANES), 1) < HEAD_DIM
    qd_ref[...] = (rope(seg(C_DAQ, C_DAK)) * ATTN_SCALE).astype(BF16)
    k = rope(seg(C_DAK, C_DAV))
    v = seg(C_DAV, C_NQ)
    rd_ref[:, 0:C_DAV - C_DAK] = k
    rd_ref[:, C_DAV - C_DAK:] = v
    kd_ref[...] = k.astype(BF16)
    vd_ref[...] = v.astype(BF16)
    nq = seg(C_NQ, C_CK)
    nq_ref[...] = (nq * ATTN_SCALE).astype(BF16)
    nqr_ref[...] = (rope(nq) * ATTN_SCALE).astype(BF16)
    y = seg(C_CK, C_NG)
    ss = jnp.where(first, rope(y[:, LANES:2 * LANES]), y[:, LANES:2 * LANES])
    ww = jnp.where(first, rope(y[:, 2 * LANES:3 * LANES]), y[:, 2 * LANES:3 * LANES])
    rn_ref[:, 0:LANES] = y[:, 0:LANES]
    rn_ref[:, LANES:] = ss
    rw_ref[...] = ww
    ss_ref[...] = ss.astype(BF16)
    ww_ref[...] = ww.astype(BF16)
    ng_ref[...] = seg(C_NG, C_MQ)
    mq_ref[...] = (rope(seg(C_MQ, C_MK)) * ATTN_SCALE).astype(BF16)
    k = rope(seg(C_MK, C_MV))
    v = seg(C_MV, C_MG)
    rm_ref[:, 0:C_MV - C_MK] = k
    rm_ref[:, C_MV - C_MK:] = v
    mk_ref[...] = k.astype(BF16)
    mv_ref[...] = v.astype(BF16)
    for t in range(3):
        mg_ref[:, t * D_MODEL:(t + 1) * D_MODEL] = seg(C_MG + t * D_MODEL, C_MG + (t + 1) * D_MODEL)


def project_inputs(x, w_pad, cos, sin, tm=256):
    t = x.shape[0]
    zero = jnp.zeros_like(sin)
    cos_t = jnp.tile(cos, (1, LANES // HALF))
    sa = jnp.concatenate([-sin, zero, -sin, zero], axis=1)
    sb = jnp.concatenate([zero, sin, zero, sin], axis=1)
    names_f32 = (("rows_diff", 1024), ("rows_nsa", 256), ("rows_moba", 512), ("rows_win", 128),
                 ("merge_gate", 3 * D_MODEL), ("nsa_gate", LANES))
    names_bf16 = (("q_diff", 512), ("k_diff", 512), ("v_diff", 512), ("q_nsa", 256), ("q_nsa_rot", 256),
                  ("kv_slc", 128), ("kv_win", 128), ("q_moba", 256), ("k_moba", 256), ("v_moba", 256))
    row = lambda i: (i, 0)
    outs = pl.pallas_call(
        _project_kernel,
        out_shape=[jax.ShapeDtypeStruct((t, n), F32) for _, n in names_f32]
        + [jax.ShapeDtypeStruct((t, n), BF16) for _, n in names_bf16],
        grid=(t // tm,),
        in_specs=[pl.BlockSpec((tm, D_MODEL), row),
                  pl.BlockSpec(w_pad.shape, lambda i: (0, 0), pipeline_mode=pl.Buffered(1)),
                  pl.BlockSpec((tm, LANES), row), pl.BlockSpec((tm, LANES), row), pl.BlockSpec((tm, LANES), row)],
        out_specs=[pl.BlockSpec((tm, n), row) for _, n in names_f32 + names_bf16],
        compiler_params=pltpu.CompilerParams(dimension_semantics=("parallel",), vmem_limit_bytes=VMEM_LIMIT),
        name="project_inputs",
    )(x, w_pad, cos_t, sa, sb)
    return dict(zip([n for n, _ in names_f32 + names_bf16], outs))


def _online_update(s, v, m_ref, l_ref, acc_ref):
    tk = s.shape[1]
    dv = acc_ref.shape[-1]
    m_prev = m_ref[...]
    m_next = jnp.maximum(m_prev, jnp.max(s, axis=-1, keepdims=True))
    alpha = jnp.exp(m_prev - m_next)
    p = jnp.exp(s - jnp.concatenate([m_next] * (tk // LANES), axis=1))
    l_ref[...] = alpha * l_ref[...] + jnp.sum(p, axis=-1, keepdims=True)
    m_ref[...] = m_next
    acc_ref[...] = acc_ref[...] * alpha[:, :dv] + jnp.dot(p.astype(BF16), v, preferred_element_type=F32)


def _init_state(*refs):
    for m_ref, l_ref, acc_ref in zip(refs[0::3], refs[1::3], refs[2::3]):
        m_ref[...] = jnp.full(m_ref.shape, NEG, F32)
        l_ref[...] = jnp.zeros(l_ref.shape, F32)
        acc_ref[...] = jnp.zeros(acc_ref.shape, F32)


def _normalised(l_ref, acc_ref):
    dv = acc_ref.shape[-1]
    return acc_ref[...] / jnp.maximum(l_ref[...], 1e-30)[:, :dv]


def _diff_attn_kernel(q_ref, k_ref, v_ref, par_ref, o_ref,
                      m0, l0, a0, m1, l1, a1, *, tq, tk, out_scale):
    qi = pl.program_id(2)
    q = q_ref[...]
    lane = lax.broadcasted_iota(jnp.int32, q.shape, 1)
    zero = jnp.zeros_like(q)
    qa = jnp.where(lane < HEAD_DIM, q, zero)
    qb = jnp.where(lane >= HEAD_DIM, q, zero)
    q0 = qi * tq
    row = q0 + lax.broadcasted_iota(jnp.int32, (tq, tk), 0)
    col = lax.broadcasted_iota(jnp.int32, (tq, tk), 1)
    _init_state(m0, l0, a0, m1, l1, a1)

    def step(j, causal):
        start = pl.multiple_of(j * tk, tk)
        k = k_ref[pl.ds(start, tk), :]
        v = v_ref[pl.ds(start, tk), :]
        sa, sb = _nt_dot(qa, k), _nt_dot(qb, k)
        if causal:
            mask = (col + start) <= row
            sa, sb = jnp.where(mask, sa, NEG), jnp.where(mask, sb, NEG)
        _online_update(sa, v, m0, l0, a0)
        _online_update(sb, v, m1, l1, a1)

    n_full = (q0 + 1) // tk
    lax.fori_loop(0, n_full, lambda j, c: (step(j, False), c)[1], 0)
    lax.fori_loop(n_full, (q0 + tq + tk - 1) // tk, lambda j, c: (step(j, True), c)[1], 0)
    lam = par_ref[0:1, :]
    g = par_ref[1:2, :]
    o = _normalised(l0, a0) - lam * _normalised(l1, a1)
    o = o * lax.rsqrt(jnp.mean(o * o, axis=-1, keepdims=True) + RMS_EPS) * g
    o_ref[...] = (o * out_scale).astype(o_ref.dtype)


def diff_attention(q, k, v, n_batch, s, lam, subln_g, lambda_init, tq=256, tk=512):
    nq = s // tq
    par = jnp.concatenate([jnp.broadcast_to(lam.astype(F32), (1, LANES)),
                           subln_g.astype(F32).reshape(1, LANES),
                           jnp.zeros((6, LANES), F32)], 0)
    kern = functools.partial(_diff_attn_kernel, tq=tq, tk=tk, out_scale=1.0 - lambda_init)
    st = [pltpu.VMEM((tq, LANES), F32)] * 6
    return pl.pallas_call(
        kern,
        out_shape=jax.ShapeDtypeStruct((n_batch * s, DIFF_HEADS * LANES), F32),
        grid=(n_batch, DIFF_HEADS, nq),
        in_specs=[pl.BlockSpec((tq, LANES), lambda bi, h, i: (bi * nq + i, h)),
                  pl.BlockSpec((s, LANES), lambda bi, h, i: (bi, h)),
                  pl.BlockSpec((s, LANES), lambda bi, h, i: (bi, h)),
                  pl.BlockSpec((8, LANES), lambda bi, h, i: (0, 0))],
        out_specs=pl.BlockSpec((tq, LANES), lambda bi, h, i: (bi * nq + i, h)),
        scratch_shapes=st,
        compiler_params=pltpu.CompilerParams(
            dimension_semantics=("parallel", "parallel", "arbitrary"), vmem_limit_bytes=VMEM_LIMIT),
        name="diff_attention",
    )(q, k, v, par)


def _rank_select(score, n_candidates, top_n):
    lane = lax.broadcasted_iota(jnp.int32, score.shape, 1)
    rank = jnp.zeros(score.shape, F32)
    for c in range(n_candidates):
        col = score[:, c:c + 1]
        ahead = (col > score) | ((col == score) & (c < lane))
        rank = rank + jnp.where(ahead, 1.0, 0.0)
    return rank < top_n


def _rank_select_rows(score, n_candidates, top_n):
    idx = lax.broadcasted_iota(jnp.int32, score.shape, 0)
    rank = jnp.zeros(score.shape, F32)
    for c in range(n_candidates):
        cand = score[c:c + 1]
        ahead = (cand > score) | ((cand == score) & (c < idx))
        rank = rank + jnp.where(ahead, 1.0, 0.0)
    return rank < top_n


def _rows_to_lanes(x_t):
    n, q = x_t.shape
    if n < LANES:
        x_t = jnp.concatenate([x_t, jnp.zeros((LANES - n, q), F32)], axis=0)
    return jnp.concatenate([x_t[:, c * LANES:(c + 1) * LANES].T for c in range(q // LANES)], axis=0)


def _stack_heads(q_ref, tq):
    first = lax.broadcasted_iota(jnp.int32, (tq, LANES), 1) < HEAD_DIM
    tiles = []
    for t in range(NSA_HEADS // 2):
        pair = q_ref[:, t * LANES:(t + 1) * LANES].astype(F32)
        tiles.append(jnp.where(first, pair, 0.0))
        tiles.append(jnp.where(first, pltpu.roll(pair, HEAD_DIM, 1), 0.0))
    return jnp.concatenate(tiles, axis=0).astype(BF16)


def _nsa_attn_kernel(qn_ref, qr_ref, cc_ref, ss_ref, ww_ref, gate_ref, ov_ref, e_ref, o_ref, m, l, acc,
                     *, tq, tk, tkw, n_cmp, n_slc):
    qi = pl.program_id(1)
    q0 = qi * tq
    nh = NSA_HEADS
    r = nh * tq
    qn = _stack_heads(qn_ref, tq)
    qr = _stack_heads(qr_ref, tq)

    cc = cc_ref[...]
    n_pad = cc.shape[0]
    qpos = q0 + lax.broadcasted_iota(jnp.int32, (tq, n_pad), 0)
    qpos = jnp.concatenate([qpos] * nh, axis=0)
    n_idx = lax.broadcasted_iota(jnp.int32, (r, n_pad), 1)
    ok = (n_idx * CMP_STRIDE + CMP_LEN - 1 <= qpos) & (n_idx < n_cmp)
    sc = jnp.where(ok, _nt_dot(qn, cc), NEG)
    pc = jnp.where(ok, jnp.exp(sc - jnp.max(sc, axis=-1, keepdims=True)), 0.0)
    pc = pc / jnp.maximum(jnp.sum(pc, axis=-1, keepdims=True), 1e-30)
    p_hi = pc.astype(BF16)
    p_lo = (pc - p_hi.astype(F32)).astype(BF16)
    o_cmp = jnp.dot(p_hi, cc, preferred_element_type=F32)
    ovt = ov_ref[...]
    imp = _nt_dot(ovt, p_hi) + _nt_dot(ovt, p_lo)
    imp = imp[:, 0:tq] + imp[:, tq:2 * tq] + imp[:, 2 * tq:3 * tq] + imp[:, 3 * tq:4 * tq]
    imp = imp[0:-(-n_slc // 8) * 8]
    blk = lax.broadcasted_iota(jnp.int32, imp.shape, 0)
    cur = (q0 + lax.broadcasted_iota(jnp.int32, imp.shape, 1)) // SLC_LEN
    valid = blk <= cur
    forced = (blk == 0) | (blk == cur) | (blk == cur - 1)
    score = jnp.where(forced, jnp.inf, jnp.where(valid, imp, -jnp.inf))
    chosen = jnp.where(valid & _rank_select_rows(score, n_slc, min(SLC_TOPN, n_slc)), 1.0, 0.0)
    chosen = _rows_to_lanes(chosen).astype(BF16)
    chosen = jnp.concatenate([chosen] * nh, axis=0)

    def sweep(kv_ref, tile, lo, hi, mask_fn):
        _init_state(m, l, acc)

        def body(j, c):
            start = pl.multiple_of(j * tile, tile)
            kv = kv_ref[pl.ds(start, tile), :]
            _online_update(jnp.where(mask_fn(j, start), _nt_dot(qr, kv), NEG), kv, m, l, acc)
            return c

        lax.fori_loop(lo, hi, body, 0)
        return _normalised(l, acc)

    row = jnp.concatenate([q0 + lax.broadcasted_iota(jnp.int32, (tq, tk), 0)] * nh, axis=0)
    col = lax.broadcasted_iota(jnp.int32, (r, tk), 1)
    o_slc = sweep(ss_ref, tk, 0, (q0 + tq + tk - 1) // tk,
                  lambda j, start: ((col + start) <= row)
                  & (jnp.dot(chosen, e_ref[j], preferred_element_type=F32) > 0.5))
    row_w = jnp.concatenate([q0 + lax.broadcasted_iota(jnp.int32, (tq, tkw), 0)] * nh, axis=0)
    col_w = lax.broadcasted_iota(jnp.int32, (r, tkw), 1)
    o_win = sweep(ww_ref, tkw, jnp.maximum(q0 - WINDOW + 1, 0) // tkw, (q0 + tq + tkw - 1) // tkw,
                  lambda j, start: ((col_w + start) <= row_w) & ((col_w + start) > row_w - WINDOW))

    g = jax.nn.sigmoid(gate_ref[...])
    first = lax.broadcasted_iota(jnp.int32, (tq, LANES), 1) < HEAD_DIM
    mixed = []
    for h in range(nh):
        rows = slice(h * tq, (h + 1) * tq)
        mixed.append(g[:, 3 * h:3 * h + 1] * o_cmp[rows] + g[:, 3 * h + 1:3 * h + 2] * o_slc[rows]
                     + g[:, 3 * h + 2:3 * h + 3] * o_win[rows])
    for t in range(nh // 2):
        o_ref[:, t * LANES:(t + 1) * LANES] = jnp.where(first, pltpu.roll(mixed[2 * t], HEAD_DIM, 1), mixed[2 * t + 1])


def _block_expander(n_chunks, n_rows, tk, block_len):
    key = jnp.arange(n_chunks)[:, None, None] * tk + jnp.arange(tk)[None, None, :]
    return (key // block_len == jnp.arange(n_rows)[None, :, None]).astype(BF16)


def nsa_attention(qn, qr, cc, ss, ww, gate, n_batch, s, tq=128, tk=512, tkw=256):
    nq = s // tq
    n_cmp = cc.shape[1]
    n_pad = -(-n_cmp // LANES) * LANES
    cc = jnp.pad(cc, ((0, 0), (0, n_pad - n_cmp), (0, 0)))
    n_slc = s // SLC_LEN
    assert n_slc <= LANES
    cmp_start = jnp.arange(n_pad) * CMP_STRIDE
    slc_start = jnp.arange(LANES) * SLC_LEN
    ov = ((cmp_start[None, :] <= slc_start[:, None] + SLC_LEN - 1)
          & (cmp_start[None, :] + CMP_LEN - 1 >= slc_start[:, None])).astype(BF16)
    expander = _block_expander(s // tk, LANES, tk, SLC_LEN)
    r = NSA_HEADS * tq
    row = lambda bi, i: (bi * nq + i, 0)
    return pl.pallas_call(
        functools.partial(_nsa_attn_kernel, tq=tq, tk=tk, tkw=tkw, n_cmp=n_cmp, n_slc=n_slc),
        out_shape=jax.ShapeDtypeStruct((n_batch * s, NSA_HEADS * HEAD_DIM), F32),
        grid=(n_batch, nq),
        in_specs=[pl.BlockSpec((tq, NSA_HEADS * HEAD_DIM), row),
                  pl.BlockSpec((tq, NSA_HEADS * HEAD_DIM), row),
                  pl.BlockSpec((None, n_pad, LANES), lambda bi, i: (bi, 0, 0)),
                  pl.BlockSpec((s, LANES), lambda bi, i: (bi, 0)),
                  pl.BlockSpec((s, LANES), lambda bi, i: (bi, 0)),
                  pl.BlockSpec((tq, LANES), row),
                  pl.BlockSpec(ov.shape, lambda bi, i: (0, 0)),
                  pl.BlockSpec(expander.shape, lambda bi, i: (0, 0, 0))],
        out_specs=pl.BlockSpec((tq, NSA_HEADS * HEAD_DIM), row),
        scratch_shapes=[pltpu.VMEM((r, LANES), F32), pltpu.VMEM((r, LANES), F32), pltpu.VMEM((r, LANES), F32)],
        compiler_params=pltpu.CompilerParams(
            dimension_semantics=("parallel", "arbitrary"), vmem_limit_bytes=VMEM_LIMIT),
        name="nsa_attention",
    )(qn, qr, cc, ss, ww, gate, ov, expander)


def _moba_attn_kernel(q_ref, k_ref, v_ref, km_ref, e_ref, o_ref, m0, l0, a0, m1, l1, a1, *, tq, tk, n_blk):
    qi = pl.program_id(2)
    q = q_ref[...]
    lane = lax.broadcasted_iota(jnp.int32, q.shape, 1)
    zero = jnp.zeros_like(q)
    qa = jnp.where(lane < HEAD_DIM, q, zero)
    qb = jnp.where(lane >= HEAD_DIM, q, zero)
    q0 = qi * tq
    row = q0 + lax.broadcasted_iota(jnp.int32, (tq, tk), 0)
    col = lax.broadcasted_iota(jnp.int32, (tq, tk), 1)

    km = km_ref[...]
    blk = lax.broadcasted_iota(jnp.int32, (km.shape[0], tq), 0)
    cur = (q0 + lax.broadcasted_iota(jnp.int32, (km.shape[0], tq), 1)) // MOBA_BLOCK
    earlier = blk < cur

    def block_mask(qh):
        gate = jnp.where(earlier, _nt_dot(km, qh), -jnp.inf)
        chosen = (earlier & _rank_select_rows(gate, n_blk, min(MOBA_TOPK, n_blk))) | (blk == cur)
        return _rows_to_lanes(jnp.where(chosen, 1.0, 0.0)).astype(BF16)

    bm0 = block_mask(qa)
    bm1 = block_mask(qb)
    _init_state(m0, l0, a0, m1, l1, a1)

    def body(j, carry):
        start = pl.multiple_of(j * tk, tk)
        k = k_ref[pl.ds(start, tk), :]
        v = v_ref[pl.ds(start, tk), :]
        causal = (col + start) <= row
        e = e_ref[j]
        mask0 = causal & (jnp.dot(bm0, e, preferred_element_type=F32) > 0.5)
        mask1 = causal & (jnp.dot(bm1, e, preferred_element_type=F32) > 0.5)
        _online_update(jnp.where(mask0, _nt_dot(qa, k), NEG), v, m0, l0, a0)
        _online_update(jnp.where(mask1, _nt_dot(qb, k), NEG), v, m1, l1, a1)
        return carry

    lax.fori_loop(0, (q0 + tq + tk - 1) // tk, body, 0)
    o_ref[...] = jnp.where(lane < HEAD_DIM, _normalised(l0, a0), _normalised(l1, a1)).astype(o_ref.dtype)


def moba_attention(q, k, v, k_mean, n_batch, s, tq=256, tk=512):
    nq = s // tq
    n_blk = k_mean.shape[1]
    n_pad = -(-n_blk // 16) * 16
    km = jnp.pad(k_mean, ((0, 0), (0, n_pad - n_blk), (0, 0))).astype(BF16)
    expander = _block_expander(s // tk, LANES, tk, MOBA_BLOCK)
    st = [pltpu.VMEM((tq, LANES), F32)] * 6
    return pl.pallas_call(
        functools.partial(_moba_attn_kernel, tq=tq, tk=tk, n_blk=n_blk),
        out_shape=jax.ShapeDtypeStruct((n_batch * s, MOBA_HEADS * HEAD_DIM), F32),
        grid=(n_batch, MOBA_HEADS // 2, nq),
        in_specs=[pl.BlockSpec((tq, LANES), lambda bi, h, i: (bi * nq + i, h)),
                  pl.BlockSpec((s, LANES), lambda bi, h, i: (bi, h)),
                  pl.BlockSpec((s, LANES), lambda bi, h, i: (bi, h)),
                  pl.BlockSpec((None, n_pad, LANES), lambda bi, h, i: (bi, 0, h)),
                  pl.BlockSpec(expander.shape, lambda bi, h, i: (0, 0, 0))],
        out_specs=pl.BlockSpec((tq, LANES), lambda bi, h, i: (bi * nq + i, h)),
        scratch_shapes=st,
        compiler_params=pltpu.CompilerParams(
            dimension_semantics=("parallel", "parallel", "arbitrary"), vmem_limit_bytes=VMEM_LIMIT),
        name="moba_attention",
    )(q, k, v, km, expander)


def _layer_norm(z, g, b):
    mu = jnp.mean(z, axis=-1, keepdims=True)
    zc = z - mu
    var = jnp.mean(zc * zc, axis=-1, keepdims=True)
    return zc * lax.rsqrt(var + LN_EPS) * g + b


def _split_bf16(a):
    hi = a.astype(BF16)
    return hi, (a - hi.astype(F32)).astype(BF16)


def _route_experts(xn, rwh_ref, rwl_ref, rb_ref):
    tm = xn.shape[0]
    group = N_EXPERTS // N_EXPERT_GROUPS
    xh, xl = _split_bf16(xn)
    rwh = rwh_ref[...]
    logits = _nt_dot(rwh, xh) + _nt_dot(rwh, xl) + _nt_dot(rwl_ref[...], xh)
    s = jax.nn.sigmoid(logits)
    sb = s + jnp.concatenate([rb_ref[...]] * (tm // LANES), axis=1)
    grp = sb.reshape(N_EXPERT_GROUPS, group, tm)
    member = lax.broadcasted_iota(jnp.int32, grp.shape, 1).astype(F32)
    m1 = jnp.max(grp, axis=1, keepdims=True)
    first = jnp.min(jnp.where(grp == m1, member, float(group)), axis=1, keepdims=True)
    m2 = jnp.max(jnp.where(member == first, -jnp.inf, grp), axis=1, keepdims=True)
    group_ok = _rank_select_rows(m1 + m2, N_EXPERT_GROUPS, TOPK_GROUPS)
    cand = jnp.where(group_ok, grp, -jnp.inf).reshape(N_EXPERTS, tm)
    w = jnp.where(_rank_select_rows(cand, N_EXPERTS, TOP_K), s, 0.0)
    w = w / jnp.sum(w, axis=0, keepdims=True) * ROUTED_SCALE
    gate = _rows_to_lanes(w)
    hi = gate.astype(BF16).astype(F32)
    lane = lax.broadcasted_iota(jnp.int32, gate.shape, 1)
    return jnp.where(lane < N_EXPERTS, hi, pltpu.roll(gate - hi, N_EXPERTS, 1)).astype(BF16)


def _merge_kernel(x_ref, odp_ref, ods_ref, onp_ref, ons_ref, omp_ref, oms_ref, g0_ref, g1_ref, g2_ref,
                  wd_ref, wn_ref, wm_ref, wo_ref, ln_ref, rwh_ref, rwl_ref, rb_ref, o_ref, gate_ref,
                  *, n_prompt_tiles):
    from_prompt = pl.program_id(0) < n_prompt_tiles

    def branch(p_r, s_r, w_r, g_r):
        o = jnp.where(from_prompt, p_r[...], s_r[...])
        y = jnp.dot(o.astype(BF16), w_r[...], preferred_element_type=F32)
        return jax.nn.sigmoid(g_r[...]) * y

    merged = (branch(odp_ref, ods_ref, wd_ref, g0_ref) + branch(onp_ref, ons_ref, wn_ref, g1_ref)
              + branch(omp_ref, oms_ref, wm_ref, g2_ref))
    y = jnp.dot(merged.astype(BF16), wo_ref[...], preferred_element_type=F32)
    z = DEEPNORM_ALPHA * x_ref[...] + y
    xn = _layer_norm(z, ln_ref[0:1, :], ln_ref[1:2, :])
    o_ref[...] = xn
    gate_ref[...] = _route_experts(xn, rwh_ref, rwl_ref, rb_ref)


def merge_project_norm(x, o_diff, o_nsa, o_moba, merge_gate, w_d, w_n, w_m, w_o, ln_g, ln_b,
                       router_w, router_b, tm=256):
    assert 2 * N_EXPERTS == LANES
    t = x.shape[0]
    tp, ts = o_diff[0].shape[0], o_diff[1].shape[0]
    assert tp % tm == 0 and ts % tm == 0 and tp + ts == t
    npt = tp // tm
    ln = jnp.concatenate([ln_g.reshape(1, -1), ln_b.reshape(1, -1), jnp.zeros((6, D_MODEL), F32)], 0)
    rwh, rwl = _split_bf16(router_w.astype(F32).T)
    rb = jnp.broadcast_to(router_b.astype(F32)[:, None], (N_EXPERTS, LANES))
    row = lambda i: (i, 0)
    full = lambda i: (0, 0)
    pair_specs, pair_args = [], []
    for o_p, o_s in (o_diff, o_nsa, o_moba):
        pair_specs += [pl.BlockSpec((tm, o_p.shape[1]), lambda i: (jnp.minimum(i, npt - 1), 0)),
                       pl.BlockSpec((tm, o_s.shape[1]), lambda i: (jnp.maximum(i - npt, 0), 0))]
        pair_args += [o_p, o_s]
    return pl.pallas_call(
        functools.partial(_merge_kernel, n_prompt_tiles=npt),
        out_shape=[jax.ShapeDtypeStruct((t, D_MODEL), F32), jax.ShapeDtypeStruct((t, LANES), BF16)],
        grid=(t // tm,),
        in_specs=[pl.BlockSpec((tm, D_MODEL), row)] + pair_specs + [
                  pl.BlockSpec((tm, D_MODEL), lambda i: (i, 0)),
                  pl.BlockSpec((tm, D_MODEL), lambda i: (i, 1)),
                  pl.BlockSpec((tm, D_MODEL), lambda i: (i, 2)),
                  pl.BlockSpec(w_d.shape, full), pl.BlockSpec(w_n.shape, full),
                  pl.BlockSpec(w_m.shape, full), pl.BlockSpec(w_o.shape, full),
                  pl.BlockSpec((8, D_MODEL), full),
                  pl.BlockSpec(rwh.shape, full), pl.BlockSpec(rwl.shape, full), pl.BlockSpec(rb.shape, full)],
        out_specs=[pl.BlockSpec((tm, D_MODEL), row), pl.BlockSpec((tm, LANES), row)],
        compiler_params=pltpu.CompilerParams(
            dimension_semantics=("parallel",), vmem_limit_bytes=VMEM_LIMIT),
        name="merge_project_norm",
    )(x, *pair_args, merge_gate, merge_gate, merge_gate, w_d, w_n, w_m, w_o, ln, rwh, rwl, rb)


def _moe_kernel(x_ref, gate_ref, pick_ref, wg_ref, wu_ref, wd_ref, ln_ref, o_ref, acc_ref):
    f = pl.program_id(1)

    @pl.when(f == 0)
    def _():
        acc_ref[...] = jnp.zeros(acc_ref.shape, F32)

    n_e = wg_ref.shape[0]
    wg = jnp.concatenate([wg_ref[j] for j in range(n_e)], axis=1)
    wu = jnp.concatenate([wu_ref[j] for j in range(n_e)], axis=1)
    xb = x_ref[...].astype(BF16)
    hg = jnp.dot(xb, wg, preferred_element_type=F32)
    hu = jnp.dot(xb, wu, preferred_element_type=F32)
    w = jnp.dot(gate_ref[...], pick_ref[...], preferred_element_type=F32)
    lane = lax.broadcasted_iota(jnp.int32, w.shape, 1)
    w = jnp.where((f * n_e + lane >= N_EXPERTS) & (lane < n_e), 1.0, w)
    h = jax.nn.silu(hg) * hu
    h = jnp.concatenate([h[:, j * D_EXPERT:(j + 1) * D_EXPERT] * w[:, j:j + 1] for j in range(n_e)], axis=1)
    acc_ref[...] += jnp.dot(h.astype(BF16), wd_ref[...], preferred_element_type=F32)

    @pl.when(f == pl.num_programs(1) - 1)
    def _():
        z = DEEPNORM_ALPHA * x_ref[...] + acc_ref[...]
        o_ref[...] = _layer_norm(z, ln_ref[0:1, :], ln_ref[1:2, :])


def moe_norm(x, gate, wg, wu, wd, ln_g, ln_b, tm=768, experts_per_step=MOE_EXPERTS_PER_STEP):
    t = x.shape[0]
    f_tot = wd.shape[0]
    tf = experts_per_step * D_EXPERT
    n_col = gate.shape[1]
    n_steps = wg.shape[0] // experts_per_step
    assert wg.shape[0] % experts_per_step == 0 and t % tm == 0
    expert = jnp.arange(n_steps)[:, None, None] * experts_per_step + jnp.arange(LANES)[None, None, :]
    src = jnp.arange(n_col)[None, :, None] % N_EXPERTS
    pick = ((src == expert) & (jnp.arange(LANES)[None, None, :] < experts_per_step)).astype(BF16)
    ln = jnp.concatenate([ln_g.reshape(1, -1), ln_b.reshape(1, -1), jnp.zeros((6, D_MODEL), F32)], 0)
    return pl.pallas_call(
        _moe_kernel,
        out_shape=jax.ShapeDtypeStruct((t, D_MODEL), F32),
        grid=(t // tm, f_tot // tf),
        in_specs=[pl.BlockSpec((tm, D_MODEL), lambda i, f: (i, 0)),
                  pl.BlockSpec((tm, n_col), lambda i, f: (i, 0)),
                  pl.BlockSpec((None, n_col, LANES), lambda i, f: (f, 0, 0)),
                  pl.BlockSpec((experts_per_step, D_MODEL, D_EXPERT), lambda i, f: (f, 0, 0)),
                  pl.BlockSpec((experts_per_step, D_MODEL, D_EXPERT), lambda i, f: (f, 0, 0)),
                  pl.BlockSpec((tf, D_MODEL), lambda i, f: (f, 0)),
                  pl.BlockSpec((8, D_MODEL), lambda i, f: (0, 0))],
        out_specs=pl.BlockSpec((tm, D_MODEL), lambda i, f: (i, 0)),
        scratch_shapes=[pltpu.VMEM((tm, D_MODEL), F32)],
        compiler_params=pltpu.CompilerParams(
            dimension_semantics=("parallel", "arbitrary"), vmem_limit_bytes=VMEM_LIMIT),
        name="moe_norm",
    )(x, gate, pick, wg, wu, wd, ln)


TOK_PAD = 8
NEW_PAD = 16


def _softmax_two(s, sn):
    m = jnp.maximum(jnp.max(s, axis=-1, keepdims=True), jnp.max(sn, axis=-1, keepdims=True))
    p = jnp.exp(s - m)
    pn = jnp.exp(sn - m)
    inv = 1.0 / (jnp.sum(p, axis=-1, keepdims=True) + jnp.sum(pn, axis=-1, keepdims=True))
    return p * inv, pn * inv


def _new_row_mask(rows, n_new):
    t = lax.broadcasted_iota(jnp.int32, (rows, NEW_PAD), 0) & (TOK_PAD - 1)
    i = lax.broadcasted_iota(jnp.int32, (rows, NEW_PAD), 1)
    return (i <= t) & (i < n_new)


def _dec_diff_kernel(pt_ref, q_ref, new_ref, par_ref, *rest, n_pages, n_new, out_scale):
    pages = rest[:n_pages]
    o_ref, s_ref = rest[n_pages:]
    nqk = 2 * DIFF_HEADS * HEAD_DIM
    half = DIFF_HEADS * TOK_PAD
    q = q_ref[...]
    for j in range(n_pages):
        s_ref[:, j * PAGE_SIZE:(j + 1) * PAGE_SIZE] = _nt_dot(q, pages[j][:, 0:nqk].astype(BF16))
    sn = _nt_dot(q, new_ref[:, 0:nqk].astype(BF16))
    sn = jnp.where(_new_row_mask(2 * half, n_new), sn, NEG)
    p, pn = _softmax_two(s_ref[...], sn)
    lam = par_ref[0:1, 0:1]
    a = (p[0:half] - lam * p[half:2 * half]).astype(BF16)
    an = (pn[0:half] - lam * pn[half:2 * half]).astype(BF16)
    o = jnp.dot(an, new_ref[:, nqk:].astype(BF16), preferred_element_type=F32)
    for j in range(n_pages):
        o = o + jnp.dot(a[:, j * PAGE_SIZE:(j + 1) * PAGE_SIZE], pages[j][:, nqk:].astype(BF16),
                        preferred_element_type=F32)
    g = par_ref[1:2, :]
    for h in range(DIFF_HEADS):
        oh = o[h * TOK_PAD:(h + 1) * TOK_PAD, h * LANES:(h + 1) * LANES]
        oh = oh * lax.rsqrt(jnp.mean(oh * oh, axis=-1, keepdims=True) + RMS_EPS) * g
        o_ref[:, h * LANES:(h + 1) * LANES] = oh * out_scale


def _page_specs(layer, n_pages, width):
    return [pl.BlockSpec((None, None, PAGE_SIZE, width),
                         lambda b, pt, j=j: (layer, pt[b * n_pages + j], 0, 0)) for j in range(n_pages)]


def _per_seq(shape):
    return pl.BlockSpec((None,) + shape, lambda b, pt: (b,) + (0,) * len(shape))


def _shared(shape):
    return pl.BlockSpec(shape, lambda b, pt: (0,) * len(shape))


def _pad_rows(a, n):
    return jnp.pad(a, ((0, 0), (0, n - a.shape[1])) + ((0, 0),) * (a.ndim - 2))


def decode_diff_attention(layer, page_table, cache, da_q, rows_new, lam, subln_g, lambda_init):
    b, n_new, _ = da_q.shape
    n_pages = page_table.shape[1]
    q = da_q.astype(F32).reshape(b, n_new, DIFF_HEADS, 2, HEAD_DIM)
    q = _pad_rows(jnp.transpose(q, (0, 3, 2, 1, 4)).reshape(b * 2 * DIFF_HEADS, n_new, HEAD_DIM), TOK_PAD)
    q = q.reshape(b, 2, DIFF_HEADS, TOK_PAD, 1, HEAD_DIM)
    head = 2 * jnp.arange(DIFF_HEADS)[None, :] + jnp.arange(2)[:, None]
    place = (head[:, :, None] == jnp.arange(2 * DIFF_HEADS)).astype(F32)
    qbd = (q * place[None, :, :, None, :, None]).reshape(b, 2 * DIFF_HEADS * TOK_PAD, 2 * DIFF_HEADS * HEAD_DIM)
    par = jnp.concatenate([jnp.broadcast_to(lam.astype(F32), (1, LANES)), subln_g.astype(F32).reshape(1, LANES),
                           jnp.zeros((6, LANES), F32)], 0)
    rows = 2 * DIFF_HEADS * TOK_PAD
    width = cache.shape[-1]
    return dict(
        body=functools.partial(_dec_diff_kernel, n_pages=n_pages, n_new=n_new, out_scale=1.0 - lambda_init),
        args=[qbd.astype(BF16), _pad_rows(rows_new, NEW_PAD), par] + [cache] * n_pages,
        in_specs=[_per_seq((rows, 2 * DIFF_HEADS * HEAD_DIM)), _per_seq((NEW_PAD, width)), _shared((8, LANES))]
        + _page_specs(layer, n_pages, width),
        out_shape=jax.ShapeDtypeStruct((b, TOK_PAD, DIFF_HEADS * LANES), F32),
        out_spec=_per_seq((TOK_PAD, DIFF_HEADS * LANES)),
        scratch=[pltpu.VMEM((rows, n_pages * PAGE_SIZE), F32)])


def _dec_moba_kernel(pt_ref, q_ref, new_ref, *rest, n_pages, n_new, past_len):
    pages = rest[:n_pages]
    o_ref, s_ref, km_ref = rest[n_pages:]
    w = MOBA_HEADS * HEAD_DIM
    rows = MOBA_HEADS * TOK_PAD
    ppb = MOBA_BLOCK // PAGE_SIZE
    n_blk = n_pages // ppb
    q = q_ref[...]
    km_ref[...] = jnp.zeros(km_ref.shape, F32)
    for n in range(n_blk):
        tot = jnp.sum(pages[ppb * n][:, 0:w], axis=0, keepdims=True)
        for j in range(ppb * n + 1, ppb * (n + 1)):
            tot = tot + jnp.sum(pages[j][:, 0:w], axis=0, keepdims=True)
        km_ref[n:n + 1, :] = tot / MOBA_BLOCK
    gate = _nt_dot(q, km_ref[...].astype(BF16))
    lane = lax.broadcasted_iota(jnp.int32, gate.shape, 1)
    t = lax.broadcasted_iota(jnp.int32, gate.shape, 0) & (TOK_PAD - 1)
    earlier = (lane < (past_len + t) // MOBA_BLOCK) & (lane < n_blk)
    gate = jnp.where(earlier, gate, -jnp.inf)
    chosen = jnp.where(earlier & _rank_select(gate, n_blk, MOBA_TOPK), 1.0, 0.0)
    for j in range(n_pages):
        sj = _nt_dot(q, pages[j][:, 0:w].astype(BF16))
        n = j // ppb
        s_ref[:, j * PAGE_SIZE:(j + 1) * PAGE_SIZE] = jnp.where(chosen[:, n:n + 1] > 0.5, sj, NEG)
    sn = jnp.where(_new_row_mask(rows, n_new), _nt_dot(q, new_ref[:, 0:w].astype(BF16)), NEG)
    p, pn = _softmax_two(s_ref[...], sn)
    p = p.astype(BF16)
    o = jnp.dot(pn.astype(BF16), new_ref[:, w:].astype(BF16), preferred_element_type=F32)
    for j in range(n_pages):
        o = o + jnp.dot(p[:, j * PAGE_SIZE:(j + 1) * PAGE_SIZE], pages[j][:, w:].astype(BF16),
                        preferred_element_type=F32)
    for h in range(MOBA_HEADS):
        o_ref[:, h * HEAD_DIM:(h + 1) * HEAD_DIM] = o[h * TOK_PAD:(h + 1) * TOK_PAD, h * HEAD_DIM:(h + 1) * HEAD_DIM]


def _head_tiles(a, n_heads):
    b, n_new, _ = a.shape
    a = a.astype(F32).reshape(b, n_new, n_heads, HEAD_DIM)
    return _pad_rows(jnp.transpose(a, (0, 2, 1, 3)).reshape(b * n_heads, n_new, HEAD_DIM), TOK_PAD).reshape(
        b, n_heads, TOK_PAD, HEAD_DIM)


def decode_moba_attention(layer, page_table, cache, m_q, rows_new):
    b, n_new, _ = m_q.shape
    n_pages = page_table.shape[1]
    past_len = n_pages * PAGE_SIZE
    assert past_len % MOBA_BLOCK == 0 and n_new <= TOK_PAD and n_pages * PAGE_SIZE // MOBA_BLOCK <= NEW_PAD
    q = _head_tiles(m_q, MOBA_HEADS)
    place = jnp.eye(MOBA_HEADS, dtype=F32)
    q = (q[:, :, :, None, :] * place[None, :, None, :, None]).reshape(b, MOBA_HEADS * TOK_PAD, MOBA_HEADS * HEAD_DIM)
    rows = MOBA_HEADS * TOK_PAD
    width = cache.shape[-1]
    return dict(
        body=functools.partial(_dec_moba_kernel, n_pages=n_pages, n_new=n_new, past_len=past_len),
        args=[q.astype(BF16), _pad_rows(rows_new, NEW_PAD)] + [cache] * n_pages,
        in_specs=[_per_seq((rows, MOBA_HEADS * HEAD_DIM)), _per_seq((NEW_PAD, width))]
        + _page_specs(layer, n_pages, width),
        out_shape=jax.ShapeDtypeStruct((b, TOK_PAD, MOBA_HEADS * HEAD_DIM), F32),
        out_spec=_per_seq((TOK_PAD, MOBA_HEADS * HEAD_DIM)),
        scratch=[pltpu.VMEM((rows, past_len), F32), pltpu.VMEM((NEW_PAD, MOBA_HEADS * HEAD_DIM), F32)])


def _dec_nsa_kernel(pt_ref, qn_ref, qr_ref, gate_ref, new_ref, wnew_ref, win_ref,
                    wc_ref, cb_ref, w2_ref, ov_ref, ex_ref, *rest, n_pages, n_new, past_len, win_pos0):
    pages = rest[:n_pages]
    o_ref, cmp_ref, slc_ref = rest[n_pages:]
    rows = NSA_HEADS * TOK_PAD
    n_chunk = past_len // CMP_STRIDE
    n_cmp = (past_len + n_new - CMP_LEN) // CMP_STRIDE + 1
    n_slc = -(-(past_len + n_new) // SLC_LEN)
    hid = cb_ref.shape[1] // 2
    for j in range(n_pages):
        cmp_ref[j * PAGE_SIZE:(j + 1) * PAGE_SIZE, :] = pages[j][:, 0:2 * HEAD_DIM]
        slc_ref[j * PAGE_SIZE:(j + 1) * PAGE_SIZE, :] = pages[j][:, 2 * HEAD_DIM:4 * HEAD_DIM]

    y = jnp.zeros((n_chunk, 4 * hid), F32)
    for r in range(CMP_STRIDE):
        xr = cmp_ref[pl.ds(r, n_chunk, stride=CMP_STRIDE), :].astype(BF16)
        y = y + jnp.dot(xr, wc_ref[r], preferred_element_type=F32)
    hk = y[:, 0:hid] + pltpu.roll(y[:, hid:2 * hid], n_chunk - 1, 0) + cb_ref[0:1, 0:hid]
    hv = y[:, 2 * hid:3 * hid] + pltpu.roll(y[:, 3 * hid:4 * hid], n_chunk - 1, 0) + cb_ref[0:1, hid:2 * hid]
    cc = jnp.dot(jax.nn.gelu(jnp.concatenate([hk, hv], axis=1)).astype(BF16), w2_ref[...],
                 preferred_element_type=F32).astype(BF16)

    t = lax.broadcasted_iota(jnp.int32, (rows, n_chunk), 0) & (TOK_PAD - 1)
    n_idx = lax.broadcasted_iota(jnp.int32, (rows, n_chunk), 1)
    ok = (n_idx * CMP_STRIDE + CMP_LEN - 1 <= past_len + t) & (n_idx < n_cmp)
    sc = jnp.where(ok, _nt_dot(qn_ref[...], cc), NEG)
    pc = jnp.exp(sc - jnp.max(sc, axis=-1, keepdims=True))
    pc = pc / jnp.sum(pc, axis=-1, keepdims=True)
    p_hi = pc.astype(BF16)
    p_lo = (pc - p_hi.astype(F32)).astype(BF16)
    o_cmp = jnp.dot(p_hi, cc, preferred_element_type=F32)[:, HEAD_DIM:]
    ov = ov_ref[...]
    imp = jnp.dot(p_hi, ov, preferred_element_type=F32) + jnp.dot(p_lo, ov, preferred_element_type=F32)
    imp = imp[0:TOK_PAD] + imp[TOK_PAD:2 * TOK_PAD] + imp[2 * TOK_PAD:3 * TOK_PAD] + imp[3 * TOK_PAD:4 * TOK_PAD]
    blk = lax.broadcasted_iota(jnp.int32, imp.shape, 1)
    cur = (past_len + lax.broadcasted_iota(jnp.int32, imp.shape, 0)) // SLC_LEN
    valid = blk <= cur
    forced = (blk == 0) | (blk == cur) | (blk == cur - 1)
    score = jnp.where(forced, jnp.inf, jnp.where(valid, imp, -jnp.inf))
    chosen = jnp.where(valid & _rank_select(score, n_slc, min(SLC_TOPN, n_slc)), 1.0, 0.0).astype(BF16)
    key_ok = jnp.dot(chosen, ex_ref[...], preferred_element_type=F32)
    key_ok = jnp.concatenate([key_ok] * NSA_HEADS, axis=0)
    new_mask = _new_row_mask(rows, n_new)

    qr = qr_ref[...]
    kv = slc_ref[...].astype(BF16)
    kv_new = new_ref[:, 2 * HEAD_DIM:4 * HEAD_DIM].astype(BF16)
    p, pn = _softmax_two(jnp.where(key_ok > 0.5, _nt_dot(qr, kv), NEG), jnp.where(new_mask, _nt_dot(qr, kv_new), NEG))
    o_slc = (jnp.dot(p.astype(BF16), kv, preferred_element_type=F32)
             + jnp.dot(pn.astype(BF16), kv_new, preferred_element_type=F32))[:, HEAD_DIM:]

    kv = win_ref[...].astype(BF16)
    kv_new = wnew_ref[...].astype(BF16)
    n_win = kv.shape[0]
    wpos = win_pos0 + lax.broadcasted_iota(jnp.int32, (rows, n_win), 1)
    qpos = past_len + (lax.broadcasted_iota(jnp.int32, (rows, n_win), 0) & (TOK_PAD - 1))
    ok = (wpos <= qpos) & (wpos > qpos - WINDOW)
    p, pn = _softmax_two(jnp.where(ok, _nt_dot(qr, kv), NEG), jnp.where(new_mask, _nt_dot(qr, kv_new), NEG))
    o_win = (jnp.dot(p.astype(BF16), kv, preferred_element_type=F32)
             + jnp.dot(pn.astype(BF16), kv_new, preferred_element_type=F32))[:, HEAD_DIM:]

    g = jax.nn.sigmoid(gate_ref[...])
    o = g[:, 0:1] * o_cmp + g[:, 1:2] * o_slc + g[:, 2:3] * o_win
    for h in range(NSA_HEADS):
        o_ref[:, h * HEAD_DIM:(h + 1) * HEAD_DIM] = o[h * TOK_PAD:(h + 1) * TOK_PAD]


def decode_nsa_attention(layer, page_table, cache, win_state, n_q, nq_r, n_gate, rows_new, rows_win,
                         cmp_pos, cmp_w1, cmp_w2):
    b, n_new, _ = n_q.shape
    n_pages = page_table.shape[1]
    past_len = n_pages * PAGE_SIZE
    n_win = win_state.shape[2]
    assert past_len % SLC_LEN == 0 and n_new < CMP_STRIDE and past_len >= CMP_LEN and n_new <= TOK_PAD
    n_chunk = past_len // CMP_STRIDE
    assert n_chunk == LANES, "compressed tokens are laid out on one lane tile"
    hid = cmp_w1.shape[-1]
    lane_pad = lambda a: jnp.pad(a, ((0, 0),) * (a.ndim - 1) + ((0, LANES - a.shape[-1]),))
    qn = lane_pad(_head_tiles(n_q, NSA_HEADS)).reshape(b, -1, LANES).astype(BF16)
    qr = lane_pad(_head_tiles(nq_r, NSA_HEADS)).reshape(b, -1, LANES).astype(BF16)
    g = jnp.transpose(n_gate[..., :N_GATE].reshape(b, n_new, NSA_HEADS, 3), (0, 2, 1, 3))
    g = lane_pad(_pad_rows(g.reshape(b * NSA_HEADS, n_new, 3), TOK_PAD)).reshape(b, -1, LANES)
    w1 = cmp_w1.reshape(2, 2, CMP_STRIDE, HEAD_DIM, hid)
    zero = jnp.zeros((CMP_STRIDE, HEAD_DIM, 2 * hid), F32)
    top = jnp.concatenate([w1[0, 0], w1[0, 1], zero], axis=-1)
    bot = jnp.concatenate([zero, w1[1, 0], w1[1, 1]], axis=-1)
    wc = jnp.concatenate([top, bot], axis=1).astype(BF16)
    bias = jnp.concatenate([cmp_pos[0].reshape(1, -1) @ cmp_w1[0], cmp_pos[1].reshape(1, -1) @ cmp_w1[1]], -1)
    cb = jnp.concatenate([bias, jnp.zeros((7, 2 * hid), F32)], 0)
    zw = jnp.zeros((hid, HEAD_DIM), F32)
    w2 = jnp.concatenate([jnp.concatenate([cmp_w2[0], zw], 1), jnp.concatenate([zw, cmp_w2[1]], 1)], 0).astype(BF16)
    cmp_start = jnp.arange(n_chunk) * CMP_STRIDE
    slc_start = jnp.arange(LANES) * SLC_LEN
    ov = ((cmp_start[:, None] <= slc_start[None, :] + SLC_LEN - 1)
          & (cmp_start[:, None] + CMP_LEN - 1 >= slc_start[None, :])).astype(BF16)
    ex = (jnp.arange(past_len)[None, :] // SLC_LEN == jnp.arange(LANES)[:, None]).astype(BF16)
    rows = NSA_HEADS * TOK_PAD
    width = cache.shape[-1]
    return dict(
        body=functools.partial(_dec_nsa_kernel, n_pages=n_pages, n_new=n_new, past_len=past_len,
                               win_pos0=past_len - n_win),
        args=[qn, qr, g, _pad_rows(rows_new, NEW_PAD), _pad_rows(rows_win, NEW_PAD), win_state,
              wc, cb, w2, ov, ex] + [cache] * n_pages,
        in_specs=[_per_seq((rows, LANES)), _per_seq((rows, LANES)), _per_seq((rows, LANES)),
                  _per_seq((NEW_PAD, width)), _per_seq((NEW_PAD, 2 * HEAD_DIM)),
                  pl.BlockSpec((None, None, n_win, 2 * HEAD_DIM), lambda bi, pt: (layer, bi, 0, 0)),
                  _shared(wc.shape), _shared(cb.shape), _shared(w2.shape), _shared(ov.shape), _shared(ex.shape)]
        + _page_specs(layer, n_pages, width),
        out_shape=jax.ShapeDtypeStruct((b, TOK_PAD, NSA_HEADS * HEAD_DIM), F32),
        out_spec=_per_seq((TOK_PAD, NSA_HEADS * HEAD_DIM)),
        scratch=[pltpu.VMEM((past_len, 2 * HEAD_DIM), F32), pltpu.VMEM((past_len, 2 * HEAD_DIM), F32)])


def _decode_kernel(pt_ref, *refs, parts):
    pos = 0
    ins = []
    for _, n_in, _ in parts:
        ins.append(refs[pos:pos + n_in])
        pos += n_in
    outs = refs[pos:pos + len(parts)]
    pos += len(parts)
    for (body, _, n_scratch), part_ins, o_ref in zip(parts, ins, outs):
        body(pt_ref, *part_ins, o_ref, *refs[pos:pos + n_scratch])
        pos += n_scratch


def decode_attention(page_table, parts, n_new):
    b = page_table.shape[0]
    outs = pl.pallas_call(
        functools.partial(_decode_kernel, parts=[(p["body"], len(p["args"]), len(p["scratch"])) for p in parts]),
        out_shape=[p["out_shape"] for p in parts],
        grid_spec=pltpu.PrefetchScalarGridSpec(
            num_scalar_prefetch=1, grid=(b,),
            in_specs=[s for p in parts for s in p["in_specs"]],
            out_specs=[p["out_spec"] for p in parts],
            scratch_shapes=[s for p in parts for s in p["scratch"]]),
        compiler_params=pltpu.CompilerParams(dimension_semantics=("arbitrary",), vmem_limit_bytes=VMEM_LIMIT),
        name="decode_attention",
    )(page_table.reshape(-1), *[a for p in parts for a in p["args"]])
    return [o[:, :n_new] for o in outs]


def _compress(kv, pos_emb, w1, w2):
    b, l, _ = kv.shape
    n_chunk = l // CMP_STRIDE
    n_cmp = (l - CMP_LEN) // CMP_STRIDE + 1
    x = kv[:, :n_chunk * CMP_STRIDE].reshape(b * n_chunk, CMP_STRIDE * HEAD_DIM)
    half = CMP_STRIDE * HEAD_DIM
    w_cat = jnp.concatenate([w1[:half], w1[half:]], axis=1).astype(BF16)
    rows = x.shape[0]
    tm = 512 if rows % 512 == 0 else rows
    y = matmul(x, w_cat, tm, w_cat.shape[1]).reshape(b, n_chunk, 2, w1.shape[1])
    bias = pos_emb.reshape(1, -1) @ w1
    hid = y[:, :n_cmp, 0] + y[:, 1:n_cmp + 1, 1] + bias
    return jax.nn.gelu(hid) @ w2


def kernel(x_prompt, x_sample, cache_diff, cache_nsa, cache_moba, state_nsa_win, page_table, w_in, diff_lambda, diff_subln, nsa_cmp_pos, nsa_cmp_w1, nsa_cmp_w2, w_br_diff, w_br_nsa, w_br_moba, w_out, ln1_g, ln1_b, ln2_g, ln2_b, router_w, router_b, exp_w_gate, exp_w_up, exp_w_down, sh_w_gate, sh_w_up, sh_w_down):
    bp, sp, _ = x_prompt.shape
    bs, ss, _ = x_sample.shape
    tp, ts = bp * sp, bs * ss
    n_pages = page_table.shape[1]
    past_len = n_pages * PAGE_SIZE
    pos_p = jnp.arange(sp, dtype=jnp.int32)
    pos_s = past_len + jnp.arange(ss, dtype=jnp.int32)
    pos = jnp.concatenate([jnp.tile(pos_p, bp), jnp.tile(pos_s, bs)])
    inv = ROPE_THETA ** (-jnp.arange(HALF, dtype=F32) / HALF)
    ang = pos.astype(F32)[:, None] * inv[None, :]
    cos, sin = jnp.cos(ang), jnp.sin(ang)

    x = jnp.concatenate([x_prompt.reshape(tp, D_MODEL), x_sample.reshape(ts, D_MODEL)], 0)
    st_p = [[], [], [], []]
    st_s = [[], [], [], []]
    for l in range(DEPTH):
        lambda_init = 0.8 - 0.6 * math.exp(-0.3 * l)
        pj = project_inputs(x, repack_input_weight(w_in[l]), cos, sin)
        rows_diff, rows_nsa, rows_moba, rows_win = pj["rows_diff"], pj["rows_nsa"], pj["rows_moba"], pj["rows_win"]

        lp_ = diff_lambda[l].astype(F32)
        lam = jnp.exp(jnp.sum(lp_[0] * lp_[1])) - jnp.exp(jnp.sum(lp_[2] * lp_[3])) + lambda_init

        def pr(a):
            return a[:tp].reshape(bp, sp, a.shape[-1])

        o_diff_p = diff_attention(pj["q_diff"], pj["k_diff"], pj["v_diff"], bp, sp, lam, diff_subln[l], lambda_init)
        ck_c = _compress(pr(rows_nsa[:, 0:HEAD_DIM]), nsa_cmp_pos[l, 0], nsa_cmp_w1[l, 0], nsa_cmp_w2[l, 0])
        cv_c = _compress(pr(rows_nsa[:, HEAD_DIM:2 * HEAD_DIM]), nsa_cmp_pos[l, 1], nsa_cmp_w1[l, 1],
                         nsa_cmp_w2[l, 1])
        o_nsa_p = nsa_attention(pj["q_nsa"], pj["q_nsa_rot"], jnp.concatenate([ck_c, cv_c], -1).astype(BF16),
                                pj["kv_slc"], pj["kv_win"], pj["nsa_gate"], bp, sp)
        k_mean = jnp.mean(pr(rows_moba[:, 0:MOBA_HEADS * HEAD_DIM]).reshape(bp, sp // MOBA_BLOCK, MOBA_BLOCK, -1),
                          axis=2)
        o_moba_p = moba_attention(pj["q_moba"], pj["k_moba"], pj["v_moba"], k_mean, bp, sp)

        def sm(a):
            return a[tp:].reshape(bs, ss, a.shape[-1])

        o_diff_s, o_nsa_s, o_moba_s = decode_attention(page_table, [
            decode_diff_attention(l, page_table, cache_diff, sm(pj["q_diff"]), sm(rows_diff),
                                  lam, diff_subln[l], lambda_init),
            decode_nsa_attention(l, page_table, cache_nsa, state_nsa_win, sm(pj["q_nsa"]),
                                 sm(pj["q_nsa_rot"]), sm(pj["nsa_gate"]), sm(rows_nsa), sm(rows_win),
                                 nsa_cmp_pos[l], nsa_cmp_w1[l], nsa_cmp_w2[l]),
            decode_moba_attention(l, page_table, cache_moba, sm(pj["q_moba"]), sm(rows_moba))], ss)

        x, expert_gate = merge_project_norm(
            x, (o_diff_p, o_diff_s.reshape(ts, -1)), (o_nsa_p, o_nsa_s.reshape(ts, -1)),
            (o_moba_p, o_moba_s.reshape(ts, -1)),
            pj["merge_gate"], w_br_diff[l].astype(BF16), w_br_nsa[l].astype(BF16),
            w_br_moba[l].astype(BF16), w_out[l].astype(BF16), ln1_g[l], ln1_b[l], router_w[l], router_b[l])

        def with_shared(w_exp, w_sh):
            w_sh = jnp.transpose(w_sh.reshape(D_MODEL, D_SHARED // D_EXPERT, D_EXPERT), (1, 0, 2))
            return jnp.concatenate([w_exp, w_sh], 0).astype(BF16)

        wd = jnp.concatenate([exp_w_down[l].reshape(-1, D_MODEL), sh_w_down[l]], 0).astype(BF16)
        x = moe_norm(x, expert_gate, with_shared(exp_w_gate[l], sh_w_gate[l]), with_shared(exp_w_up[l], sh_w_up[l]),
                     wd, ln2_g[l], ln2_b[l])

        for lst, val in zip(st_p, (rows_diff, rows_nsa, rows_moba)):
            lst.append(pr(val))
        n_keep = min(WINDOW, sp)
        st_p[3].append(pr(rows_win)[:, sp - n_keep:])
        for lst, val in zip(st_s, (rows_diff, rows_nsa, rows_moba)):
            lst.append(sm(val))
        full_win = jnp.concatenate([state_nsa_win[l], sm(rows_win)], 1)
        n_keep = min(WINDOW, past_len + ss)
        st_s[3].append(full_win[:, full_win.shape[1] - n_keep:])

    outs_p = [jnp.stack(a, 0) for a in st_p]
    outs_s = [jnp.stack(a, 0) for a in st_s]
    return (x[:tp].reshape(bp, sp, D_MODEL), x[tp:].reshape(bs, ss, D_MODEL),
            outs_p[0], outs_s[0], outs_p[1], outs_s[1], outs_p[2], outs_s[2], outs_p[3], outs_s[3])
```

```python
import functools
import math

import jax
import jax.numpy as jnp
from jax import lax
from jax.experimental import pallas as pl
from jax.experimental.pallas import tpu as pltpu

F32 = jnp.float32
BF16 = jnp.bfloat16

D_MODEL = 1024
DEPTH = 2
PAGE_SIZE = 128
HEAD_DIM = 64
HALF = HEAD_DIM // 2
ATTN_SCALE = HEAD_DIM ** -0.5
ROPE_THETA = 10000.0
DIFF_HEADS = 4
NSA_HEADS = 4
CMP_LEN = 32
CMP_STRIDE = 16
SLC_LEN = 64
SLC_TOPN = 16
WINDOW = 512
MOBA_HEADS = 4
MOBA_BLOCK = 256
MOBA_TOPK = 3
N_EXPERTS = 64
N_EXPERT_GROUPS = 8
TOPK_GROUPS = 4
TOP_K = 6
D_EXPERT = 128
D_SHARED = 256
ROUTED_SCALE = 2.5
LN_EPS = 1e-5
RMS_EPS = 1e-5
DEEPNORM_ALPHA = (2 * DEPTH) ** 0.25

C_DAQ, C_DAK, C_DAV, C_NQ = 0, 512, 1024, 1536
C_CK, C_CV, C_SK, C_SV, C_WK, C_WV, C_NG = 1792, 1856, 1920, 1984, 2048, 2112, 2176
N_GATE = 3 * NSA_HEADS
C_MQ_SRC = C_NG + N_GATE
C_MQ, C_MK, C_MV, C_MG = 2304, 2560, 2816, 3072
N_IN_PAD = C_MG + 3 * D_MODEL

LANES = 128
NEG = -1e30
VMEM_LIMIT = 56 * 1024 * 1024
MOE_EXPERTS_PER_STEP = 11

def _nt_dot(a, b):
    return lax.dot_general(a, b, (((1,), (1,)), ((), ())), preferred_element_type=F32)


def _mm_kernel(x_ref, w_ref, o_ref):
    o_ref[...] = jnp.dot(x_ref[...].astype(BF16), w_ref[...],
                         preferred_element_type=F32).astype(o_ref.dtype)


def matmul(x, w, tm, tn, out_dtype=F32):
    m, k = x.shape
    n = w.shape[1]
    assert m % tm == 0 and n % tn == 0, (x.shape, w.shape, tm, tn)
    return pl.pallas_call(
        _mm_kernel,
        out_shape=jax.ShapeDtypeStruct((m, n), out_dtype),
        grid=(m // tm, n // tn),
        in_specs=[pl.BlockSpec((tm, k), lambda i, j: (i, 0)),
                  pl.BlockSpec((k, tn), lambda i, j: (0, j))],
        out_specs=pl.BlockSpec((tm, tn), lambda i, j: (i, j)),
        compiler_params=pltpu.CompilerParams(
            dimension_semantics=("parallel", "arbitrary"), vmem_limit_bytes=VMEM_LIMIT),
        name="matmul",
    )(x, w)


def _repack_kernel(lo_ref, hi_ref, o_ref, *, first_moved_tile, shift):
    j = pl.program_id(0)
    lane = lax.broadcasted_iota(jnp.int32, lo_ref.shape, 1)
    hi = hi_ref[...]
    kept = jnp.where((j < first_moved_tile - 1) | (lane < LANES - shift), hi, 0.0)
    moved = jnp.where(lane < shift, pltpu.roll(lo_ref[...], shift, 1), pltpu.roll(hi, shift, 1))
    o_ref[...] = jnp.where(j < first_moved_tile, kept, moved).astype(o_ref.dtype)


def repack_input_weight(w):
    d, n_src = w.shape
    shift = C_MQ - C_MQ_SRC
    last = (n_src - 1) // LANES
    assert C_MQ % LANES == 0 and 0 < shift < LANES and n_src + shift == N_IN_PAD
    return pl.pallas_call(
        functools.partial(_repack_kernel, first_moved_tile=C_MQ // LANES, shift=shift),
        out_shape=jax.ShapeDtypeStruct((d, N_IN_PAD), BF16),
        grid=(N_IN_PAD // LANES,),
        in_specs=[pl.BlockSpec((d, LANES), lambda j: (0, jnp.maximum(j - 1, 0))),
                  pl.BlockSpec((d, LANES), lambda j: (0, jnp.minimum(j, last)))],
        out_specs=pl.BlockSpec((d, LANES), lambda j: (0, j)),
        compiler_params=pltpu.CompilerParams(dimension_semantics=("parallel",), vmem_limit_bytes=VMEM_LIMIT),
        name="repack_input_weight",
    )(w, w)


def _project_kernel(x_ref, w_ref, cos_ref, sa_ref, sb_ref, *refs, n_prompt_tiles, n_carry):
    (rdp_ref, rds_ref, rnp_ref, rns_ref, rmp_ref, rms_ref, rw_ref, mg_ref, ng_ref,
     qd_ref, kd_ref, vd_ref, nq_ref, nqr_ref, ss_ref, ww_ref, mq_ref, mk_ref, mv_ref) = refs[n_carry:]
    tile = pl.program_id(0)

    def put_rows(prompt_ref, sample_ref, lo, hi, val):
        @pl.when(tile < n_prompt_tiles)
        def _():
            prompt_ref[:, lo:hi] = val

        @pl.when(tile >= n_prompt_tiles)
        def _():
            sample_ref[:, lo:hi] = val

    xb = x_ref[...].astype(BF16)
    cos, sa, sb = cos_ref[...], sa_ref[...], sb_ref[...]

    def seg(a, b):
        return jnp.dot(xb, w_ref[:, a:b], preferred_element_type=F32)

    def rope(y):
        tiles = []
        for t in range(y.shape[1] // LANES):
            yt = y[:, t * LANES:(t + 1) * LANES]
            tiles.append(yt * cos + pltpu.roll(yt, LANES - HALF, 1) * sa + pltpu.roll(yt, HALF, 1) * sb)
        return tiles[0] if len(tiles) == 1 else jnp.concatenate(tiles, axis=1)

    first = lax.broadcasted_iota(jnp.int32, (x_ref.shape[0], LANES), 1) < HEAD_DIM
    qd_ref[...] = (rope(seg(C_DAQ, C_DAK)) * ATTN_SCALE).astype(BF16)
    k = rope(seg(C_DAK, C_DAV))
    v = seg(C_DAV, C_NQ)
    put_rows(rdp_ref, rds_ref, 0, C_DAV - C_DAK, k)
    put_rows(rdp_ref, rds_ref, C_DAV - C_DAK, C_NQ - C_DAK, v)
    kd_ref[...] = k.astype(BF16)
    vd_ref[...] = v.astype(BF16)
    nq = seg(C_NQ, C_CK)
    nq_ref[...] = (nq * ATTN_SCALE).astype(BF16)
    nqr_ref[...] = (rope(nq) * ATTN_SCALE).astype(BF16)
    y = seg(C_CK, C_NG)
    ss = jnp.where(first, rope(y[:, LANES:2 * LANES]), y[:, LANES:2 * LANES])
    ww = jnp.where(first, rope(y[:, 2 * LANES:3 * LANES]), y[:, 2 * LANES:3 * LANES])
    put_rows(rnp_ref, rns_ref, 0, LANES, y[:, 0:LANES])
    put_rows(rnp_ref, rns_ref, LANES, 2 * LANES, ss)
    rw_ref[...] = ww
    ss_ref[...] = ss.astype(BF16)
    ww_ref[...] = ww.astype(BF16)
    ng_ref[...] = seg(C_NG, C_MQ)
    mq_ref[...] = (rope(seg(C_MQ, C_MK)) * ATTN_SCALE).astype(BF16)
    k = rope(seg(C_MK, C_MV))
    v = seg(C_MV, C_MG)
    put_rows(rmp_ref, rms_ref, 0, C_MV - C_MK, k)
    put_rows(rmp_ref, rms_ref, C_MV - C_MK, C_MG - C_MK, v)
    mk_ref[...] = k.astype(BF16)
    mv_ref[...] = v.astype(BF16)
    for t in range(3):
        mg_ref[:, t * D_MODEL:(t + 1) * D_MODEL] = seg(C_MG + t * D_MODEL, C_MG + (t + 1) * D_MODEL)


def project_inputs(x, w_pad, cos, sin, layer, n_prompt, cache_rows=None, tm=256):
    t = x.shape[0]
    n_sample = t - n_prompt
    assert n_prompt % tm == 0 and n_sample % tm == 0
    npt = n_prompt // tm
    zero = jnp.zeros_like(sin)
    cos_t = jnp.tile(cos, (1, LANES // HALF))
    sa = jnp.concatenate([-sin, zero, -sin, zero], axis=1)
    sb = jnp.concatenate([zero, sin, zero, sin], axis=1)
    row_widths = (1024, 256, 512)
    names_f32 = (("rows_win", 128), ("merge_gate", 3 * D_MODEL), ("nsa_gate", LANES))
    names_bf16 = (("q_diff", 512), ("k_diff", 512), ("v_diff", 512), ("q_nsa", 256), ("q_nsa_rot", 256),
                  ("kv_slc", 128), ("kv_win", 128), ("q_moba", 256), ("k_moba", 256), ("v_moba", 256))
    row = lambda i: (i, 0)
    rows_shapes, rows_specs = [], []
    for n in row_widths:
        rows_shapes += [jax.ShapeDtypeStruct((DEPTH, n_prompt, n), F32), jax.ShapeDtypeStruct((DEPTH, n_sample, n), F32)]
        rows_specs += [pl.BlockSpec((None, tm, n), lambda i: (layer, jnp.minimum(i, npt - 1), 0)),
                       pl.BlockSpec((None, tm, n), lambda i: (layer, jnp.maximum(i - npt, 0), 0))]
    carry = list(cache_rows) if cache_rows is not None else []
    n_fixed = 5
    outs = pl.pallas_call(
        functools.partial(_project_kernel, n_prompt_tiles=npt, n_carry=len(carry)),
        out_shape=rows_shapes + [jax.ShapeDtypeStruct((t, n), F32) for _, n in names_f32]
        + [jax.ShapeDtypeStruct((t, n), BF16) for _, n in names_bf16],
        grid=(t // tm,),
        in_specs=[pl.BlockSpec((tm, D_MODEL), row),
                  pl.BlockSpec(w_pad.shape, lambda i: (0, 0), pipeline_mode=pl.Buffered(1)),
                  pl.BlockSpec((tm, LANES), row), pl.BlockSpec((tm, LANES), row), pl.BlockSpec((tm, LANES), row)]
        + [pl.BlockSpec(memory_space=pl.ANY)] * len(carry),
        out_specs=rows_specs + [pl.BlockSpec((tm, n), row) for _, n in names_f32 + names_bf16],
        input_output_aliases={n_fixed + k: k for k in range(len(carry))},
        compiler_params=pltpu.CompilerParams(dimension_semantics=("arbitrary",), vmem_limit_bytes=VMEM_LIMIT),
        name="project_inputs",
    )(x, w_pad, cos_t, sa, sb, *carry)
    res = dict(zip([n for n, _ in names_f32 + names_bf16], outs[len(rows_shapes):]))
    res["cache_rows"] = tuple(outs[:len(rows_shapes)])
    return res


def _online_update(s, v, m_ref, l_ref, acc_ref):
    tk = s.shape[1]
    dv = acc_ref.shape[-1]
    m_prev = m_ref[...]
    m_next = jnp.maximum(m_prev, jnp.max(s, axis=-1, keepdims=True))
    alpha = jnp.exp(m_prev - m_next)
    p = jnp.exp(s - jnp.concatenate([m_next] * (tk // LANES), axis=1))
    l_ref[...] = alpha * l_ref[...] + jnp.sum(p, axis=-1, keepdims=True)
    m_ref[...] = m_next
    acc_ref[...] = acc_ref[...] * alpha[:, :dv] + jnp.dot(p.astype(BF16), v, preferred_element_type=F32)


def _init_state(*refs):
    for m_ref, l_ref, acc_ref in zip(refs[0::3], refs[1::3], refs[2::3]):
        m_ref[...] = jnp.full(m_ref.shape, NEG, F32)
        l_ref[...] = jnp.zeros(l_ref.shape, F32)
        acc_ref[...] = jnp.zeros(acc_ref.shape, F32)


def _normalised(l_ref, acc_ref):
    dv = acc_ref.shape[-1]
    return acc_ref[...] / jnp.maximum(l_ref[...], 1e-30)[:, :dv]


def _diff_attn_kernel(q_ref, k_ref, v_ref, par_ref, o_ref,
                      m0, l0, a0, m1, l1, a1, *, tq, tk, out_scale):
    qi = pl.program_id(2)
    q = q_ref[...]
    lane = lax.broadcasted_iota(jnp.int32, q.shape, 1)
    zero = jnp.zeros_like(q)
    qa = jnp.where(lane < HEAD_DIM, q, zero)
    qb = jnp.where(lane >= HEAD_DIM, q, zero)
    q0 = qi * tq
    row = q0 + lax.broadcasted_iota(jnp.int32, (tq, tk), 0)
    col = lax.broadcasted_iota(jnp.int32, (tq, tk), 1)
    _init_state(m0, l0, a0, m1, l1, a1)

    def step(j, causal):
        start = pl.multiple_of(j * tk, tk)
        k = k_ref[pl.ds(start, tk), :]
        v = v_ref[pl.ds(start, tk), :]
        sa, sb = _nt_dot(qa, k), _nt_dot(qb, k)
        if causal:
            mask = (col + start) <= row
            sa, sb = jnp.where(mask, sa, NEG), jnp.where(mask, sb, NEG)
        _online_update(sa, v, m0, l0, a0)
        _online_update(sb, v, m1, l1, a1)

    n_full = (q0 + 1) // tk
    lax.fori_loop(0, n_full, lambda j, c: (step(j, False), c)[1], 0)
    lax.fori_loop(n_full, (q0 + tq + tk - 1) // tk, lambda j, c: (step(j, True), c)[1], 0)
    lam = par_ref[0:1, :]
    g = par_ref[1:2, :]
    o = _normalised(l0, a0) - lam * _normalised(l1, a1)
    o = o * lax.rsqrt(jnp.mean(o * o, axis=-1, keepdims=True) + RMS_EPS) * g
    o_ref[...] = (o * out_scale).astype(o_ref.dtype)


def diff_attention(q, k, v, n_batch, s, lam, subln_g, lambda_init, tq=256, tk=512):
    nq = s // tq
    par = jnp.concatenate([jnp.broadcast_to(lam.astype(F32), (1, LANES)),
                           subln_g.astype(F32).reshape(1, LANES),
                           jnp.zeros((6, LANES), F32)], 0)
    kern = functools.partial(_diff_attn_kernel, tq=tq, tk=tk, out_scale=1.0 - lambda_init)
    st = [pltpu.VMEM((tq, LANES), F32)] * 6
    return pl.pallas_call(
        kern,
        out_shape=jax.ShapeDtypeStruct((n_batch * s, DIFF_HEADS * LANES), F32),
        grid=(n_batch, DIFF_HEADS, nq),
        in_specs=[pl.BlockSpec((tq, LANES), lambda bi, h, i: (bi * nq + i, h)),
                  pl.BlockSpec((s, LANES), lambda bi, h, i: (bi, h)),
                  pl.BlockSpec((s, LANES), lambda bi, h, i: (bi, h)),
                  pl.BlockSpec((8, LANES), lambda bi, h, i: (0, 0))],
        out_specs=pl.BlockSpec((tq, LANES), lambda bi, h, i: (bi * nq + i, h)),
        scratch_shapes=st,
        compiler_params=pltpu.CompilerParams(
            dimension_semantics=("parallel", "parallel", "arbitrary"), vmem_limit_bytes=VMEM_LIMIT),
        name="diff_attention",
    )(q, k, v, par)


def _rank_select(score, n_candidates, top_n):
    lane = lax.broadcasted_iota(jnp.int32, score.shape, 1)
    rank = jnp.zeros(score.shape, F32)
    for c in range(n_candidates):
        col = score[:, c:c + 1]
        ahead = (col > score) | ((col == score) & (c < lane))
        rank = rank + jnp.where(ahead, 1.0, 0.0)
    return rank < top_n


def _rank_select_rows(score, n_candidates, top_n):
    idx = lax.broadcasted_iota(jnp.int32, score.shape, 0)
    rank = jnp.zeros(score.shape, F32)
    for c in range(n_candidates):
        cand = score[c:c + 1]
        ahead = (cand > score) | ((cand == score) & (c < idx))
        rank = rank + jnp.where(ahead, 1.0, 0.0)
    return rank < top_n


def _rows_to_lanes(x_t):
    n, q = x_t.shape
    if n < LANES:
        x_t = jnp.concatenate([x_t, jnp.zeros((LANES - n, q), F32)], axis=0)
    return jnp.concatenate([x_t[:, c * LANES:(c + 1) * LANES].T for c in range(q // LANES)], axis=0)


def _stack_heads(q_ref, tq):
    first = lax.broadcasted_iota(jnp.int32, (tq, LANES), 1) < HEAD_DIM
    tiles = []
    for t in range(NSA_HEADS // 2):
        pair = q_ref[:, t * LANES:(t + 1) * LANES].astype(F32)
        tiles.append(jnp.where(first, pair, 0.0))
        tiles.append(jnp.where(first, pltpu.roll(pair, HEAD_DIM, 1), 0.0))
    return jnp.concatenate(tiles, axis=0).astype(BF16)


def _nsa_attn_kernel(qn_ref, qr_ref, cc_ref, ss_ref, ww_ref, gate_ref, ov_ref, e_ref, o_ref, m, l, acc,
                     *, tq, tk, tkw, n_cmp, n_slc):
    qi = pl.program_id(1)
    q0 = qi * tq
    nh = NSA_HEADS
    r = nh * tq
    qn = _stack_heads(qn_ref, tq)
    qr = _stack_heads(qr_ref, tq)

    cc = cc_ref[...]
    n_pad = cc.shape[0]
    qpos = q0 + lax.broadcasted_iota(jnp.int32, (tq, n_pad), 0)
    qpos = jnp.concatenate([qpos] * nh, axis=0)
    n_idx = lax.broadcasted_iota(jnp.int32, (r, n_pad), 1)
    ok = (n_idx * CMP_STRIDE + CMP_LEN - 1 <= qpos) & (n_idx < n_cmp)
    sc = jnp.where(ok, _nt_dot(qn, cc), NEG)
    pc = jnp.where(ok, jnp.exp(sc - jnp.max(sc, axis=-1, keepdims=True)), 0.0)
    pc = pc / jnp.maximum(jnp.sum(pc, axis=-1, keepdims=True), 1e-30)
    p_hi = pc.astype(BF16)
    p_lo = (pc - p_hi.astype(F32)).astype(BF16)
    o_cmp = jnp.dot(p_hi, cc, preferred_element_type=F32)
    ovt = ov_ref[...]
    imp = _nt_dot(ovt, p_hi) + _nt_dot(ovt, p_lo)
    imp = imp[:, 0:tq] + imp[:, tq:2 * tq] + imp[:, 2 * tq:3 * tq] + imp[:, 3 * tq:4 * tq]
    imp = imp[0:-(-n_slc // 8) * 8]
    blk = lax.broadcasted_iota(jnp.int32, imp.shape, 0)
    cur = (q0 + lax.broadcasted_iota(jnp.int32, imp.shape, 1)) // SLC_LEN
    valid = blk <= cur
    forced = (blk == 0) | (blk == cur) | (blk == cur - 1)
    score = jnp.where(forced, jnp.inf, jnp.where(valid, imp, -jnp.inf))
    chosen = jnp.where(valid & _rank_select_rows(score, n_slc, min(SLC_TOPN, n_slc)), 1.0, 0.0)
    chosen = _rows_to_lanes(chosen).astype(BF16)
    chosen = jnp.concatenate([chosen] * nh, axis=0)

    def sweep(kv_ref, tile, lo, hi, mask_fn):
        _init_state(m, l, acc)

        def body(j, c):
            start = pl.multiple_of(j * tile, tile)
            kv = kv_ref[pl.ds(start, tile), :]
            _online_update(jnp.where(mask_fn(j, start), _nt_dot(qr, kv), NEG), kv, m, l, acc)
            return c

        lax.fori_loop(lo, hi, body, 0)
        return _normalised(l, acc)

    row = jnp.concatenate([q0 + lax.broadcasted_iota(jnp.int32, (tq, tk), 0)] * nh, axis=0)
    col = lax.broadcasted_iota(jnp.int32, (r, tk), 1)
    o_slc = sweep(ss_ref, tk, 0, (q0 + tq + tk - 1) // tk,
                  lambda j, start: ((col + start) <= row)
                  & (jnp.dot(chosen, e_ref[j], preferred_element_type=F32) > 0.5))
    row_w = jnp.concatenate([q0 + lax.broadcasted_iota(jnp.int32, (tq, tkw), 0)] * nh, axis=0)
    col_w = lax.broadcasted_iota(jnp.int32, (r, tkw), 1)
    o_win = sweep(ww_ref, tkw, jnp.maximum(q0 - WINDOW + 1, 0) // tkw, (q0 + tq + tkw - 1) // tkw,
                  lambda j, start: ((col_w + start) <= row_w) & ((col_w + start) > row_w - WINDOW))

    g = jax.nn.sigmoid(gate_ref[...])
    first = lax.broadcasted_iota(jnp.int32, (tq, LANES), 1) < HEAD_DIM
    mixed = []
    for h in range(nh):
        rows = slice(h * tq, (h + 1) * tq)
        mixed.append(g[:, 3 * h:3 * h + 1] * o_cmp[rows] + g[:, 3 * h + 1:3 * h + 2] * o_slc[rows]
                     + g[:, 3 * h + 2:3 * h + 3] * o_win[rows])
    for t in range(nh // 2):
        o_ref[:, t * LANES:(t + 1) * LANES] = jnp.where(first, pltpu.roll(mixed[2 * t], HEAD_DIM, 1), mixed[2 * t + 1])


def _block_expander(n_chunks, n_rows, tk, block_len):
    key = jnp.arange(n_chunks)[:, None, None] * tk + jnp.arange(tk)[None, None, :]
    return (key // block_len == jnp.arange(n_rows)[None, :, None]).astype(BF16)


def nsa_attention(qn, qr, cc, ss, ww, gate, n_batch, s, tq=128, tk=512, tkw=256):
    nq = s // tq
    n_cmp = cc.shape[1]
    n_pad = -(-n_cmp // LANES) * LANES
    cc = jnp.pad(cc, ((0, 0), (0, n_pad - n_cmp), (0, 0)))
    n_slc = s // SLC_LEN
    assert n_slc <= LANES
    cmp_start = jnp.arange(n_pad) * CMP_STRIDE
    slc_start = jnp.arange(LANES) * SLC_LEN
    ov = ((cmp_start[None, :] <= slc_start[:, None] + SLC_LEN - 1)
          & (cmp_start[None, :] + CMP_LEN - 1 >= slc_start[:, None])).astype(BF16)
    expander = _block_expander(s // tk, LANES, tk, SLC_LEN)
    r = NSA_HEADS * tq
    row = lambda bi, i: (bi * nq + i, 0)
    return pl.pallas_call(
        functools.partial(_nsa_attn_kernel, tq=tq, tk=tk, tkw=tkw, n_cmp=n_cmp, n_slc=n_slc),
        out_shape=jax.ShapeDtypeStruct((n_batch * s, NSA_HEADS * HEAD_DIM), F32),
        grid=(n_batch, nq),
        in_specs=[pl.BlockSpec((tq, NSA_HEADS * HEAD_DIM), row),
                  pl.BlockSpec((tq, NSA_HEADS * HEAD_DIM), row),
                  pl.BlockSpec((None, n_pad, LANES), lambda bi, i: (bi, 0, 0)),
                  pl.BlockSpec((s, LANES), lambda bi, i: (bi, 0)),
                  pl.BlockSpec((s, LANES), lambda bi, i: (bi, 0)),
                  pl.BlockSpec((tq, LANES), row),
                  pl.BlockSpec(ov.shape, lambda bi, i: (0, 0)),
                  pl.BlockSpec(expander.shape, lambda bi, i: (0, 0, 0))],
        out_specs=pl.BlockSpec((tq, NSA_HEADS * HEAD_DIM), row),
        scratch_shapes=[pltpu.VMEM((r, LANES), F32), pltpu.VMEM((r, LANES), F32), pltpu.VMEM((r, LANES), F32)],
        compiler_params=pltpu.CompilerParams(
            dimension_semantics=("parallel", "arbitrary"), vmem_limit_bytes=VMEM_LIMIT),
        name="nsa_attention",
    )(qn, qr, cc, ss, ww, gate, ov, expander)


def _moba_attn_kernel(q_ref, k_ref, v_ref, km_ref, e_ref, o_ref, m0, l0, a0, m1, l1, a1, *, tq, tk, n_blk):
    qi = pl.program_id(2)
    q = q_ref[...]
    lane = lax.broadcasted_iota(jnp.int32, q.shape, 1)
    zero = jnp.zeros_like(q)
    qa = jnp.where(lane < HEAD_DIM, q, zero)
    qb = jnp.where(lane >= HEAD_DIM, q, zero)
    q0 = qi * tq
    row = q0 + lax.broadcasted_iota(jnp.int32, (tq, tk), 0)
    col = lax.broadcasted_iota(jnp.int32, (tq, tk), 1)

    km = km_ref[...]
    blk = lax.broadcasted_iota(jnp.int32, (km.shape[0], tq), 0)
    cur = (q0 + lax.broadcasted_iota(jnp.int32, (km.shape[0], tq), 1)) // MOBA_BLOCK
    earlier = blk < cur

    def block_mask(qh):
        gate = jnp.where(earlier, _nt_dot(km, qh), -jnp.inf)
        chosen = (earlier & _rank_select_rows(gate, n_blk, min(MOBA_TOPK, n_blk))) | (blk == cur)
        return _rows_to_lanes(jnp.where(chosen, 1.0, 0.0)).astype(BF16)

    bm0 = block_mask(qa)
    bm1 = block_mask(qb)
    _init_state(m0, l0, a0, m1, l1, a1)

    def body(j, carry):
        start = pl.multiple_of(j * tk, tk)
        k = k_ref[pl.ds(start, tk), :]
        v = v_ref[pl.ds(start, tk), :]
        causal = (col + start) <= row
        e = e_ref[j]
        mask0 = causal & (jnp.dot(bm0, e, preferred_element_type=F32) > 0.5)
        mask1 = causal & (jnp.dot(bm1, e, preferred_element_type=F32) > 0.5)
        _online_update(jnp.where(mask0, _nt_dot(qa, k), NEG), v, m0, l0, a0)
        _online_update(jnp.where(mask1, _nt_dot(qb, k), NEG), v, m1, l1, a1)
        return carry

    lax.fori_loop(0, (q0 + tq + tk - 1) // tk, body, 0)
    o_ref[...] = jnp.where(lane < HEAD_DIM, _normalised(l0, a0), _normalised(l1, a1)).astype(o_ref.dtype)


def moba_attention(q, k, v, k_mean, n_batch, s, tq=256, tk=512):
    nq = s // tq
    n_blk = k_mean.shape[1]
    n_pad = -(-n_blk // 16) * 16
    km = jnp.pad(k_mean, ((0, 0), (0, n_pad - n_blk), (0, 0))).astype(BF16)
    expander = _block_expander(s // tk, LANES, tk, MOBA_BLOCK)
    st = [pltpu.VMEM((tq, LANES), F32)] * 6
    return pl.pallas_call(
        functools.partial(_moba_attn_kernel, tq=tq, tk=tk, n_blk=n_blk),
        out_shape=jax.ShapeDtypeStruct((n_batch * s, MOBA_HEADS * HEAD_DIM), F32),
        grid=(n_batch, MOBA_HEADS // 2, nq),
        in_specs=[pl.BlockSpec((tq, LANES), lambda bi, h, i: (bi * nq + i, h)),
                  pl.BlockSpec((s, LANES), lambda bi, h, i: (bi, h)),
                  pl.BlockSpec((s, LANES), lambda bi, h, i: (bi, h)),
                  pl.BlockSpec((None, n_pad, LANES), lambda bi, h, i: (bi, 0, h)),
                  pl.BlockSpec(expander.shape, lambda bi, h, i: (0, 0, 0))],
        out_specs=pl.BlockSpec((tq, LANES), lambda bi, h, i: (bi * nq + i, h)),
        scratch_shapes=st,
        compiler_params=pltpu.CompilerParams(
            dimension_semantics=("parallel", "parallel", "arbitrary"), vmem_limit_bytes=VMEM_LIMIT),
        name="moba_attention",
    )(q, k, v, km, expander)


def _layer_norm(z, g, b):
    mu = jnp.mean(z, axis=-1, keepdims=True)
    zc = z - mu
    var = jnp.mean(zc * zc, axis=-1, keepdims=True)
    return zc * lax.rsqrt(var + LN_EPS) * g + b


def _split_bf16(a):
    hi = a.astype(BF16)
    return hi, (a - hi.astype(F32)).astype(BF16)


def _route_experts(xn, rwh_ref, rwl_ref, rb_ref):
    tm = xn.shape[0]
    group = N_EXPERTS // N_EXPERT_GROUPS
    xh, xl = _split_bf16(xn)
    rwh = rwh_ref[...]
    logits = _nt_dot(rwh, xh) + _nt_dot(rwh, xl) + _nt_dot(rwl_ref[...], xh)
    s = jax.nn.sigmoid(logits)
    sb = s + jnp.concatenate([rb_ref[...]] * (tm // LANES), axis=1)
    grp = sb.reshape(N_EXPERT_GROUPS, group, tm)
    member = lax.broadcasted_iota(jnp.int32, grp.shape, 1).astype(F32)
    m1 = jnp.max(grp, axis=1, keepdims=True)
    first = jnp.min(jnp.where(grp == m1, member, float(group)), axis=1, keepdims=True)
    m2 = jnp.max(jnp.where(member == first, -jnp.inf, grp), axis=1, keepdims=True)
    group_ok = _rank_select_rows(m1 + m2, N_EXPERT_GROUPS, TOPK_GROUPS)
    cand = jnp.where(group_ok, grp, -jnp.inf).reshape(N_EXPERTS, tm)
    w = jnp.where(_rank_select_rows(cand, N_EXPERTS, TOP_K), s, 0.0)
    w = w / jnp.sum(w, axis=0, keepdims=True) * ROUTED_SCALE
    gate = _rows_to_lanes(w)
    hi = gate.astype(BF16).astype(F32)
    lane = lax.broadcasted_iota(jnp.int32, gate.shape, 1)
    return jnp.where(lane < N_EXPERTS, hi, pltpu.roll(gate - hi, N_EXPERTS, 1)).astype(BF16)


def _merge_kernel(x_ref, odp_ref, ods_ref, onp_ref, ons_ref, omp_ref, oms_ref, g0_ref, g1_ref, g2_ref,
                  wd_ref, wn_ref, wm_ref, wo_ref, ln_ref, rwh_ref, rwl_ref, rb_ref, o_ref, gate_ref,
                  *, n_prompt_tiles):
    from_prompt = pl.program_id(0) < n_prompt_tiles

    def branch(p_r, s_r, w_r, g_r):
        o = jnp.where(from_prompt, p_r[...], s_r[...])
        y = jnp.dot(o.astype(BF16), w_r[...], preferred_element_type=F32)
        return jax.nn.sigmoid(g_r[...]) * y

    merged = (branch(odp_ref, ods_ref, wd_ref, g0_ref) + branch(onp_ref, ons_ref, wn_ref, g1_ref)
              + branch(omp_ref, oms_ref, wm_ref, g2_ref))
    y = jnp.dot(merged.astype(BF16), wo_ref[...], preferred_element_type=F32)
    z = DEEPNORM_ALPHA * x_ref[...] + y
    xn = _layer_norm(z, ln_ref[0:1, :], ln_ref[1:2, :])
    o_ref[...] = xn
    gate_ref[...] = _route_experts(xn, rwh_ref, rwl_ref, rb_ref)


def merge_project_norm(x, o_diff, o_nsa, o_moba, merge_gate, w_d, w_n, w_m, w_o, ln_g, ln_b,
                       router_w, router_b, tm=256):
    assert 2 * N_EXPERTS == LANES
    t = x.shape[0]
    tp, ts = o_diff[0].shape[0], o_diff[1].shape[0]
    assert tp % tm == 0 and ts % tm == 0 and tp + ts == t
    npt = tp // tm
    ln = jnp.concatenate([ln_g.reshape(1, -1), ln_b.reshape(1, -1), jnp.zeros((6, D_MODEL), F32)], 0)
    rwh, rwl = _split_bf16(router_w.astype(F32).T)
    rb = jnp.broadcast_to(router_b.astype(F32)[:, None], (N_EXPERTS, LANES))
    row = lambda i: (i, 0)
    full = lambda i: (0, 0)
    pair_specs, pair_args = [], []
    for o_p, o_s in (o_diff, o_nsa, o_moba):
        pair_specs += [pl.BlockSpec((tm, o_p.shape[1]), lambda i: (jnp.minimum(i, npt - 1), 0)),
                       pl.BlockSpec((tm, o_s.shape[1]), lambda i: (jnp.maximum(i - npt, 0), 0))]
        pair_args += [o_p, o_s]
    return pl.pallas_call(
        functools.partial(_merge_kernel, n_prompt_tiles=npt),
        out_shape=[jax.ShapeDtypeStruct((t, D_MODEL), F32), jax.ShapeDtypeStruct((t, LANES), BF16)],
        grid=(t // tm,),
        in_specs=[pl.BlockSpec((tm, D_MODEL), row)] + pair_specs + [
                  pl.BlockSpec((tm, D_MODEL), lambda i: (i, 0)),
                  pl.BlockSpec((tm, D_MODEL), lambda i: (i, 1)),
                  pl.BlockSpec((tm, D_MODEL), lambda i: (i, 2)),
                  pl.BlockSpec(w_d.shape, full), pl.BlockSpec(w_n.shape, full),
                  pl.BlockSpec(w_m.shape, full), pl.BlockSpec(w_o.shape, full),
                  pl.BlockSpec((8, D_MODEL), full),
                  pl.BlockSpec(rwh.shape, full), pl.BlockSpec(rwl.shape, full), pl.BlockSpec(rb.shape, full)],
        out_specs=[pl.BlockSpec((tm, D_MODEL), row), pl.BlockSpec((tm, LANES), row)],
        compiler_params=pltpu.CompilerParams(
            dimension_semantics=("parallel",), vmem_limit_bytes=VMEM_LIMIT),
        name="merge_project_norm",
    )(x, *pair_args, merge_gate, merge_gate, merge_gate, w_d, w_n, w_m, w_o, ln, rwh, rwl, rb)


def _moe_kernel(x_ref, gate_ref, pick_ref, wg_ref, wu_ref, wd_ref, ln_ref, o_ref, acc_ref):
    f = pl.program_id(1)

    @pl.when(f == 0)
    def _():
        acc_ref[...] = jnp.zeros(acc_ref.shape, F32)

    n_e = wg_ref.shape[0]
    wg = jnp.concatenate([wg_ref[j] for j in range(n_e)], axis=1)
    wu = jnp.concatenate([wu_ref[j] for j in range(n_e)], axis=1)
    xb = x_ref[...].astype(BF16)
    hg = jnp.dot(xb, wg, preferred_element_type=F32)
    hu = jnp.dot(xb, wu, preferred_element_type=F32)
    w = jnp.dot(gate_ref[...], pick_ref[...], preferred_element_type=F32)
    lane = lax.broadcasted_iota(jnp.int32, w.shape, 1)
    w = jnp.where((f * n_e + lane >= N_EXPERTS) & (lane < n_e), 1.0, w)
    h = jax.nn.silu(hg) * hu
    h = jnp.concatenate([h[:, j * D_EXPERT:(j + 1) * D_EXPERT] * w[:, j:j + 1] for j in range(n_e)], axis=1)
    acc_ref[...] += jnp.dot(h.astype(BF16), wd_ref[...], preferred_element_type=F32)

    @pl.when(f == pl.num_programs(1) - 1)
    def _():
        z = DEEPNORM_ALPHA * x_ref[...] + acc_ref[...]
        o_ref[...] = _layer_norm(z, ln_ref[0:1, :], ln_ref[1:2, :])


def moe_norm(x, gate, wg, wu, wd, ln_g, ln_b, tm=768, experts_per_step=MOE_EXPERTS_PER_STEP):
    t = x.shape[0]
    f_tot = wd.shape[0]
    tf = experts_per_step * D_EXPERT
    n_col = gate.shape[1]
    n_steps = wg.shape[0] // experts_per_step
    assert wg.shape[0] % experts_per_step == 0 and t % tm == 0
    expert = jnp.arange(n_steps)[:, None, None] * experts_per_step + jnp.arange(LANES)[None, None, :]
    src = jnp.arange(n_col)[None, :, None] % N_EXPERTS
    pick = ((src == expert) & (jnp.arange(LANES)[None, None, :] < experts_per_step)).astype(BF16)
    ln = jnp.concatenate([ln_g.reshape(1, -1), ln_b.reshape(1, -1), jnp.zeros((6, D_MODEL), F32)], 0)
    return pl.pallas_call(
        _moe_kernel,
        out_shape=jax.ShapeDtypeStruct((t, D_MODEL), F32),
        grid=(t // tm, f_tot // tf),
        in_specs=[pl.BlockSpec((tm, D_MODEL), lambda i, f: (i, 0)),
                  pl.BlockSpec((tm, n_col), lambda i, f: (i, 0)),
                  pl.BlockSpec((None, n_col, LANES), lambda i, f: (f, 0, 0)),
                  pl.BlockSpec((experts_per_step, D_MODEL, D_EXPERT), lambda i, f: (f, 0, 0)),
                  pl.BlockSpec((experts_per_step, D_MODEL, D_EXPERT), lambda i, f: (f, 0, 0)),
                  pl.BlockSpec((tf, D_MODEL), lambda i, f: (f, 0)),
                  pl.BlockSpec((8, D_MODEL), lambda i, f: (0, 0))],
        out_specs=pl.BlockSpec((tm, D_MODEL), lambda i, f: (i, 0)),
        scratch_shapes=[pltpu.VMEM((tm, D_MODEL), F32)],
        compiler_params=pltpu.CompilerParams(
            dimension_semantics=("parallel", "arbitrary"), vmem_limit_bytes=VMEM_LIMIT),
        name="moe_norm",
    )(x, gate, pick, wg, wu, wd, ln)


TOK_PAD = 8
NEW_PAD = 16


def _softmax_two(s, sn):
    m = jnp.maximum(jnp.max(s, axis=-1, keepdims=True), jnp.max(sn, axis=-1, keepdims=True))
    p = jnp.exp(s - m)
    pn = jnp.exp(sn - m)
    inv = 1.0 / (jnp.sum(p, axis=-1, keepdims=True) + jnp.sum(pn, axis=-1, keepdims=True))
    return p * inv, pn * inv


def _new_row_mask(rows, n_new):
    t = lax.broadcasted_iota(jnp.int32, (rows, NEW_PAD), 0) & (TOK_PAD - 1)
    i = lax.broadcasted_iota(jnp.int32, (rows, NEW_PAD), 1)
    return (i <= t) & (i < n_new)


def _dec_diff_kernel(pt_ref, q_ref, new_ref, par_ref, *rest, n_pages, n_new, out_scale):
    pages = rest[:n_pages]
    o_ref, s_ref = rest[n_pages:]
    nqk = 2 * DIFF_HEADS * HEAD_DIM
    half = DIFF_HEADS * TOK_PAD
    q = q_ref[...]
    for j in range(n_pages):
        s_ref[:, j * PAGE_SIZE:(j + 1) * PAGE_SIZE] = _nt_dot(q, pages[j][:, 0:nqk].astype(BF16))
    sn = _nt_dot(q, new_ref[:, 0:nqk].astype(BF16))
    sn = jnp.where(_new_row_mask(2 * half, n_new), sn, NEG)
    p, pn = _softmax_two(s_ref[...], sn)
    lam = par_ref[0:1, 0:1]
    a = (p[0:half] - lam * p[half:2 * half]).astype(BF16)
    an = (pn[0:half] - lam * pn[half:2 * half]).astype(BF16)
    o = jnp.dot(an, new_ref[:, nqk:].astype(BF16), preferred_element_type=F32)
    for j in range(n_pages):
        o = o + jnp.dot(a[:, j * PAGE_SIZE:(j + 1) * PAGE_SIZE], pages[j][:, nqk:].astype(BF16),
                        preferred_element_type=F32)
    g = par_ref[1:2, :]
    for h in range(DIFF_HEADS):
        oh = o[h * TOK_PAD:(h + 1) * TOK_PAD, h * LANES:(h + 1) * LANES]
        oh = oh * lax.rsqrt(jnp.mean(oh * oh, axis=-1, keepdims=True) + RMS_EPS) * g
        o_ref[:, h * LANES:(h + 1) * LANES] = oh * out_scale


def _page_specs(layer, n_pages, width):
    return [pl.BlockSpec((None, None, PAGE_SIZE, width),
                         lambda b, pt, j=j: (layer, pt[b * n_pages + j], 0, 0)) for j in range(n_pages)]


def _per_seq(shape):
    return pl.BlockSpec((None,) + shape, lambda b, pt: (b,) + (0,) * len(shape))


def _shared(shape):
    return pl.BlockSpec(shape, lambda b, pt: (0,) * len(shape))


def _pad_rows(a, n):
    return jnp.pad(a, ((0, 0), (0, n - a.shape[1])) + ((0, 0),) * (a.ndim - 2))


def decode_diff_attention(layer, page_table, cache, da_q, rows_new, lam, subln_g, lambda_init):
    b, n_new, _ = da_q.shape
    n_pages = page_table.shape[1]
    q = da_q.astype(F32).reshape(b, n_new, DIFF_HEADS, 2, HEAD_DIM)
    q = _pad_rows(jnp.transpose(q, (0, 3, 2, 1, 4)).reshape(b * 2 * DIFF_HEADS, n_new, HEAD_DIM), TOK_PAD)
    q = q.reshape(b, 2, DIFF_HEADS, TOK_PAD, 1, HEAD_DIM)
    head = 2 * jnp.arange(DIFF_HEADS)[None, :] + jnp.arange(2)[:, None]
    place = (head[:, :, None] == jnp.arange(2 * DIFF_HEADS)).astype(F32)
    qbd = (q * place[None, :, :, None, :, None]).reshape(b, 2 * DIFF_HEADS * TOK_PAD, 2 * DIFF_HEADS * HEAD_DIM)
    par = jnp.concatenate([jnp.broadcast_to(lam.astype(F32), (1, LANES)), subln_g.astype(F32).reshape(1, LANES),
                           jnp.zeros((6, LANES), F32)], 0)
    rows = 2 * DIFF_HEADS * TOK_PAD
    width = cache.shape[-1]
    return dict(
        body=functools.partial(_dec_diff_kernel, n_pages=n_pages, n_new=n_new, out_scale=1.0 - lambda_init),
        args=[qbd.astype(BF16), _pad_rows(rows_new, NEW_PAD), par] + [cache] * n_pages,
        in_specs=[_per_seq((rows, 2 * DIFF_HEADS * HEAD_DIM)), _per_seq((NEW_PAD, width)), _shared((8, LANES))]
        + _page_specs(layer, n_pages, width),
        out_shape=jax.ShapeDtypeStruct((b, TOK_PAD, DIFF_HEADS * LANES), F32),
        out_spec=_per_seq((TOK_PAD, DIFF_HEADS * LANES)),
        scratch=[pltpu.VMEM((rows, n_pages * PAGE_SIZE), F32)])


def _dec_moba_kernel(pt_ref, q_ref, new_ref, *rest, n_pages, n_new, past_len):
    pages = rest[:n_pages]
    o_ref, s_ref, km_ref = rest[n_pages:]
    w = MOBA_HEADS * HEAD_DIM
    rows = MOBA_HEADS * TOK_PAD
    ppb = MOBA_BLOCK // PAGE_SIZE
    n_blk = n_pages // ppb
    q = q_ref[...]
    km_ref[...] = jnp.zeros(km_ref.shape, F32)
    for n in range(n_blk):
        tot = jnp.sum(pages[ppb * n][:, 0:w], axis=0, keepdims=True)
        for j in range(ppb * n + 1, ppb * (n + 1)):
            tot = tot + jnp.sum(pages[j][:, 0:w], axis=0, keepdims=True)
        km_ref[n:n + 1, :] = tot / MOBA_BLOCK
    gate = _nt_dot(q, km_ref[...].astype(BF16))
    lane = lax.broadcasted_iota(jnp.int32, gate.shape, 1)
    t = lax.broadcasted_iota(jnp.int32, gate.shape, 0) & (TOK_PAD - 1)
    earlier = (lane < (past_len + t) // MOBA_BLOCK) & (lane < n_blk)
    gate = jnp.where(earlier, gate, -jnp.inf)
    chosen = jnp.where(earlier & _rank_select(gate, n_blk, MOBA_TOPK), 1.0, 0.0)
    for j in range(n_pages):
        sj = _nt_dot(q, pages[j][:, 0:w].astype(BF16))
        n = j // ppb
        s_ref[:, j * PAGE_SIZE:(j + 1) * PAGE_SIZE] = jnp.where(chosen[:, n:n + 1] > 0.5, sj, NEG)
    sn = jnp.where(_new_row_mask(rows, n_new), _nt_dot(q, new_ref[:, 0:w].astype(BF16)), NEG)
    p, pn = _softmax_two(s_ref[...], sn)
    p = p.astype(BF16)
    o = jnp.dot(pn.astype(BF16), new_ref[:, w:].astype(BF16), preferred_element_type=F32)
    for j in range(n_pages):
        o = o + jnp.dot(p[:, j * PAGE_SIZE:(j + 1) * PAGE_SIZE], pages[j][:, w:].astype(BF16),
                        preferred_element_type=F32)
    for h in range(MOBA_HEADS):
        o_ref[:, h * HEAD_DIM:(h + 1) * HEAD_DIM] = o[h * TOK_PAD:(h + 1) * TOK_PAD, h * HEAD_DIM:(h + 1) * HEAD_DIM]


def _head_tiles(a, n_heads):
    b, n_new, _ = a.shape
    a = a.astype(F32).reshape(b, n_new, n_heads, HEAD_DIM)
    return _pad_rows(jnp.transpose(a, (0, 2, 1, 3)).reshape(b * n_heads, n_new, HEAD_DIM), TOK_PAD).reshape(
        b, n_heads, TOK_PAD, HEAD_DIM)


def decode_moba_attention(layer, page_table, cache, m_q, rows_new):
    b, n_new, _ = m_q.shape
    n_pages = page_table.shape[1]
    past_len = n_pages * PAGE_SIZE
    assert past_len % MOBA_BLOCK == 0 and n_new <= TOK_PAD and n_pages * PAGE_SIZE // MOBA_BLOCK <= NEW_PAD
    q = _head_tiles(m_q, MOBA_HEADS)
    place = jnp.eye(MOBA_HEADS, dtype=F32)
    q = (q[:, :, :, None, :] * place[None, :, None, :, None]).reshape(b, MOBA_HEADS * TOK_PAD, MOBA_HEADS * HEAD_DIM)
    rows = MOBA_HEADS * TOK_PAD
    width = cache.shape[-1]
    return dict(
        body=functools.partial(_dec_moba_kernel, n_pages=n_pages, n_new=n_new, past_len=past_len),
        args=[q.astype(BF16), _pad_rows(rows_new, NEW_PAD)] + [cache] * n_pages,
        in_specs=[_per_seq((rows, MOBA_HEADS * HEAD_DIM)), _per_seq((NEW_PAD, width))]
        + _page_specs(layer, n_pages, width),
        out_shape=jax.ShapeDtypeStruct((b, TOK_PAD, MOBA_HEADS * HEAD_DIM), F32),
        out_spec=_per_seq((TOK_PAD, MOBA_HEADS * HEAD_DIM)),
        scratch=[pltpu.VMEM((rows, past_len), F32), pltpu.VMEM((NEW_PAD, MOBA_HEADS * HEAD_DIM), F32)])


def _dec_nsa_kernel(pt_ref, qn_ref, qr_ref, gate_ref, new_ref, wnew_ref, win_ref,
                    wc_ref, cb_ref, w2_ref, ov_ref, ex_ref, *rest, n_pages, n_new, past_len, win_pos0):
    pages = rest[:n_pages]
    o_ref, cmp_ref, slc_ref = rest[n_pages:]
    rows = NSA_HEADS * TOK_PAD
    n_chunk = past_len // CMP_STRIDE
    n_cmp = (past_len + n_new - CMP_LEN) // CMP_STRIDE + 1
    n_slc = -(-(past_len + n_new) // SLC_LEN)
    hid = cb_ref.shape[1] // 2
    for j in range(n_pages):
        cmp_ref[j * PAGE_SIZE:(j + 1) * PAGE_SIZE, :] = pages[j][:, 0:2 * HEAD_DIM]
        slc_ref[j * PAGE_SIZE:(j + 1) * PAGE_SIZE, :] = pages[j][:, 2 * HEAD_DIM:4 * HEAD_DIM]

    y = jnp.zeros((n_chunk, 4 * hid), F32)
    for r in range(CMP_STRIDE):
        xr = cmp_ref[pl.ds(r, n_chunk, stride=CMP_STRIDE), :].astype(BF16)
        y = y + jnp.dot(xr, wc_ref[r], preferred_element_type=F32)
    hk = y[:, 0:hid] + pltpu.roll(y[:, hid:2 * hid], n_chunk - 1, 0) + cb_ref[0:1, 0:hid]
    hv = y[:, 2 * hid:3 * hid] + pltpu.roll(y[:, 3 * hid:4 * hid], n_chunk - 1, 0) + cb_ref[0:1, hid:2 * hid]
    cc = jnp.dot(jax.nn.gelu(jnp.concatenate([hk, hv], axis=1)).astype(BF16), w2_ref[...],
                 preferred_element_type=F32).astype(BF16)

    t = lax.broadcasted_iota(jnp.int32, (rows, n_chunk), 0) & (TOK_PAD - 1)
    n_idx = lax.broadcasted_iota(jnp.int32, (rows, n_chunk), 1)
    ok = (n_idx * CMP_STRIDE + CMP_LEN - 1 <= past_len + t) & (n_idx < n_cmp)
    sc = jnp.where(ok, _nt_dot(qn_ref[...], cc), NEG)
    pc = jnp.exp(sc - jnp.max(sc, axis=-1, keepdims=True))
    pc = pc / jnp.sum(pc, axis=-1, keepdims=True)
    p_hi = pc.astype(BF16)
    p_lo = (pc - p_hi.astype(F32)).astype(BF16)
    o_cmp = jnp.dot(p_hi, cc, preferred_element_type=F32)[:, HEAD_DIM:]
    ov = ov_ref[...]
    imp = jnp.dot(p_hi, ov, preferred_element_type=F32) + jnp.dot(p_lo, ov, preferred_element_type=F32)
    imp = imp[0:TOK_PAD] + imp[TOK_PAD:2 * TOK_PAD] + imp[2 * TOK_PAD:3 * TOK_PAD] + imp[3 * TOK_PAD:4 * TOK_PAD]
    blk = lax.broadcasted_iota(jnp.int32, imp.shape, 1)
    cur = (past_len + lax.broadcasted_iota(jnp.int32, imp.shape, 0)) // SLC_LEN
    valid = blk <= cur
    forced = (blk == 0) | (blk == cur) | (blk == cur - 1)
    score = jnp.where(forced, jnp.inf, jnp.where(valid, imp, -jnp.inf))
    chosen = jnp.where(valid & _rank_select(score, n_slc, min(SLC_TOPN, n_slc)), 1.0, 0.0).astype(BF16)
    key_ok = jnp.dot(chosen, ex_ref[...], preferred_element_type=F32)
    key_ok = jnp.concatenate([key_ok] * NSA_HEADS, axis=0)
    new_mask = _new_row_mask(rows, n_new)

    qr = qr_ref[...]
    kv = slc_ref[...].astype(BF16)
    kv_new = new_ref[:, 2 * HEAD_DIM:4 * HEAD_DIM].astype(BF16)
    p, pn = _softmax_two(jnp.where(key_ok > 0.5, _nt_dot(qr, kv), NEG), jnp.where(new_mask, _nt_dot(qr, kv_new), NEG))
    o_slc = (jnp.dot(p.astype(BF16), kv, preferred_element_type=F32)
             + jnp.dot(pn.astype(BF16), kv_new, preferred_element_type=F32))[:, HEAD_DIM:]

    kv = win_ref[...].astype(BF16)
    kv_new = wnew_ref[...].astype(BF16)
    n_win = kv.shape[0]
    wpos = win_pos0 + lax.broadcasted_iota(jnp.int32, (rows, n_win), 1)
    qpos = past_len + (lax.broadcasted_iota(jnp.int32, (rows, n_win), 0) & (TOK_PAD - 1))
    ok = (wpos <= qpos) & (wpos > qpos - WINDOW)
    p, pn = _softmax_two(jnp.where(ok, _nt_dot(qr, kv), NEG), jnp.where(new_mask, _nt_dot(qr, kv_new), NEG))
    o_win = (jnp.dot(p.astype(BF16), kv, preferred_element_type=F32)
             + jnp.dot(pn.astype(BF16), kv_new, preferred_element_type=F32))[:, HEAD_DIM:]

    g = jax.nn.sigmoid(gate_ref[...])
    o = g[:, 0:1] * o_cmp + g[:, 1:2] * o_slc + g[:, 2:3] * o_win
    for h in range(NSA_HEADS):
        o_ref[:, h * HEAD_DIM:(h + 1) * HEAD_DIM] = o[h * TOK_PAD:(h + 1) * TOK_PAD]


def decode_nsa_attention(layer, page_table, cache, win_state, n_q, nq_r, n_gate, rows_new, rows_win,
                         cmp_pos, cmp_w1, cmp_w2):
    b, n_new, _ = n_q.shape
    n_pages = page_table.shape[1]
    past_len = n_pages * PAGE_SIZE
    n_win = win_state.shape[2]
    assert past_len % SLC_LEN == 0 and n_new < CMP_STRIDE and past_len >= CMP_LEN and n_new <= TOK_PAD
    n_chunk = past_len // CMP_STRIDE
    assert n_chunk == LANES, "compressed tokens are laid out on one lane tile"
    hid = cmp_w1.shape[-1]
    lane_pad = lambda a: jnp.pad(a, ((0, 0),) * (a.ndim - 1) + ((0, LANES - a.shape[-1]),))
    qn = lane_pad(_head_tiles(n_q, NSA_HEADS)).reshape(b, -1, LANES).astype(BF16)
    qr = lane_pad(_head_tiles(nq_r, NSA_HEADS)).reshape(b, -1, LANES).astype(BF16)
    g = jnp.transpose(n_gate[..., :N_GATE].reshape(b, n_new, NSA_HEADS, 3), (0, 2, 1, 3))
    g = lane_pad(_pad_rows(g.reshape(b * NSA_HEADS, n_new, 3), TOK_PAD)).reshape(b, -1, LANES)
    w1 = cmp_w1.reshape(2, 2, CMP_STRIDE, HEAD_DIM, hid)
    zero = jnp.zeros((CMP_STRIDE, HEAD_DIM, 2 * hid), F32)
    top = jnp.concatenate([w1[0, 0], w1[0, 1], zero], axis=-1)
    bot = jnp.concatenate([zero, w1[1, 0], w1[1, 1]], axis=-1)
    wc = jnp.concatenate([top, bot], axis=1).astype(BF16)
    bias = jnp.concatenate([cmp_pos[0].reshape(1, -1) @ cmp_w1[0], cmp_pos[1].reshape(1, -1) @ cmp_w1[1]], -1)
    cb = jnp.concatenate([bias, jnp.zeros((7, 2 * hid), F32)], 0)
    zw = jnp.zeros((hid, HEAD_DIM), F32)
    w2 = jnp.concatenate([jnp.concatenate([cmp_w2[0], zw], 1), jnp.concatenate([zw, cmp_w2[1]], 1)], 0).astype(BF16)
    cmp_start = jnp.arange(n_chunk) * CMP_STRIDE
    slc_start = jnp.arange(LANES) * SLC_LEN
    ov = ((cmp_start[:, None] <= slc_start[None, :] + SLC_LEN - 1)
          & (cmp_start[:, None] + CMP_LEN - 1 >= slc_start[None, :])).astype(BF16)
    ex = (jnp.arange(past_len)[None, :] // SLC_LEN == jnp.arange(LANES)[:, None]).astype(BF16)
    rows = NSA_HEADS * TOK_PAD
    width = cache.shape[-1]
    return dict(
        body=functools.partial(_dec_nsa_kernel, n_pages=n_pages, n_new=n_new, past_len=past_len,
                               win_pos0=past_len - n_win),
        args=[qn, qr, g, _pad_rows(rows_new, NEW_PAD), _pad_rows(rows_win, NEW_PAD), win_state,
              wc, cb, w2, ov, ex] + [cache] * n_pages,
        in_specs=[_per_seq((rows, LANES)), _per_seq((rows, LANES)), _per_seq((rows, LANES)),
                  _per_seq((NEW_PAD, width)), _per_seq((NEW_PAD, 2 * HEAD_DIM)),
                  pl.BlockSpec((None, None, n_win, 2 * HEAD_DIM), lambda bi, pt: (layer, bi, 0, 0)),
                  _shared(wc.shape), _shared(cb.shape), _shared(w2.shape), _shared(ov.shape), _shared(ex.shape)]
        + _page_specs(layer, n_pages, width),
        out_shape=jax.ShapeDtypeStruct((b, TOK_PAD, NSA_HEADS * HEAD_DIM), F32),
        out_spec=_per_seq((TOK_PAD, NSA_HEADS * HEAD_DIM)),
        scratch=[pltpu.VMEM((past_len, 2 * HEAD_DIM), F32), pltpu.VMEM((past_len, 2 * HEAD_DIM), F32)])


def _decode_kernel(pt_ref, *refs, parts):
    pos = 0
    ins = []
    for _, n_in, _ in parts:
        ins.append(refs[pos:pos + n_in])
        pos += n_in
    outs = refs[pos:pos + len(parts)]
    pos += len(parts)
    for (body, _, n_scratch), part_ins, o_ref in zip(parts, ins, outs):
        body(pt_ref, *part_ins, o_ref, *refs[pos:pos + n_scratch])
        pos += n_scratch


def decode_attention(page_table, parts, n_new):
    b = page_table.shape[0]
    outs = pl.pallas_call(
        functools.partial(_decode_kernel, parts=[(p["body"], len(p["args"]), len(p["scratch"])) for p in parts]),
        out_shape=[p["out_shape"] for p in parts],
        grid_spec=pltpu.PrefetchScalarGridSpec(
            num_scalar_prefetch=1, grid=(b,),
            in_specs=[s for p in parts for s in p["in_specs"]],
            out_specs=[p["out_spec"] for p in parts],
            scratch_shapes=[s for p in parts for s in p["scratch"]]),
        compiler_params=pltpu.CompilerParams(dimension_semantics=("arbitrary",), vmem_limit_bytes=VMEM_LIMIT),
        name="decode_attention",
    )(page_table.reshape(-1), *[a for p in parts for a in p["args"]])
    return [o[:, :n_new] for o in outs]


def _compress(kv, pos_emb, w1, w2):
    b, l, _ = kv.shape
    n_chunk = l // CMP_STRIDE
    n_cmp = (l - CMP_LEN) // CMP_STRIDE + 1
    x = kv[:, :n_chunk * CMP_STRIDE].reshape(b * n_chunk, CMP_STRIDE * HEAD_DIM)
    half = CMP_STRIDE * HEAD_DIM
    w_cat = jnp.concatenate([w1[:half], w1[half:]], axis=1).astype(BF16)
    rows = x.shape[0]
    tm = 512 if rows % 512 == 0 else rows
    y = matmul(x, w_cat, tm, w_cat.shape[1]).reshape(b, n_chunk, 2, w1.shape[1])
    bias = pos_emb.reshape(1, -1) @ w1
    hid = y[:, :n_cmp, 0] + y[:, 1:n_cmp + 1, 1] + bias
    return jax.nn.gelu(hid) @ w2


def kernel(x_prompt, x_sample, cache_diff, cache_nsa, cache_moba, state_nsa_win, page_table, w_in, diff_lambda, diff_subln, nsa_cmp_pos, nsa_cmp_w1, nsa_cmp_w2, w_br_diff, w_br_nsa, w_br_moba, w_out, ln1_g, ln1_b, ln2_g, ln2_b, router_w, router_b, exp_w_gate, exp_w_up, exp_w_down, sh_w_gate, sh_w_up, sh_w_down):
    bp, sp, _ = x_prompt.shape
    bs, ss, _ = x_sample.shape
    tp, ts = bp * sp, bs * ss
    n_pages = page_table.shape[1]
    past_len = n_pages * PAGE_SIZE
    pos_p = jnp.arange(sp, dtype=jnp.int32)
    pos_s = past_len + jnp.arange(ss, dtype=jnp.int32)
    pos = jnp.concatenate([jnp.tile(pos_p, bp), jnp.tile(pos_s, bs)])
    inv = ROPE_THETA ** (-jnp.arange(HALF, dtype=F32) / HALF)
    ang = pos.astype(F32)[:, None] * inv[None, :]
    cos, sin = jnp.cos(ang), jnp.sin(ang)

    x = jnp.concatenate([x_prompt.reshape(tp, D_MODEL), x_sample.reshape(ts, D_MODEL)], 0)
    win_p, win_s = [], []
    cache_rows = None
    for l in range(DEPTH):
        lambda_init = 0.8 - 0.6 * math.exp(-0.3 * l)
        pj = project_inputs(x, repack_input_weight(w_in[l]), cos, sin, l, tp, cache_rows)
        cache_rows = pj["cache_rows"]
        rows_win = pj["rows_win"]
        rows_diff_p, rows_nsa_p, rows_moba_p = (a[l] for a in cache_rows[0::2])
        rows_diff_s, rows_nsa_s, rows_moba_s = (a[l].reshape(bs, ss, -1) for a in cache_rows[1::2])

        lp_ = diff_lambda[l].astype(F32)
        lam = jnp.exp(jnp.sum(lp_[0] * lp_[1])) - jnp.exp(jnp.sum(lp_[2] * lp_[3])) + lambda_init

        def pr(a):
            return a.reshape(bp, sp, a.shape[-1])

        o_diff_p = diff_attention(pj["q_diff"], pj["k_diff"], pj["v_diff"], bp, sp, lam, diff_subln[l], lambda_init)
        ck_c = _compress(pr(rows_nsa_p[:, 0:HEAD_DIM]), nsa_cmp_pos[l, 0], nsa_cmp_w1[l, 0], nsa_cmp_w2[l, 0])
        cv_c = _compress(pr(rows_nsa_p[:, HEAD_DIM:2 * HEAD_DIM]), nsa_cmp_pos[l, 1], nsa_cmp_w1[l, 1],
                         nsa_cmp_w2[l, 1])
        o_nsa_p = nsa_attention(pj["q_nsa"], pj["q_nsa_rot"], jnp.concatenate([ck_c, cv_c], -1).astype(BF16),
                                pj["kv_slc"], pj["kv_win"], pj["nsa_gate"], bp, sp)
        k_mean = jnp.mean(rows_moba_p[:, 0:MOBA_HEADS * HEAD_DIM].reshape(bp, sp // MOBA_BLOCK, MOBA_BLOCK, -1),
                          axis=2)
        o_moba_p = moba_attention(pj["q_moba"], pj["k_moba"], pj["v_moba"], k_mean, bp, sp)

        def sm(a):
            return a[tp:].reshape(bs, ss, a.shape[-1])

        o_diff_s, o_nsa_s, o_moba_s = decode_attention(page_table, [
            decode_diff_attention(l, page_table, cache_diff, sm(pj["q_diff"]), rows_diff_s,
                                  lam, diff_subln[l], lambda_init),
            decode_nsa_attention(l, page_table, cache_nsa, state_nsa_win, sm(pj["q_nsa"]),
                                 sm(pj["q_nsa_rot"]), sm(pj["nsa_gate"]), rows_nsa_s, sm(rows_win),
                                 nsa_cmp_pos[l], nsa_cmp_w1[l], nsa_cmp_w2[l]),
            decode_moba_attention(l, page_table, cache_moba, sm(pj["q_moba"]), rows_moba_s)], ss)

        x, expert_gate = merge_project_norm(
            x, (o_diff_p, o_diff_s.reshape(ts, -1)), (o_nsa_p, o_nsa_s.reshape(ts, -1)),
            (o_moba_p, o_moba_s.reshape(ts, -1)),
            pj["merge_gate"], w_br_diff[l].astype(BF16), w_br_nsa[l].astype(BF16),
            w_br_moba[l].astype(BF16), w_out[l].astype(BF16), ln1_g[l], ln1_b[l], router_w[l], router_b[l])

        def with_shared(w_exp, w_sh):
            w_sh = jnp.transpose(w_sh.reshape(D_MODEL, D_SHARED // D_EXPERT, D_EXPERT), (1, 0, 2))
            return jnp.concatenate([w_exp, w_sh], 0).astype(BF16)

        wd = jnp.concatenate([exp_w_down[l].reshape(-1, D_MODEL), sh_w_down[l]], 0).astype(BF16)
        x = moe_norm(x, expert_gate, with_shared(exp_w_gate[l], sh_w_gate[l]), with_shared(exp_w_up[l], sh_w_up[l]),
                     wd, ln2_g[l], ln2_b[l])

        n_keep = min(WINDOW, sp)
        win_p.append(pr(rows_win[:tp])[:, sp - n_keep:])
        full_win = jnp.concatenate([state_nsa_win[l], sm(rows_win)], 1)
        n_keep = min(WINDOW, past_len + ss)
        win_s.append(full_win[:, full_win.shape[1] - n_keep:])

    new_p = [a.reshape(DEPTH, bp, sp, a.shape[-1]) for a in cache_rows[0::2]]
    new_s = [a.reshape(DEPTH, bs, ss, a.shape[-1]) for a in cache_rows[1::2]]
    return (x[:tp].reshape(bp, sp, D_MODEL), x[tp:].reshape(bs, ss, D_MODEL),
            new_p[0], new_s[0], new_p[1], new_s[1], new_p[2], new_s[2], jnp.stack(win_p, 0), jnp.stack(win_s, 0))
```

```python
import functools
import math

import jax
import jax.numpy as jnp
from jax import lax
from jax.experimental import pallas as pl
from jax.experimental.pallas import tpu as pltpu

F32 = jnp.float32
BF16 = jnp.bfloat16

D_MODEL = 1024
DEPTH = 2
PAGE_SIZE = 128
HEAD_DIM = 64
HALF = HEAD_DIM // 2
ATTN_SCALE = HEAD_DIM ** -0.5
ROPE_THETA = 10000.0
DIFF_HEADS = 4
NSA_HEADS = 4
CMP_LEN = 32
CMP_STRIDE = 16
SLC_LEN = 64
SLC_TOPN = 16
WINDOW = 512
MOBA_HEADS = 4
MOBA_BLOCK = 256
MOBA_TOPK = 3
N_EXPERTS = 64
N_EXPERT_GROUPS = 8
TOPK_GROUPS = 4
TOP_K = 6
D_EXPERT = 128
D_SHARED = 256
ROUTED_SCALE = 2.5
LN_EPS = 1e-5
RMS_EPS = 1e-5
DEEPNORM_ALPHA = (2 * DEPTH) ** 0.25

C_DAQ, C_DAK, C_DAV, C_NQ = 0, 512, 1024, 1536
C_CK, C_CV, C_SK, C_SV, C_WK, C_WV, C_NG = 1792, 1856, 1920, 1984, 2048, 2112, 2176
N_GATE = 3 * NSA_HEADS
C_MQ_SRC = C_NG + N_GATE
C_MQ, C_MK, C_MV, C_MG = 2304, 2560, 2816, 3072
N_IN_PAD = C_MG + 3 * D_MODEL

LANES = 128
NEG = -1e30
VMEM_LIMIT = 56 * 1024 * 1024
MOE_EXPERTS_PER_STEP = 11

def _nt_dot(a, b):
    return lax.dot_general(a, b, (((1,), (1,)), ((), ())), preferred_element_type=F32)


def _mm_kernel(x_ref, w_ref, o_ref):
    o_ref[...] = jnp.dot(x_ref[...].astype(BF16), w_ref[...],
                         preferred_element_type=F32).astype(o_ref.dtype)


def matmul(x, w, tm, tn, out_dtype=F32):
    m, k = x.shape
    n = w.shape[1]
    assert m % tm == 0 and n % tn == 0, (x.shape, w.shape, tm, tn)
    return pl.pallas_call(
        _mm_kernel,
        out_shape=jax.ShapeDtypeStruct((m, n), out_dtype),
        grid=(m // tm, n // tn),
        in_specs=[pl.BlockSpec((tm, k), lambda i, j: (i, 0)),
                  pl.BlockSpec((k, tn), lambda i, j: (0, j))],
        out_specs=pl.BlockSpec((tm, tn), lambda i, j: (i, j)),
        compiler_params=pltpu.CompilerParams(
            dimension_semantics=("parallel", "arbitrary"), vmem_limit_bytes=VMEM_LIMIT),
        name="matmul",
    )(x, w)


def _repack_kernel(lo_ref, hi_ref, o_ref, *, first_moved_tile, shift):
    j = pl.program_id(0)
    lane = lax.broadcasted_iota(jnp.int32, lo_ref.shape, 1)
    hi = hi_ref[...]
    kept = jnp.where((j < first_moved_tile - 1) | (lane < LANES - shift), hi, 0.0)
    moved = jnp.where(lane < shift, pltpu.roll(lo_ref[...], shift, 1), pltpu.roll(hi, shift, 1))
    o_ref[...] = jnp.where(j < first_moved_tile, kept, moved).astype(o_ref.dtype)


def repack_input_weight(w):
    d, n_src = w.shape
    shift = C_MQ - C_MQ_SRC
    last = (n_src - 1) // LANES
    assert C_MQ % LANES == 0 and 0 < shift < LANES and n_src + shift == N_IN_PAD
    return pl.pallas_call(
        functools.partial(_repack_kernel, first_moved_tile=C_MQ // LANES, shift=shift),
        out_shape=jax.ShapeDtypeStruct((d, N_IN_PAD), BF16),
        grid=(N_IN_PAD // LANES,),
        in_specs=[pl.BlockSpec((d, LANES), lambda j: (0, jnp.maximum(j - 1, 0))),
                  pl.BlockSpec((d, LANES), lambda j: (0, jnp.minimum(j, last)))],
        out_specs=pl.BlockSpec((d, LANES), lambda j: (0, j)),
        compiler_params=pltpu.CompilerParams(dimension_semantics=("parallel",), vmem_limit_bytes=VMEM_LIMIT),
        name="repack_input_weight",
    )(w, w)


def _project_kernel(x_ref, w_ref, cos_ref, sa_ref, sb_ref, *refs, n_prompt_tiles, n_carry):
    (rdp_ref, rds_ref, rnp_ref, rns_ref, rmp_ref, rms_ref, rw_ref, mg_ref, ng_ref,
     qd_ref, kd_ref, vd_ref, nq_ref, nqr_ref, ss_ref, ww_ref, mq_ref, mk_ref, mv_ref) = refs[n_carry:]
    tile = pl.program_id(0)

    def put_rows(prompt_ref, sample_ref, lo, hi, val):
        @pl.when(tile < n_prompt_tiles)
        def _():
            prompt_ref[:, lo:hi] = val

        @pl.when(tile >= n_prompt_tiles)
        def _():
            sample_ref[:, lo:hi] = val

    xb = x_ref[...].astype(BF16)
    cos, sa, sb = cos_ref[...], sa_ref[...], sb_ref[...]

    def seg(a, b):
        return jnp.dot(xb, w_ref[:, a:b], preferred_element_type=F32)

    def rope(y):
        tiles = []
        for t in range(y.shape[1] // LANES):
            yt = y[:, t * LANES:(t + 1) * LANES]
            tiles.append(yt * cos + pltpu.roll(yt, LANES - HALF, 1) * sa + pltpu.roll(yt, HALF, 1) * sb)
        return tiles[0] if len(tiles) == 1 else jnp.concatenate(tiles, axis=1)

    first = lax.broadcasted_iota(jnp.int32, (x_ref.shape[0], LANES), 1) < HEAD_DIM
    qd_ref[...] = (rope(seg(C_DAQ, C_DAK)) * ATTN_SCALE).astype(BF16)
    k = rope(seg(C_DAK, C_DAV))
    v = seg(C_DAV, C_NQ)
    put_rows(rdp_ref, rds_ref, 0, C_DAV - C_DAK, k)
    put_rows(rdp_ref, rds_ref, C_DAV - C_DAK, C_NQ - C_DAK, v)
    kd_ref[...] = k.astype(BF16)
    vd_ref[...] = v.astype(BF16)
    nq = seg(C_NQ, C_CK)
    nq_ref[...] = (nq * ATTN_SCALE).astype(BF16)
    nqr_ref[...] = (rope(nq) * ATTN_SCALE).astype(BF16)
    y = seg(C_CK, C_NG)
    ss = jnp.where(first, rope(y[:, LANES:2 * LANES]), y[:, LANES:2 * LANES])
    ww = jnp.where(first, rope(y[:, 2 * LANES:3 * LANES]), y[:, 2 * LANES:3 * LANES])
    put_rows(rnp_ref, rns_ref, 0, LANES, y[:, 0:LANES])
    put_rows(rnp_ref, rns_ref, LANES, 2 * LANES, ss)
    rw_ref[...] = ww
    ss_ref[...] = ss.astype(BF16)
    ww_ref[...] = ww.astype(BF16)
    ng_ref[...] = seg(C_NG, C_MQ)
    mq_ref[...] = (rope(seg(C_MQ, C_MK)) * ATTN_SCALE).astype(BF16)
    k = rope(seg(C_MK, C_MV))
    v = seg(C_MV, C_MG)
    put_rows(rmp_ref, rms_ref, 0, C_MV - C_MK, k)
    put_rows(rmp_ref, rms_ref, C_MV - C_MK, C_MG - C_MK, v)
    mk_ref[...] = k.astype(BF16)
    mv_ref[...] = v.astype(BF16)
    for t in range(3):
        mg_ref[:, t * D_MODEL:(t + 1) * D_MODEL] = seg(C_MG + t * D_MODEL, C_MG + (t + 1) * D_MODEL)


def project_inputs(x, w_pad, cos, sin, layer, n_prompt, cache_rows=None, tm=256):
    t = x.shape[0]
    n_sample = t - n_prompt
    assert n_prompt % tm == 0 and n_sample % tm == 0
    npt = n_prompt // tm
    zero = jnp.zeros_like(sin)
    cos_t = jnp.tile(cos, (1, LANES // HALF))
    sa = jnp.concatenate([-sin, zero, -sin, zero], axis=1)
    sb = jnp.concatenate([zero, sin, zero, sin], axis=1)
    row_widths = (1024, 256, 512)
    names_f32 = (("rows_win", 128), ("merge_gate", 3 * D_MODEL), ("nsa_gate", LANES))
    names_bf16 = (("q_diff", 512), ("k_diff", 512), ("v_diff", 512), ("q_nsa", 256), ("q_nsa_rot", 256),
                  ("kv_slc", 128), ("kv_win", 128), ("q_moba", 256), ("k_moba", 256), ("v_moba", 256))
    row = lambda i: (i, 0)
    rows_shapes, rows_specs = [], []
    for n in row_widths:
        rows_shapes += [jax.ShapeDtypeStruct((DEPTH, n_prompt, n), F32), jax.ShapeDtypeStruct((DEPTH, n_sample, n), F32)]
        rows_specs += [pl.BlockSpec((None, tm, n), lambda i: (layer, jnp.minimum(i, npt - 1), 0)),
                       pl.BlockSpec((None, tm, n), lambda i: (layer, jnp.maximum(i - npt, 0), 0))]
    carry = list(cache_rows) if cache_rows is not None else []
    n_fixed = 5
    outs = pl.pallas_call(
        functools.partial(_project_kernel, n_prompt_tiles=npt, n_carry=len(carry)),
        out_shape=rows_shapes + [jax.ShapeDtypeStruct((t, n), F32) for _, n in names_f32]
        + [jax.ShapeDtypeStruct((t, n), BF16) for _, n in names_bf16],
        grid=(t // tm,),
        in_specs=[pl.BlockSpec((tm, D_MODEL), row),
                  pl.BlockSpec(w_pad.shape, lambda i: (0, 0), pipeline_mode=pl.Buffered(1)),
                  pl.BlockSpec((tm, LANES), row), pl.BlockSpec((tm, LANES), row), pl.BlockSpec((tm, LANES), row)]
        + [pl.BlockSpec(memory_space=pl.ANY)] * len(carry),
        out_specs=rows_specs + [pl.BlockSpec((tm, n), row) for _, n in names_f32 + names_bf16],
        input_output_aliases={n_fixed + k: k for k in range(len(carry))},
        compiler_params=pltpu.CompilerParams(dimension_semantics=("arbitrary",), vmem_limit_bytes=VMEM_LIMIT),
        name="project_inputs",
    )(x, w_pad, cos_t, sa, sb, *carry)
    res = dict(zip([n for n, _ in names_f32 + names_bf16], outs[len(rows_shapes):]))
    res["cache_rows"] = tuple(outs[:len(rows_shapes)])
    return res


def _online_update(s, v, m_ref, l_ref, acc_ref):
    tk = s.shape[1]
    dv = acc_ref.shape[-1]
    m_prev = m_ref[...]
    m_next = jnp.maximum(m_prev, jnp.max(s, axis=-1, keepdims=True))
    alpha = jnp.exp(m_prev - m_next)
    p = jnp.exp(s - jnp.concatenate([m_next] * (tk // LANES), axis=1))
    l_ref[...] = alpha * l_ref[...] + jnp.sum(p, axis=-1, keepdims=True)
    m_ref[...] = m_next
    acc_ref[...] = acc_ref[...] * alpha[:, :dv] + jnp.dot(p.astype(BF16), v, preferred_element_type=F32)


def _init_state(*refs):
    for m_ref, l_ref, acc_ref in zip(refs[0::3], refs[1::3], refs[2::3]):
        m_ref[...] = jnp.full(m_ref.shape, NEG, F32)
        l_ref[...] = jnp.zeros(l_ref.shape, F32)
        acc_ref[...] = jnp.zeros(acc_ref.shape, F32)


def _normalised(l_ref, acc_ref):
    dv = acc_ref.shape[-1]
    return acc_ref[...] / jnp.maximum(l_ref[...], 1e-30)[:, :dv]


def _diff_attn_kernel(q_ref, k_ref, v_ref, par_ref, o_ref,
                      m0, l0, a0, m1, l1, a1, *, tq, tk, out_scale):
    qi = pl.program_id(2)
    q = q_ref[...]
    lane = lax.broadcasted_iota(jnp.int32, q.shape, 1)
    zero = jnp.zeros_like(q)
    qa = jnp.where(lane < HEAD_DIM, q, zero)
    qb = jnp.where(lane >= HEAD_DIM, q, zero)
    q0 = qi * tq
    row = q0 + lax.broadcasted_iota(jnp.int32, (tq, tk), 0)
    col = lax.broadcasted_iota(jnp.int32, (tq, tk), 1)
    _init_state(m0, l0, a0, m1, l1, a1)

    def step(j, causal):
        start = pl.multiple_of(j * tk, tk)
        k = k_ref[pl.ds(start, tk), :]
        v = v_ref[pl.ds(start, tk), :]
        sa, sb = _nt_dot(qa, k), _nt_dot(qb, k)
        if causal:
            mask = (col + start) <= row
            sa, sb = jnp.where(mask, sa, NEG), jnp.where(mask, sb, NEG)
        _online_update(sa, v, m0, l0, a0)
        _online_update(sb, v, m1, l1, a1)

    n_full = (q0 + 1) // tk
    lax.fori_loop(0, n_full, lambda j, c: (step(j, False), c)[1], 0)
    lax.fori_loop(n_full, (q0 + tq + tk - 1) // tk, lambda j, c: (step(j, True), c)[1], 0)
    lam = par_ref[0:1, :]
    g = par_ref[1:2, :]
    o = _normalised(l0, a0) - lam * _normalised(l1, a1)
    o = o * lax.rsqrt(jnp.mean(o * o, axis=-1, keepdims=True) + RMS_EPS) * g
    o_ref[...] = (o * out_scale).astype(o_ref.dtype)


def diff_attention(q, k, v, n_batch, s, lam, subln_g, lambda_init, tq=512, tk=512):
    nq = s // tq
    par = jnp.concatenate([jnp.broadcast_to(lam.astype(F32), (1, LANES)),
                           subln_g.astype(F32).reshape(1, LANES),
                           jnp.zeros((6, LANES), F32)], 0)
    kern = functools.partial(_diff_attn_kernel, tq=tq, tk=tk, out_scale=1.0 - lambda_init)
    st = [pltpu.VMEM((tq, LANES), F32)] * 6
    return pl.pallas_call(
        kern,
        out_shape=jax.ShapeDtypeStruct((n_batch * s, DIFF_HEADS * LANES), F32),
        grid=(n_batch, DIFF_HEADS, nq),
        in_specs=[pl.BlockSpec((tq, LANES), lambda bi, h, i: (bi * nq + i, h)),
                  pl.BlockSpec((s, LANES), lambda bi, h, i: (bi, h)),
                  pl.BlockSpec((s, LANES), lambda bi, h, i: (bi, h)),
                  pl.BlockSpec((8, LANES), lambda bi, h, i: (0, 0))],
        out_specs=pl.BlockSpec((tq, LANES), lambda bi, h, i: (bi * nq + i, h)),
        scratch_shapes=st,
        compiler_params=pltpu.CompilerParams(
            dimension_semantics=("parallel", "parallel", "arbitrary"), vmem_limit_bytes=VMEM_LIMIT),
        name="diff_attention",
    )(q, k, v, par)


def _rank_select(score, n_candidates, top_n):
    lane = lax.broadcasted_iota(jnp.int32, score.shape, 1)
    rank = jnp.zeros(score.shape, F32)
    for c in range(n_candidates):
        col = score[:, c:c + 1]
        ahead = (col > score) | ((col == score) & (c < lane))
        rank = rank + jnp.where(ahead, 1.0, 0.0)
    return rank < top_n


def _rank_select_rows(score, n_candidates, top_n):
    idx = lax.broadcasted_iota(jnp.int32, score.shape, 0)
    rank = jnp.zeros(score.shape, F32)
    for c in range(n_candidates):
        cand = score[c:c + 1]
        ahead = (cand > score) | ((cand == score) & (c < idx))
        rank = rank + jnp.where(ahead, 1.0, 0.0)
    return rank < top_n


def _rows_to_lanes(x_t):
    n, q = x_t.shape
    if n < LANES:
        x_t = jnp.concatenate([x_t, jnp.zeros((LANES - n, q), F32)], axis=0)
    return jnp.concatenate([x_t[:, c * LANES:(c + 1) * LANES].T for c in range(q // LANES)], axis=0)


def _stack_heads(q_ref, tq):
    first = lax.broadcasted_iota(jnp.int32, (tq, LANES), 1) < HEAD_DIM
    tiles = []
    for t in range(NSA_HEADS // 2):
        pair = q_ref[:, t * LANES:(t + 1) * LANES].astype(F32)
        tiles.append(jnp.where(first, pair, 0.0))
        tiles.append(jnp.where(first, pltpu.roll(pair, HEAD_DIM, 1), 0.0))
    return jnp.concatenate(tiles, axis=0).astype(BF16)


def _nsa_attn_kernel(qn_ref, qr_ref, cc_ref, ss_ref, ww_ref, gate_ref, ov_ref, e_ref, o_ref, m, l, acc,
                     *, tq, tk, tkw, n_cmp, n_slc):
    qi = pl.program_id(1)
    q0 = qi * tq
    nh = NSA_HEADS
    r = nh * tq
    qn = _stack_heads(qn_ref, tq)
    qr = _stack_heads(qr_ref, tq)

    cc = cc_ref[...]
    n_pad = cc.shape[0]
    qpos = q0 + lax.broadcasted_iota(jnp.int32, (tq, n_pad), 0)
    qpos = jnp.concatenate([qpos] * nh, axis=0)
    n_idx = lax.broadcasted_iota(jnp.int32, (r, n_pad), 1)
    ok = (n_idx * CMP_STRIDE + CMP_LEN - 1 <= qpos) & (n_idx < n_cmp)
    sc = jnp.where(ok, _nt_dot(qn, cc), NEG)
    pc = jnp.where(ok, jnp.exp(sc - jnp.max(sc, axis=-1, keepdims=True)), 0.0)
    pc = pc / jnp.maximum(jnp.sum(pc, axis=-1, keepdims=True), 1e-30)
    p_hi = pc.astype(BF16)
    p_lo = (pc - p_hi.astype(F32)).astype(BF16)
    o_cmp = jnp.dot(p_hi, cc, preferred_element_type=F32)
    ovt = ov_ref[...]
    imp = _nt_dot(ovt, p_hi) + _nt_dot(ovt, p_lo)
    imp = imp[:, 0:tq] + imp[:, tq:2 * tq] + imp[:, 2 * tq:3 * tq] + imp[:, 3 * tq:4 * tq]
    imp = imp[0:-(-n_slc // 8) * 8]
    blk = lax.broadcasted_iota(jnp.int32, imp.shape, 0)
    cur = (q0 + lax.broadcasted_iota(jnp.int32, imp.shape, 1)) // SLC_LEN
    valid = blk <= cur
    forced = (blk == 0) | (blk == cur) | (blk == cur - 1)
    score = jnp.where(forced, jnp.inf, jnp.where(valid, imp, -jnp.inf))
    chosen = jnp.where(valid & _rank_select_rows(score, n_slc, min(SLC_TOPN, n_slc)), 1.0, 0.0)
    chosen = _rows_to_lanes(chosen).astype(BF16)
    chosen = jnp.concatenate([chosen] * nh, axis=0)

    def sweep(kv_ref, tile, lo, hi, mask_fn):
        _init_state(m, l, acc)

        def body(j, c):
            start = pl.multiple_of(j * tile, tile)
            kv = kv_ref[pl.ds(start, tile), :]
            _online_update(jnp.where(mask_fn(j, start), _nt_dot(qr, kv), NEG), kv, m, l, acc)
            return c

        lax.fori_loop(lo, hi, body, 0)
        return _normalised(l, acc)

    row = jnp.concatenate([q0 + lax.broadcasted_iota(jnp.int32, (tq, tk), 0)] * nh, axis=0)
    col = lax.broadcasted_iota(jnp.int32, (r, tk), 1)
    o_slc = sweep(ss_ref, tk, 0, (q0 + tq + tk - 1) // tk,
                  lambda j, start: ((col + start) <= row)
                  & (jnp.dot(chosen, e_ref[j], preferred_element_type=F32) > 0.5))
    row_w = jnp.concatenate([q0 + lax.broadcasted_iota(jnp.int32, (tq, tkw), 0)] * nh, axis=0)
    col_w = lax.broadcasted_iota(jnp.int32, (r, tkw), 1)
    o_win = sweep(ww_ref, tkw, jnp.maximum(q0 - WINDOW + 1, 0) // tkw, (q0 + tq + tkw - 1) // tkw,
                  lambda j, start: ((col_w + start) <= row_w) & ((col_w + start) > row_w - WINDOW))

    g = jax.nn.sigmoid(gate_ref[...])
    first = lax.broadcasted_iota(jnp.int32, (tq, LANES), 1) < HEAD_DIM
    mixed = []
    for h in range(nh):
        rows = slice(h * tq, (h + 1) * tq)
        mixed.append(g[:, 3 * h:3 * h + 1] * o_cmp[rows] + g[:, 3 * h + 1:3 * h + 2] * o_slc[rows]
                     + g[:, 3 * h + 2:3 * h + 3] * o_win[rows])
    for t in range(nh // 2):
        o_ref[:, t * LANES:(t + 1) * LANES] = jnp.where(first, pltpu.roll(mixed[2 * t], HEAD_DIM, 1), mixed[2 * t + 1])


def _block_expander(n_chunks, n_rows, tk, block_len):
    key = jnp.arange(n_chunks)[:, None, None] * tk + jnp.arange(tk)[None, None, :]
    return (key // block_len == jnp.arange(n_rows)[None, :, None]).astype(BF16)


def nsa_attention(qn, qr, cc, ss, ww, gate, n_batch, s, tq=256, tk=512, tkw=256):
    nq = s // tq
    n_cmp = cc.shape[1]
    n_pad = -(-n_cmp // LANES) * LANES
    cc = jnp.pad(cc, ((0, 0), (0, n_pad - n_cmp), (0, 0)))
    n_slc = s // SLC_LEN
    assert n_slc <= LANES
    cmp_start = jnp.arange(n_pad) * CMP_STRIDE
    slc_start = jnp.arange(LANES) * SLC_LEN
    ov = ((cmp_start[None, :] <= slc_start[:, None] + SLC_LEN - 1)
          & (cmp_start[None, :] + CMP_LEN - 1 >= slc_start[:, None])).astype(BF16)
    expander = _block_expander(s // tk, LANES, tk, SLC_LEN)
    r = NSA_HEADS * tq
    row = lambda bi, i: (bi * nq + i, 0)
    return pl.pallas_call(
        functools.partial(_nsa_attn_kernel, tq=tq, tk=tk, tkw=tkw, n_cmp=n_cmp, n_slc=n_slc),
        out_shape=jax.ShapeDtypeStruct((n_batch * s, NSA_HEADS * HEAD_DIM), F32),
        grid=(n_batch, nq),
        in_specs=[pl.BlockSpec((tq, NSA_HEADS * HEAD_DIM), row),
                  pl.BlockSpec((tq, NSA_HEADS * HEAD_DIM), row),
                  pl.BlockSpec((None, n_pad, LANES), lambda bi, i: (bi, 0, 0)),
                  pl.BlockSpec((s, LANES), lambda bi, i: (bi, 0)),
                  pl.BlockSpec((s, LANES), lambda bi, i: (bi, 0)),
                  pl.BlockSpec((tq, LANES), row),
                  pl.BlockSpec(ov.shape, lambda bi, i: (0, 0)),
                  pl.BlockSpec(expander.shape, lambda bi, i: (0, 0, 0))],
        out_specs=pl.BlockSpec((tq, NSA_HEADS * HEAD_DIM), row),
        scratch_shapes=[pltpu.VMEM((r, LANES), F32), pltpu.VMEM((r, LANES), F32), pltpu.VMEM((r, LANES), F32)],
        compiler_params=pltpu.CompilerParams(
            dimension_semantics=("parallel", "arbitrary"), vmem_limit_bytes=VMEM_LIMIT),
        name="nsa_attention",
    )(qn, qr, cc, ss, ww, gate, ov, expander)


def _moba_attn_kernel(q_ref, k_ref, v_ref, km_ref, e_ref, o_ref, m0, l0, a0, m1, l1, a1, *, tq, tk, n_blk):
    qi = pl.program_id(2)
    q = q_ref[...]
    lane = lax.broadcasted_iota(jnp.int32, q.shape, 1)
    zero = jnp.zeros_like(q)
    qa = jnp.where(lane < HEAD_DIM, q, zero)
    qb = jnp.where(lane >= HEAD_DIM, q, zero)
    q0 = qi * tq
    row = q0 + lax.broadcasted_iota(jnp.int32, (tq, tk), 0)
    col = lax.broadcasted_iota(jnp.int32, (tq, tk), 1)

    km = km_ref[...]
    blk = lax.broadcasted_iota(jnp.int32, (km.shape[0], tq), 0)
    cur = (q0 + lax.broadcasted_iota(jnp.int32, (km.shape[0], tq), 1)) // MOBA_BLOCK
    earlier = blk < cur

    def block_mask(qh):
        gate = jnp.where(earlier, _nt_dot(km, qh), -jnp.inf)
        chosen = (earlier & _rank_select_rows(gate, n_blk, min(MOBA_TOPK, n_blk))) | (blk == cur)
        return _rows_to_lanes(jnp.where(chosen, 1.0, 0.0)).astype(BF16)

    bm0 = block_mask(qa)
    bm1 = block_mask(qb)
    _init_state(m0, l0, a0, m1, l1, a1)

    def body(j, carry):
        start = pl.multiple_of(j * tk, tk)
        k = k_ref[pl.ds(start, tk), :]
        v = v_ref[pl.ds(start, tk), :]
        causal = (col + start) <= row
        e = e_ref[j]
        mask0 = causal & (jnp.dot(bm0, e, preferred_element_type=F32) > 0.5)
        mask1 = causal & (jnp.dot(bm1, e, preferred_element_type=F32) > 0.5)
        _online_update(jnp.where(mask0, _nt_dot(qa, k), NEG), v, m0, l0, a0)
        _online_update(jnp.where(mask1, _nt_dot(qb, k), NEG), v, m1, l1, a1)
        return carry

    lax.fori_loop(0, (q0 + tq + tk - 1) // tk, body, 0)
    o_ref[...] = jnp.where(lane < HEAD_DIM, _normalised(l0, a0), _normalised(l1, a1)).astype(o_ref.dtype)


def moba_attention(q, k, v, k_mean, n_batch, s, tq=512, tk=512):
    nq = s // tq
    n_blk = k_mean.shape[1]
    n_pad = -(-n_blk // 16) * 16
    km = jnp.pad(k_mean, ((0, 0), (0, n_pad - n_blk), (0, 0))).astype(BF16)
    expander = _block_expander(s // tk, LANES, tk, MOBA_BLOCK)
    st = [pltpu.VMEM((tq, LANES), F32)] * 6
    return pl.pallas_call(
        functools.partial(_moba_attn_kernel, tq=tq, tk=tk, n_blk=n_blk),
        out_shape=jax.ShapeDtypeStruct((n_batch * s, MOBA_HEADS * HEAD_DIM), F32),
        grid=(n_batch, MOBA_HEADS // 2, nq),
        in_specs=[pl.BlockSpec((tq, LANES), lambda bi, h, i: (bi * nq + i, h)),
                  pl.BlockSpec((s, LANES), lambda bi, h, i: (bi, h)),
                  pl.BlockSpec((s, LANES), lambda bi, h, i: (bi, h)),
                  pl.BlockSpec((None, n_pad, LANES), lambda bi, h, i: (bi, 0, h)),
                  pl.BlockSpec(expander.shape, lambda bi, h, i: (0, 0, 0))],
        out_specs=pl.BlockSpec((tq, LANES), lambda bi, h, i: (bi * nq + i, h)),
        scratch_shapes=st,
        compiler_params=pltpu.CompilerParams(
            dimension_semantics=("parallel", "parallel", "arbitrary"), vmem_limit_bytes=VMEM_LIMIT),
        name="moba_attention",
    )(q, k, v, km, expander)


def _layer_norm(z, g, b):
    mu = jnp.mean(z, axis=-1, keepdims=True)
    zc = z - mu
    var = jnp.mean(zc * zc, axis=-1, keepdims=True)
    return zc * lax.rsqrt(var + LN_EPS) * g + b


def _split_bf16(a):
    hi = a.astype(BF16)
    return hi, (a - hi.astype(F32)).astype(BF16)


def _route_experts(xn, rwh_ref, rwl_ref, rb_ref):
    tm = xn.shape[0]
    group = N_EXPERTS // N_EXPERT_GROUPS
    xh, xl = _split_bf16(xn)
    rwh = rwh_ref[...]
    logits = _nt_dot(rwh, xh) + _nt_dot(rwh, xl) + _nt_dot(rwl_ref[...], xh)
    s = jax.nn.sigmoid(logits)
    sb = s + jnp.concatenate([rb_ref[...]] * (tm // LANES), axis=1)
    grp = sb.reshape(N_EXPERT_GROUPS, group, tm)
    member = lax.broadcasted_iota(jnp.int32, grp.shape, 1).astype(F32)
    m1 = jnp.max(grp, axis=1, keepdims=True)
    first = jnp.min(jnp.where(grp == m1, member, float(group)), axis=1, keepdims=True)
    m2 = jnp.max(jnp.where(member == first, -jnp.inf, grp), axis=1, keepdims=True)
    group_ok = _rank_select_rows(m1 + m2, N_EXPERT_GROUPS, TOPK_GROUPS)
    cand = jnp.where(group_ok, grp, -jnp.inf).reshape(N_EXPERTS, tm)
    w = jnp.where(_rank_select_rows(cand, N_EXPERTS, TOP_K), s, 0.0)
    w = w / jnp.sum(w, axis=0, keepdims=True) * ROUTED_SCALE
    gate = _rows_to_lanes(w)
    hi = gate.astype(BF16).astype(F32)
    lane = lax.broadcasted_iota(jnp.int32, gate.shape, 1)
    return jnp.where(lane < N_EXPERTS, hi, pltpu.roll(gate - hi, N_EXPERTS, 1)).astype(BF16)


def _merge_kernel(x_ref, odp_ref, ods_ref, onp_ref, ons_ref, omp_ref, oms_ref, g0_ref, g1_ref, g2_ref,
                  wd_ref, wn_ref, wm_ref, wo_ref, ln_ref, rwh_ref, rwl_ref, rb_ref, o_ref, gate_ref,
                  *, n_prompt_tiles):
    from_prompt = pl.program_id(0) < n_prompt_tiles

    def branch(p_r, s_r, w_r, g_r):
        o = jnp.where(from_prompt, p_r[...], s_r[...])
        y = jnp.dot(o.astype(BF16), w_r[...], preferred_element_type=F32)
        return jax.nn.sigmoid(g_r[...]) * y

    merged = (branch(odp_ref, ods_ref, wd_ref, g0_ref) + branch(onp_ref, ons_ref, wn_ref, g1_ref)
              + branch(omp_ref, oms_ref, wm_ref, g2_ref))
    y = jnp.dot(merged.astype(BF16), wo_ref[...], preferred_element_type=F32)
    z = DEEPNORM_ALPHA * x_ref[...] + y
    xn = _layer_norm(z, ln_ref[0:1, :], ln_ref[1:2, :])
    o_ref[...] = xn
    gate_ref[...] = _route_experts(xn, rwh_ref, rwl_ref, rb_ref)


def merge_project_norm(x, o_diff, o_nsa, o_moba, merge_gate, w_d, w_n, w_m, w_o, ln_g, ln_b,
                       router_w, router_b, tm=256):
    assert 2 * N_EXPERTS == LANES
    t = x.shape[0]
    tp, ts = o_diff[0].shape[0], o_diff[1].shape[0]
    assert tp % tm == 0 and ts % tm == 0 and tp + ts == t
    npt = tp // tm
    ln = jnp.concatenate([ln_g.reshape(1, -1), ln_b.reshape(1, -1), jnp.zeros((6, D_MODEL), F32)], 0)
    rwh, rwl = _split_bf16(router_w.astype(F32).T)
    rb = jnp.broadcast_to(router_b.astype(F32)[:, None], (N_EXPERTS, LANES))
    row = lambda i: (i, 0)
    full = lambda i: (0, 0)
    pair_specs, pair_args = [], []
    for o_p, o_s in (o_diff, o_nsa, o_moba):
        pair_specs += [pl.BlockSpec((tm, o_p.shape[1]), lambda i: (jnp.minimum(i, npt - 1), 0)),
                       pl.BlockSpec((tm, o_s.shape[1]), lambda i: (jnp.maximum(i - npt, 0), 0))]
        pair_args += [o_p, o_s]
    return pl.pallas_call(
        functools.partial(_merge_kernel, n_prompt_tiles=npt),
        out_shape=[jax.ShapeDtypeStruct((t, D_MODEL), F32), jax.ShapeDtypeStruct((t, LANES), BF16)],
        grid=(t // tm,),
        in_specs=[pl.BlockSpec((tm, D_MODEL), row)] + pair_specs + [
                  pl.BlockSpec((tm, D_MODEL), lambda i: (i, 0)),
                  pl.BlockSpec((tm, D_MODEL), lambda i: (i, 1)),
                  pl.BlockSpec((tm, D_MODEL), lambda i: (i, 2)),
                  pl.BlockSpec(w_d.shape, full), pl.BlockSpec(w_n.shape, full),
                  pl.BlockSpec(w_m.shape, full), pl.BlockSpec(w_o.shape, full),
                  pl.BlockSpec((8, D_MODEL), full),
                  pl.BlockSpec(rwh.shape, full), pl.BlockSpec(rwl.shape, full), pl.BlockSpec(rb.shape, full)],
        out_specs=[pl.BlockSpec((tm, D_MODEL), row), pl.BlockSpec((tm, LANES), row)],
        compiler_params=pltpu.CompilerParams(
            dimension_semantics=("parallel",), vmem_limit_bytes=VMEM_LIMIT),
        name="merge_project_norm",
    )(x, *pair_args, merge_gate, merge_gate, merge_gate, w_d, w_n, w_m, w_o, ln, rwh, rwl, rb)


def _moe_kernel(x_ref, gate_ref, pick_ref, wg_ref, wu_ref, wd_ref, ln_ref, o_ref, acc_ref):
    f = pl.program_id(1)

    @pl.when(f == 0)
    def _():
        acc_ref[...] = jnp.zeros(acc_ref.shape, F32)

    n_e = wg_ref.shape[0]
    wg = jnp.concatenate([wg_ref[j] for j in range(n_e)], axis=1)
    wu = jnp.concatenate([wu_ref[j] for j in range(n_e)], axis=1)
    xb = x_ref[...].astype(BF16)
    hg = jnp.dot(xb, wg, preferred_element_type=F32)
    hu = jnp.dot(xb, wu, preferred_element_type=F32)
    w = jnp.dot(gate_ref[...], pick_ref[...], preferred_element_type=F32)
    lane = lax.broadcasted_iota(jnp.int32, w.shape, 1)
    w = jnp.where((f * n_e + lane >= N_EXPERTS) & (lane < n_e), 1.0, w)
    h = jax.nn.silu(hg) * hu
    h = jnp.concatenate([h[:, j * D_EXPERT:(j + 1) * D_EXPERT] * w[:, j:j + 1] for j in range(n_e)], axis=1)
    acc_ref[...] += jnp.dot(h.astype(BF16), wd_ref[...], preferred_element_type=F32)

    @pl.when(f == pl.num_programs(1) - 1)
    def _():
        z = DEEPNORM_ALPHA * x_ref[...] + acc_ref[...]
        o_ref[...] = _layer_norm(z, ln_ref[0:1, :], ln_ref[1:2, :])


def moe_norm(x, gate, wg, wu, wd, ln_g, ln_b, tm=768, experts_per_step=MOE_EXPERTS_PER_STEP):
    t = x.shape[0]
    f_tot = wd.shape[0]
    tf = experts_per_step * D_EXPERT
    n_col = gate.shape[1]
    n_steps = wg.shape[0] // experts_per_step
    assert wg.shape[0] % experts_per_step == 0 and t % tm == 0
    expert = jnp.arange(n_steps)[:, None, None] * experts_per_step + jnp.arange(LANES)[None, None, :]
    src = jnp.arange(n_col)[None, :, None] % N_EXPERTS
    pick = ((src == expert) & (jnp.arange(LANES)[None, None, :] < experts_per_step)).astype(BF16)
    ln = jnp.concatenate([ln_g.reshape(1, -1), ln_b.reshape(1, -1), jnp.zeros((6, D_MODEL), F32)], 0)
    return pl.pallas_call(
        _moe_kernel,
        out_shape=jax.ShapeDtypeStruct((t, D_MODEL), F32),
        grid=(t // tm, f_tot // tf),
        in_specs=[pl.BlockSpec((tm, D_MODEL), lambda i, f: (i, 0)),
                  pl.BlockSpec((tm, n_col), lambda i, f: (i, 0)),
                  pl.BlockSpec((None, n_col, LANES), lambda i, f: (f, 0, 0)),
                  pl.BlockSpec((experts_per_step, D_MODEL, D_EXPERT), lambda i, f: (f, 0, 0)),
                  pl.BlockSpec((experts_per_step, D_MODEL, D_EXPERT), lambda i, f: (f, 0, 0)),
                  pl.BlockSpec((tf, D_MODEL), lambda i, f: (f, 0)),
                  pl.BlockSpec((8, D_MODEL), lambda i, f: (0, 0))],
        out_specs=pl.BlockSpec((tm, D_MODEL), lambda i, f: (i, 0)),
        scratch_shapes=[pltpu.VMEM((tm, D_MODEL), F32)],
        compiler_params=pltpu.CompilerParams(
            dimension_semantics=("parallel", "arbitrary"), vmem_limit_bytes=VMEM_LIMIT),
        name="moe_norm",
    )(x, gate, pick, wg, wu, wd, ln)


TOK_PAD = 8
NEW_PAD = 16


def _softmax_two(s, sn):
    m = jnp.maximum(jnp.max(s, axis=-1, keepdims=True), jnp.max(sn, axis=-1, keepdims=True))
    p = jnp.exp(s - m)
    pn = jnp.exp(sn - m)
    inv = 1.0 / (jnp.sum(p, axis=-1, keepdims=True) + jnp.sum(pn, axis=-1, keepdims=True))
    return p * inv, pn * inv


def _new_row_mask(rows, n_new):
    t = lax.broadcasted_iota(jnp.int32, (rows, NEW_PAD), 0) & (TOK_PAD - 1)
    i = lax.broadcasted_iota(jnp.int32, (rows, NEW_PAD), 1)
    return (i <= t) & (i < n_new)


def _dec_diff_kernel(pt_ref, q_ref, new_ref, par_ref, *rest, n_pages, n_new, out_scale):
    pages = rest[:n_pages]
    o_ref, s_ref = rest[n_pages:]
    nqk = 2 * DIFF_HEADS * HEAD_DIM
    half = DIFF_HEADS * TOK_PAD
    q = q_ref[...]
    for j in range(n_pages):
        s_ref[:, j * PAGE_SIZE:(j + 1) * PAGE_SIZE] = _nt_dot(q, pages[j][:, 0:nqk].astype(BF16))
    sn = _nt_dot(q, new_ref[:, 0:nqk].astype(BF16))
    sn = jnp.where(_new_row_mask(2 * half, n_new), sn, NEG)
    p, pn = _softmax_two(s_ref[...], sn)
    lam = par_ref[0:1, 0:1]
    a = (p[0:half] - lam * p[half:2 * half]).astype(BF16)
    an = (pn[0:half] - lam * pn[half:2 * half]).astype(BF16)
    o = jnp.dot(an, new_ref[:, nqk:].astype(BF16), preferred_element_type=F32)
    for j in range(n_pages):
        o = o + jnp.dot(a[:, j * PAGE_SIZE:(j + 1) * PAGE_SIZE], pages[j][:, nqk:].astype(BF16),
                        preferred_element_type=F32)
    g = par_ref[1:2, :]
    for h in range(DIFF_HEADS):
        oh = o[h * TOK_PAD:(h + 1) * TOK_PAD, h * LANES:(h + 1) * LANES]
        oh = oh * lax.rsqrt(jnp.mean(oh * oh, axis=-1, keepdims=True) + RMS_EPS) * g
        o_ref[:, h * LANES:(h + 1) * LANES] = oh * out_scale


def _page_specs(layer, n_pages, width):
    return [pl.BlockSpec((None, None, PAGE_SIZE, width),
                         lambda b, pt, j=j: (layer, pt[b * n_pages + j], 0, 0)) for j in range(n_pages)]


def _per_seq(shape):
    return pl.BlockSpec((None,) + shape, lambda b, pt: (b,) + (0,) * len(shape))


def _shared(shape):
    return pl.BlockSpec(shape, lambda b, pt: (0,) * len(shape))


def _pad_rows(a, n):
    return jnp.pad(a, ((0, 0), (0, n - a.shape[1])) + ((0, 0),) * (a.ndim - 2))


def decode_diff_attention(layer, page_table, cache, da_q, rows_new, lam, subln_g, lambda_init):
    b, n_new, _ = da_q.shape
    n_pages = page_table.shape[1]
    q = da_q.astype(F32).reshape(b, n_new, DIFF_HEADS, 2, HEAD_DIM)
    q = _pad_rows(jnp.transpose(q, (0, 3, 2, 1, 4)).reshape(b * 2 * DIFF_HEADS, n_new, HEAD_DIM), TOK_PAD)
    q = q.reshape(b, 2, DIFF_HEADS, TOK_PAD, 1, HEAD_DIM)
    head = 2 * jnp.arange(DIFF_HEADS)[None, :] + jnp.arange(2)[:, None]
    place = (head[:, :, None] == jnp.arange(2 * DIFF_HEADS)).astype(F32)
    qbd = (q * place[None, :, :, None, :, None]).reshape(b, 2 * DIFF_HEADS * TOK_PAD, 2 * DIFF_HEADS * HEAD_DIM)
    par = jnp.concatenate([jnp.broadcast_to(lam.astype(F32), (1, LANES)), subln_g.astype(F32).reshape(1, LANES),
                           jnp.zeros((6, LANES), F32)], 0)
    rows = 2 * DIFF_HEADS * TOK_PAD
    width = cache.shape[-1]
    return dict(
        body=functools.partial(_dec_diff_kernel, n_pages=n_pages, n_new=n_new, out_scale=1.0 - lambda_init),
        args=[qbd.astype(BF16), _pad_rows(rows_new, NEW_PAD), par] + [cache] * n_pages,
        in_specs=[_per_seq((rows, 2 * DIFF_HEADS * HEAD_DIM)), _per_seq((NEW_PAD, width)), _shared((8, LANES))]
        + _page_specs(layer, n_pages, width),
        out_shape=jax.ShapeDtypeStruct((b, TOK_PAD, DIFF_HEADS * LANES), F32),
        out_spec=_per_seq((TOK_PAD, DIFF_HEADS * LANES)),
        scratch=[pltpu.VMEM((rows, n_pages * PAGE_SIZE), F32)])


def _dec_moba_kernel(pt_ref, q_ref, new_ref, *rest, n_pages, n_new, past_len):
    pages = rest[:n_pages]
    o_ref, s_ref, km_ref = rest[n_pages:]
    w = MOBA_HEADS * HEAD_DIM
    rows = MOBA_HEADS * TOK_PAD
    ppb = MOBA_BLOCK // PAGE_SIZE
    n_blk = n_pages // ppb
    q = q_ref[...]
    km_ref[...] = jnp.zeros(km_ref.shape, F32)
    for n in range(n_blk):
        tot = jnp.sum(pages[ppb * n][:, 0:w], axis=0, keepdims=True)
        for j in range(ppb * n + 1, ppb * (n + 1)):
            tot = tot + jnp.sum(pages[j][:, 0:w], axis=0, keepdims=True)
        km_ref[n:n + 1, :] = tot / MOBA_BLOCK
    gate = _nt_dot(q, km_ref[...].astype(BF16))
    lane = lax.broadcasted_iota(jnp.int32, gate.shape, 1)
    t = lax.broadcasted_iota(jnp.int32, gate.shape, 0) & (TOK_PAD - 1)
    earlier = (lane < (past_len + t) // MOBA_BLOCK) & (lane < n_blk)
    gate = jnp.where(earlier, gate, -jnp.inf)
    chosen = jnp.where(earlier & _rank_select(gate, n_blk, MOBA_TOPK), 1.0, 0.0)
    for j in range(n_pages):
        sj = _nt_dot(q, pages[j][:, 0:w].astype(BF16))
        n = j // ppb
        s_ref[:, j * PAGE_SIZE:(j + 1) * PAGE_SIZE] = jnp.where(chosen[:, n:n + 1] > 0.5, sj, NEG)
    sn = jnp.where(_new_row_mask(rows, n_new), _nt_dot(q, new_ref[:, 0:w].astype(BF16)), NEG)
    p, pn = _softmax_two(s_ref[...], sn)
    p = p.astype(BF16)
    o = jnp.dot(pn.astype(BF16), new_ref[:, w:].astype(BF16), preferred_element_type=F32)
    for j in range(n_pages):
        o = o + jnp.dot(p[:, j * PAGE_SIZE:(j + 1) * PAGE_SIZE], pages[j][:, w:].astype(BF16),
                        preferred_element_type=F32)
    for h in range(MOBA_HEADS):
        o_ref[:, h * HEAD_DIM:(h + 1) * HEAD_DIM] = o[h * TOK_PAD:(h + 1) * TOK_PAD, h * HEAD_DIM:(h + 1) * HEAD_DIM]


def _head_tiles(a, n_heads):
    b, n_new, _ = a.shape
    a = a.astype(F32).reshape(b, n_new, n_heads, HEAD_DIM)
    return _pad_rows(jnp.transpose(a, (0, 2, 1, 3)).reshape(b * n_heads, n_new, HEAD_DIM), TOK_PAD).reshape(
        b, n_heads, TOK_PAD, HEAD_DIM)


def decode_moba_attention(layer, page_table, cache, m_q, rows_new):
    b, n_new, _ = m_q.shape
    n_pages = page_table.shape[1]
    past_len = n_pages * PAGE_SIZE
    assert past_len % MOBA_BLOCK == 0 and n_new <= TOK_PAD and n_pages * PAGE_SIZE // MOBA_BLOCK <= NEW_PAD
    q = _head_tiles(m_q, MOBA_HEADS)
    place = jnp.eye(MOBA_HEADS, dtype=F32)
    q = (q[:, :, :, None, :] * place[None, :, None, :, None]).reshape(b, MOBA_HEADS * TOK_PAD, MOBA_HEADS * HEAD_DIM)
    rows = MOBA_HEADS * TOK_PAD
    width = cache.shape[-1]
    return dict(
        body=functools.partial(_dec_moba_kernel, n_pages=n_pages, n_new=n_new, past_len=past_len),
        args=[q.astype(BF16), _pad_rows(rows_new, NEW_PAD)] + [cache] * n_pages,
        in_specs=[_per_seq((rows, MOBA_HEADS * HEAD_DIM)), _per_seq((NEW_PAD, width))]
        + _page_specs(layer, n_pages, width),
        out_shape=jax.ShapeDtypeStruct((b, TOK_PAD, MOBA_HEADS * HEAD_DIM), F32),
        out_spec=_per_seq((TOK_PAD, MOBA_HEADS * HEAD_DIM)),
        scratch=[pltpu.VMEM((rows, past_len), F32), pltpu.VMEM((NEW_PAD, MOBA_HEADS * HEAD_DIM), F32)])


def _dec_nsa_kernel(pt_ref, qn_ref, qr_ref, gate_ref, new_ref, wnew_ref, win_ref,
                    wc_ref, cb_ref, w2_ref, ov_ref, ex_ref, *rest, n_pages, n_new, past_len, win_pos0):
    pages = rest[:n_pages]
    o_ref, cmp_ref, slc_ref = rest[n_pages:]
    rows = NSA_HEADS * TOK_PAD
    n_chunk = past_len // CMP_STRIDE
    n_cmp = (past_len + n_new - CMP_LEN) // CMP_STRIDE + 1
    n_slc = -(-(past_len + n_new) // SLC_LEN)
    hid = cb_ref.shape[1] // 2
    for j in range(n_pages):
        cmp_ref[j * PAGE_SIZE:(j + 1) * PAGE_SIZE, :] = pages[j][:, 0:2 * HEAD_DIM]
        slc_ref[j * PAGE_SIZE:(j + 1) * PAGE_SIZE, :] = pages[j][:, 2 * HEAD_DIM:4 * HEAD_DIM]

    y = jnp.zeros((n_chunk, 4 * hid), F32)
    for r in range(CMP_STRIDE):
        xr = cmp_ref[pl.ds(r, n_chunk, stride=CMP_STRIDE), :].astype(BF16)
        y = y + jnp.dot(xr, wc_ref[r], preferred_element_type=F32)
    hk = y[:, 0:hid] + pltpu.roll(y[:, hid:2 * hid], n_chunk - 1, 0) + cb_ref[0:1, 0:hid]
    hv = y[:, 2 * hid:3 * hid] + pltpu.roll(y[:, 3 * hid:4 * hid], n_chunk - 1, 0) + cb_ref[0:1, hid:2 * hid]
    cc = jnp.dot(jax.nn.gelu(jnp.concatenate([hk, hv], axis=1)).astype(BF16), w2_ref[...],
                 preferred_element_type=F32).astype(BF16)

    t = lax.broadcasted_iota(jnp.int32, (rows, n_chunk), 0) & (TOK_PAD - 1)
    n_idx = lax.broadcasted_iota(jnp.int32, (rows, n_chunk), 1)
    ok = (n_idx * CMP_STRIDE + CMP_LEN - 1 <= past_len + t) & (n_idx < n_cmp)
    sc = jnp.where(ok, _nt_dot(qn_ref[...], cc), NEG)
    pc = jnp.exp(sc - jnp.max(sc, axis=-1, keepdims=True))
    pc = pc / jnp.sum(pc, axis=-1, keepdims=True)
    p_hi = pc.astype(BF16)
    p_lo = (pc - p_hi.astype(F32)).astype(BF16)
    o_cmp = jnp.dot(p_hi, cc, preferred_element_type=F32)[:, HEAD_DIM:]
    ov = ov_ref[...]
    imp = jnp.dot(p_hi, ov, preferred_element_type=F32) + jnp.dot(p_lo, ov, preferred_element_type=F32)
    imp = imp[0:TOK_PAD] + imp[TOK_PAD:2 * TOK_PAD] + imp[2 * TOK_PAD:3 * TOK_PAD] + imp[3 * TOK_PAD:4 * TOK_PAD]
    blk = lax.broadcasted_iota(jnp.int32, imp.shape, 1)
    cur = (past_len + lax.broadcasted_iota(jnp.int32, imp.shape, 0)) // SLC_LEN
    valid = blk <= cur
    forced = (blk == 0) | (blk == cur) | (blk == cur - 1)
    score = jnp.where(forced, jnp.inf, jnp.where(valid, imp, -jnp.inf))
    chosen = jnp.where(valid & _rank_select(score, n_slc, min(SLC_TOPN, n_slc)), 1.0, 0.0).astype(BF16)
    key_ok = jnp.dot(chosen, ex_ref[...], preferred_element_type=F32)
    key_ok = jnp.concatenate([key_ok] * NSA_HEADS, axis=0)
    new_mask = _new_row_mask(rows, n_new)

    qr = qr_ref[...]
    kv = slc_ref[...].astype(BF16)
    kv_new = new_ref[:, 2 * HEAD_DIM:4 * HEAD_DIM].astype(BF16)
    p, pn = _softmax_two(jnp.where(key_ok > 0.5, _nt_dot(qr, kv), NEG), jnp.where(new_mask, _nt_dot(qr, kv_new), NEG))
    o_slc = (jnp.dot(p.astype(BF16), kv, preferred_element_type=F32)
             + jnp.dot(pn.astype(BF16), kv_new, preferred_element_type=F32))[:, HEAD_DIM:]

    kv = win_ref[...].astype(BF16)
    kv_new = wnew_ref[...].astype(BF16)
    n_win = kv.shape[0]
    wpos = win_pos0 + lax.broadcasted_iota(jnp.int32, (rows, n_win), 1)
    qpos = past_len + (lax.broadcasted_iota(jnp.int32, (rows, n_win), 0) & (TOK_PAD - 1))
    ok = (wpos <= qpos) & (wpos > qpos - WINDOW)
    p, pn = _softmax_two(jnp.where(ok, _nt_dot(qr, kv), NEG), jnp.where(new_mask, _nt_dot(qr, kv_new), NEG))
    o_win = (jnp.dot(p.astype(BF16), kv, preferred_element_type=F32)
             + jnp.dot(pn.astype(BF16), kv_new, preferred_element_type=F32))[:, HEAD_DIM:]

    g = jax.nn.sigmoid(gate_ref[...])
    o = g[:, 0:1] * o_cmp + g[:, 1:2] * o_slc + g[:, 2:3] * o_win
    for h in range(NSA_HEADS):
        o_ref[:, h * HEAD_DIM:(h + 1) * HEAD_DIM] = o[h * TOK_PAD:(h + 1) * TOK_PAD]


def decode_nsa_attention(layer, page_table, cache, win_state, n_q, nq_r, n_gate, rows_new, rows_win,
                         cmp_pos, cmp_w1, cmp_w2):
    b, n_new, _ = n_q.shape
    n_pages = page_table.shape[1]
    past_len = n_pages * PAGE_SIZE
    n_win = win_state.shape[2]
    assert past_len % SLC_LEN == 0 and n_new < CMP_STRIDE and past_len >= CMP_LEN and n_new <= TOK_PAD
    n_chunk = past_len // CMP_STRIDE
    assert n_chunk == LANES, "compressed tokens are laid out on one lane tile"
    hid = cmp_w1.shape[-1]
    lane_pad = lambda a: jnp.pad(a, ((0, 0),) * (a.ndim - 1) + ((0, LANES - a.shape[-1]),))
    qn = lane_pad(_head_tiles(n_q, NSA_HEADS)).reshape(b, -1, LANES).astype(BF16)
    qr = lane_pad(_head_tiles(nq_r, NSA_HEADS)).reshape(b, -1, LANES).astype(BF16)
    g = jnp.transpose(n_gate[..., :N_GATE].reshape(b, n_new, NSA_HEADS, 3), (0, 2, 1, 3))
    g = lane_pad(_pad_rows(g.reshape(b * NSA_HEADS, n_new, 3), TOK_PAD)).reshape(b, -1, LANES)
    w1 = cmp_w1.reshape(2, 2, CMP_STRIDE, HEAD_DIM, hid)
    zero = jnp.zeros((CMP_STRIDE, HEAD_DIM, 2 * hid), F32)
    top = jnp.concatenate([w1[0, 0], w1[0, 1], zero], axis=-1)
    bot = jnp.concatenate([zero, w1[1, 0], w1[1, 1]], axis=-1)
    wc = jnp.concatenate([top, bot], axis=1).astype(BF16)
    bias = jnp.concatenate([cmp_pos[0].reshape(1, -1) @ cmp_w1[0], cmp_pos[1].reshape(1, -1) @ cmp_w1[1]], -1)
    cb = jnp.concatenate([bias, jnp.zeros((7, 2 * hid), F32)], 0)
    zw = jnp.zeros((hid, HEAD_DIM), F32)
    w2 = jnp.concatenate([jnp.concatenate([cmp_w2[0], zw], 1), jnp.concatenate([zw, cmp_w2[1]], 1)], 0).astype(BF16)
    cmp_start = jnp.arange(n_chunk) * CMP_STRIDE
    slc_start = jnp.arange(LANES) * SLC_LEN
    ov = ((cmp_start[:, None] <= slc_start[None, :] + SLC_LEN - 1)
          & (cmp_start[:, None] + CMP_LEN - 1 >= slc_start[None, :])).astype(BF16)
    ex = (jnp.arange(past_len)[None, :] // SLC_LEN == jnp.arange(LANES)[:, None]).astype(BF16)
    rows = NSA_HEADS * TOK_PAD
    width = cache.shape[-1]
    return dict(
        body=functools.partial(_dec_nsa_kernel, n_pages=n_pages, n_new=n_new, past_len=past_len,
                               win_pos0=past_len - n_win),
        args=[qn, qr, g, _pad_rows(rows_new, NEW_PAD), _pad_rows(rows_win, NEW_PAD), win_state,
              wc, cb, w2, ov, ex] + [cache] * n_pages,
        in_specs=[_per_seq((rows, LANES)), _per_seq((rows, LANES)), _per_seq((rows, LANES)),
                  _per_seq((NEW_PAD, width)), _per_seq((NEW_PAD, 2 * HEAD_DIM)),
                  pl.BlockSpec((None, None, n_win, 2 * HEAD_DIM), lambda bi, pt: (layer, bi, 0, 0)),
                  _shared(wc.shape), _shared(cb.shape), _shared(w2.shape), _shared(ov.shape), _shared(ex.shape)]
        + _page_specs(layer, n_pages, width),
        out_shape=jax.ShapeDtypeStruct((b, TOK_PAD, NSA_HEADS * HEAD_DIM), F32),
        out_spec=_per_seq((TOK_PAD, NSA_HEADS * HEAD_DIM)),
        scratch=[pltpu.VMEM((past_len, 2 * HEAD_DIM), F32), pltpu.VMEM((past_len, 2 * HEAD_DIM), F32)])


def _decode_kernel(pt_ref, *refs, parts):
    pos = 0
    ins = []
    for _, n_in, _ in parts:
        ins.append(refs[pos:pos + n_in])
        pos += n_in
    outs = refs[pos:pos + len(parts)]
    pos += len(parts)
    for (body, _, n_scratch), part_ins, o_ref in zip(parts, ins, outs):
        body(pt_ref, *part_ins, o_ref, *refs[pos:pos + n_scratch])
        pos += n_scratch


def decode_attention(page_table, parts, n_new):
    b = page_table.shape[0]
    outs = pl.pallas_call(
        functools.partial(_decode_kernel, parts=[(p["body"], len(p["args"]), len(p["scratch"])) for p in parts]),
        out_shape=[p["out_shape"] for p in parts],
        grid_spec=pltpu.PrefetchScalarGridSpec(
            num_scalar_prefetch=1, grid=(b,),
            in_specs=[s for p in parts for s in p["in_specs"]],
            out_specs=[p["out_spec"] for p in parts],
            scratch_shapes=[s for p in parts for s in p["scratch"]]),
        compiler_params=pltpu.CompilerParams(dimension_semantics=("arbitrary",), vmem_limit_bytes=VMEM_LIMIT),
        name="decode_attention",
    )(page_table.reshape(-1), *[a for p in parts for a in p["args"]])
    return [o[:, :n_new] for o in outs]


def _compress(kv, pos_emb, w1, w2):
    b, l, _ = kv.shape
    n_chunk = l // CMP_STRIDE
    n_cmp = (l - CMP_LEN) // CMP_STRIDE + 1
    x = kv[:, :n_chunk * CMP_STRIDE].reshape(b * n_chunk, CMP_STRIDE * HEAD_DIM)
    half = CMP_STRIDE * HEAD_DIM
    w_cat = jnp.concatenate([w1[:half], w1[half:]], axis=1).astype(BF16)
    rows = x.shape[0]
    tm = 512 if rows % 512 == 0 else rows
    y = matmul(x, w_cat, tm, w_cat.shape[1]).reshape(b, n_chunk, 2, w1.shape[1])
    bias = pos_emb.reshape(1, -1) @ w1
    hid = y[:, :n_cmp, 0] + y[:, 1:n_cmp + 1, 1] + bias
    return jax.nn.gelu(hid) @ w2


def kernel(x_prompt, x_sample, cache_diff, cache_nsa, cache_moba, state_nsa_win, page_table, w_in, diff_lambda, diff_subln, nsa_cmp_pos, nsa_cmp_w1, nsa_cmp_w2, w_br_diff, w_br_nsa, w_br_moba, w_out, ln1_g, ln1_b, ln2_g, ln2_b, router_w, router_b, exp_w_gate, exp_w_up, exp_w_down, sh_w_gate, sh_w_up, sh_w_down):
    bp, sp, _ = x_prompt.shape
    bs, ss, _ = x_sample.shape
    tp, ts = bp * sp, bs * ss
    n_pages = page_table.shape[1]
    past_len = n_pages * PAGE_SIZE
    pos_p = jnp.arange(sp, dtype=jnp.int32)
    pos_s = past_len + jnp.arange(ss, dtype=jnp.int32)
    pos = jnp.concatenate([jnp.tile(pos_p, bp), jnp.tile(pos_s, bs)])
    inv = ROPE_THETA ** (-jnp.arange(HALF, dtype=F32) / HALF)
    ang = pos.astype(F32)[:, None] * inv[None, :]
    cos, sin = jnp.cos(ang), jnp.sin(ang)

    x = jnp.concatenate([x_prompt.reshape(tp, D_MODEL), x_sample.reshape(ts, D_MODEL)], 0)
    win_p, win_s = [], []
    cache_rows = None
    for l in range(DEPTH):
        lambda_init = 0.8 - 0.6 * math.exp(-0.3 * l)
        pj = project_inputs(x, repack_input_weight(w_in[l]), cos, sin, l, tp, cache_rows)
        cache_rows = pj["cache_rows"]
        rows_win = pj["rows_win"]
        rows_diff_p, rows_nsa_p, rows_moba_p = (a[l] for a in cache_rows[0::2])
        rows_diff_s, rows_nsa_s, rows_moba_s = (a[l].reshape(bs, ss, -1) for a in cache_rows[1::2])

        lp_ = diff_lambda[l].astype(F32)
        lam = jnp.exp(jnp.sum(lp_[0] * lp_[1])) - jnp.exp(jnp.sum(lp_[2] * lp_[3])) + lambda_init

        def pr(a):
            return a.reshape(bp, sp, a.shape[-1])

        o_diff_p = diff_attention(pj["q_diff"], pj["k_diff"], pj["v_diff"], bp, sp, lam, diff_subln[l], lambda_init)
        ck_c = _compress(pr(rows_nsa_p[:, 0:HEAD_DIM]), nsa_cmp_pos[l, 0], nsa_cmp_w1[l, 0], nsa_cmp_w2[l, 0])
        cv_c = _compress(pr(rows_nsa_p[:, HEAD_DIM:2 * HEAD_DIM]), nsa_cmp_pos[l, 1], nsa_cmp_w1[l, 1],
                         nsa_cmp_w2[l, 1])
        o_nsa_p = nsa_attention(pj["q_nsa"], pj["q_nsa_rot"], jnp.concatenate([ck_c, cv_c], -1).astype(BF16),
                                pj["kv_slc"], pj["kv_win"], pj["nsa_gate"], bp, sp)
        k_mean = jnp.mean(rows_moba_p[:, 0:MOBA_HEADS * HEAD_DIM].reshape(bp, sp // MOBA_BLOCK, MOBA_BLOCK, -1),
                          axis=2)
        o_moba_p = moba_attention(pj["q_moba"], pj["k_moba"], pj["v_moba"], k_mean, bp, sp)

        def sm(a):
            return a[tp:].reshape(bs, ss, a.shape[-1])

        o_diff_s, o_nsa_s, o_moba_s = decode_attention(page_table, [
            decode_diff_attention(l, page_table, cache_diff, sm(pj["q_diff"]), rows_diff_s,
                                  lam, diff_subln[l], lambda_init),
            decode_nsa_attention(l, page_table, cache_nsa, state_nsa_win, sm(pj["q_nsa"]),
                                 sm(pj["q_nsa_rot"]), sm(pj["nsa_gate"]), rows_nsa_s, sm(rows_win),
                                 nsa_cmp_pos[l], nsa_cmp_w1[l], nsa_cmp_w2[l]),
            decode_moba_attention(l, page_table, cache_moba, sm(pj["q_moba"]), rows_moba_s)], ss)

        x, expert_gate = merge_project_norm(
            x, (o_diff_p, o_diff_s.reshape(ts, -1)), (o_nsa_p, o_nsa_s.reshape(ts, -1)),
            (o_moba_p, o_moba_s.reshape(ts, -1)),
            pj["merge_gate"], w_br_diff[l].astype(BF16), w_br_nsa[l].astype(BF16),
            w_br_moba[l].astype(BF16), w_out[l].astype(BF16), ln1_g[l], ln1_b[l], router_w[l], router_b[l])

        def with_shared(w_exp, w_sh):
            w_sh = jnp.transpose(w_sh.reshape(D_MODEL, D_SHARED // D_EXPERT, D_EXPERT), (1, 0, 2))
            return jnp.concatenate([w_exp, w_sh], 0).astype(BF16)

        wd = jnp.concatenate([exp_w_down[l].reshape(-1, D_MODEL), sh_w_down[l]], 0).astype(BF16)
        x = moe_norm(x, expert_gate, with_shared(exp_w_gate[l], sh_w_gate[l]), with_shared(exp_w_up[l], sh_w_up[l]),
                     wd, ln2_g[l], ln2_b[l])

        n_keep = min(WINDOW, sp)
        win_p.append(pr(rows_win[:tp])[:, sp - n_keep:])
        full_win = jnp.concatenate([state_nsa_win[l], sm(rows_win)], 1)
        n_keep = min(WINDOW, past_len + ss)
        win_s.append(full_win[:, full_win.shape[1] - n_keep:])

    new_p = [a.reshape(DEPTH, bp, sp, a.shape[-1]) for a in cache_rows[0::2]]
    new_s = [a.reshape(DEPTH, bs, ss, a.shape[-1]) for a in cache_rows[1::2]]
    return (x[:tp].reshape(bp, sp, D_MODEL), x[tp:].reshape(bs, ss, D_MODEL),
            new_p[0], new_s[0], new_p[1], new_s[1], new_p[2], new_s[2], jnp.stack(win_p, 0), jnp.stack(win_s, 0))
```

```python
import functools
import math

import jax
import jax.numpy as jnp
from jax import lax
from jax.experimental import pallas as pl
from jax.experimental.pallas import tpu as pltpu

F32 = jnp.float32
BF16 = jnp.bfloat16

D_MODEL = 1024
DEPTH = 2
PAGE_SIZE = 128
HEAD_DIM = 64
HALF = HEAD_DIM // 2
ATTN_SCALE = HEAD_DIM ** -0.5
ROPE_THETA = 10000.0
DIFF_HEADS = 4
NSA_HEADS = 4
CMP_LEN = 32
CMP_STRIDE = 16
SLC_LEN = 64
SLC_TOPN = 16
WINDOW = 512
MOBA_HEADS = 4
MOBA_BLOCK = 256
MOBA_TOPK = 3
N_EXPERTS = 64
N_EXPERT_GROUPS = 8
TOPK_GROUPS = 4
TOP_K = 6
D_EXPERT = 128
D_SHARED = 256
ROUTED_SCALE = 2.5
LN_EPS = 1e-5
RMS_EPS = 1e-5
DEEPNORM_ALPHA = (2 * DEPTH) ** 0.25

C_DAQ, C_DAK, C_DAV, C_NQ = 0, 512, 1024, 1536
C_CK, C_CV, C_SK, C_SV, C_WK, C_WV, C_NG = 1792, 1856, 1920, 1984, 2048, 2112, 2176
N_GATE = 3 * NSA_HEADS
C_MQ_SRC = C_NG + N_GATE
C_MQ, C_MK, C_MV, C_MG = 2304, 2560, 2816, 3072
N_IN_PAD = C_MG + 3 * D_MODEL

LANES = 128
NEG = -1e30
VMEM_LIMIT = 56 * 1024 * 1024
MOE_EXPERTS_PER_STEP = 11

def _nt_dot(a, b):
    return lax.dot_general(a, b, (((1,), (1,)), ((), ())), preferred_element_type=F32)


def _mm_kernel(x_ref, w_ref, o_ref):
    o_ref[...] = jnp.dot(x_ref[...].astype(BF16), w_ref[...],
                         preferred_element_type=F32).astype(o_ref.dtype)


def matmul(x, w, tm, tn, out_dtype=F32):
    m, k = x.shape
    n = w.shape[1]
    assert m % tm == 0 and n % tn == 0, (x.shape, w.shape, tm, tn)
    return pl.pallas_call(
        _mm_kernel,
        out_shape=jax.ShapeDtypeStruct((m, n), out_dtype),
        grid=(m // tm, n // tn),
        in_specs=[pl.BlockSpec((tm, k), lambda i, j: (i, 0)),
                  pl.BlockSpec((k, tn), lambda i, j: (0, j))],
        out_specs=pl.BlockSpec((tm, tn), lambda i, j: (i, j)),
        compiler_params=pltpu.CompilerParams(
            dimension_semantics=("parallel", "arbitrary"), vmem_limit_bytes=VMEM_LIMIT),
        name="matmul",
    )(x, w)


def _repack_kernel(lo_ref, hi_ref, o_ref, *, first_moved_tile, shift):
    j = pl.program_id(0)
    lane = lax.broadcasted_iota(jnp.int32, lo_ref.shape, 1)
    hi = hi_ref[...]
    kept = jnp.where((j < first_moved_tile - 1) | (lane < LANES - shift), hi, 0.0)
    moved = jnp.where(lane < shift, pltpu.roll(lo_ref[...], shift, 1), pltpu.roll(hi, shift, 1))
    o_ref[...] = jnp.where(j < first_moved_tile, kept, moved).astype(o_ref.dtype)


def repack_input_weight(w):
    d, n_src = w.shape
    shift = C_MQ - C_MQ_SRC
    last = (n_src - 1) // LANES
    assert C_MQ % LANES == 0 and 0 < shift < LANES and n_src + shift == N_IN_PAD
    return pl.pallas_call(
        functools.partial(_repack_kernel, first_moved_tile=C_MQ // LANES, shift=shift),
        out_shape=jax.ShapeDtypeStruct((d, N_IN_PAD), BF16),
        grid=(N_IN_PAD // LANES,),
        in_specs=[pl.BlockSpec((d, LANES), lambda j: (0, jnp.maximum(j - 1, 0))),
                  pl.BlockSpec((d, LANES), lambda j: (0, jnp.minimum(j, last)))],
        out_specs=pl.BlockSpec((d, LANES), lambda j: (0, j)),
        compiler_params=pltpu.CompilerParams(dimension_semantics=("parallel",), vmem_limit_bytes=VMEM_LIMIT),
        name="repack_input_weight",
    )(w, w)


def _project_kernel(x_ref, w_ref, cos_ref, sa_ref, sb_ref, *refs, n_prompt_tiles, n_carry):
    (rdp_ref, rds_ref, rnp_ref, rns_ref, rmp_ref, rms_ref, rw_ref, mg_ref, ng_ref,
     qd_ref, kd_ref, vd_ref, nq_ref, nqr_ref, ss_ref, ww_ref, mq_ref, mk_ref, mv_ref, rows_ref) = refs[n_carry:]
    tile = pl.program_id(0)
    row_refs = ((rdp_ref, rds_ref), (rnp_ref, rns_ref), (rmp_ref, rms_ref))
    row_base = [0]
    for prompt_ref, _ in row_refs[:-1]:
        row_base.append(row_base[-1] + prompt_ref.shape[1])

    def put_rows(which, lo, hi, val):
        rows_ref[:, row_base[which] + lo:row_base[which] + hi] = val

    xb = x_ref[...].astype(BF16)
    cos, sa, sb = cos_ref[...], sa_ref[...], sb_ref[...]

    def seg(a, b):
        return jnp.dot(xb, w_ref[:, a:b], preferred_element_type=F32)

    def rope(y):
        tiles = []
        for t in range(y.shape[1] // LANES):
            yt = y[:, t * LANES:(t + 1) * LANES]
            tiles.append(yt * cos + pltpu.roll(yt, LANES - HALF, 1) * sa + pltpu.roll(yt, HALF, 1) * sb)
        return tiles[0] if len(tiles) == 1 else jnp.concatenate(tiles, axis=1)

    first = lax.broadcasted_iota(jnp.int32, (x_ref.shape[0], LANES), 1) < HEAD_DIM
    qd_ref[...] = (rope(seg(C_DAQ, C_DAK)) * ATTN_SCALE).astype(BF16)
    k = rope(seg(C_DAK, C_DAV))
    v = seg(C_DAV, C_NQ)
    put_rows(0, 0, C_DAV - C_DAK, k)
    put_rows(0, C_DAV - C_DAK, C_NQ - C_DAK, v)
    kd_ref[...] = k.astype(BF16)
    vd_ref[...] = v.astype(BF16)
    nq = seg(C_NQ, C_CK)
    nq_ref[...] = (nq * ATTN_SCALE).astype(BF16)
    nqr_ref[...] = (rope(nq) * ATTN_SCALE).astype(BF16)
    y = seg(C_CK, C_NG)
    ss = jnp.where(first, rope(y[:, LANES:2 * LANES]), y[:, LANES:2 * LANES])
    ww = jnp.where(first, rope(y[:, 2 * LANES:3 * LANES]), y[:, 2 * LANES:3 * LANES])
    put_rows(1, 0, LANES, y[:, 0:LANES])
    put_rows(1, LANES, 2 * LANES, ss)
    rw_ref[...] = ww
    ss_ref[...] = ss.astype(BF16)
    ww_ref[...] = ww.astype(BF16)
    ng_ref[...] = seg(C_NG, C_MQ)
    mq_ref[...] = (rope(seg(C_MQ, C_MK)) * ATTN_SCALE).astype(BF16)
    k = rope(seg(C_MK, C_MV))
    v = seg(C_MV, C_MG)
    put_rows(2, 0, C_MV - C_MK, k)
    put_rows(2, C_MV - C_MK, C_MG - C_MK, v)
    mk_ref[...] = k.astype(BF16)
    mv_ref[...] = v.astype(BF16)
    for t in range(3):
        mg_ref[:, t * D_MODEL:(t + 1) * D_MODEL] = seg(C_MG + t * D_MODEL, C_MG + (t + 1) * D_MODEL)

    @pl.when(tile < n_prompt_tiles)
    def _():
        for which, (prompt_ref, _) in enumerate(row_refs):
            prompt_ref[...] = rows_ref[:, row_base[which]:row_base[which] + prompt_ref.shape[1]]

    @pl.when(tile >= n_prompt_tiles)
    def _():
        for which, (_, sample_ref) in enumerate(row_refs):
            sample_ref[...] = rows_ref[:, row_base[which]:row_base[which] + sample_ref.shape[1]]


def project_inputs(x, w_pad, cos, sin, layer, n_prompt, cache_rows=None, tm=256):
    t = x.shape[0]
    n_sample = t - n_prompt
    assert n_prompt % tm == 0 and n_sample % tm == 0
    npt = n_prompt // tm
    zero = jnp.zeros_like(sin)
    cos_t = jnp.tile(cos, (1, LANES // HALF))
    sa = jnp.concatenate([-sin, zero, -sin, zero], axis=1)
    sb = jnp.concatenate([zero, sin, zero, sin], axis=1)
    row_widths = (1024, 256, 512)
    names_f32 = (("rows_win", 128), ("merge_gate", 3 * D_MODEL), ("nsa_gate", LANES))
    names_bf16 = (("q_diff", 512), ("k_diff", 512), ("v_diff", 512), ("q_nsa", 256), ("q_nsa_rot", 256),
                  ("kv_slc", 128), ("kv_win", 128), ("q_moba", 256), ("k_moba", 256), ("v_moba", 256))
    row = lambda i: (i, 0)
    rows_shapes, rows_specs = [], []
    for n in row_widths:
        rows_shapes += [jax.ShapeDtypeStruct((DEPTH, n_prompt, n), F32), jax.ShapeDtypeStruct((DEPTH, n_sample, n), F32)]
        rows_specs += [pl.BlockSpec((None, tm, n), lambda i: (layer, jnp.minimum(i, npt - 1), 0)),
                       pl.BlockSpec((None, tm, n), lambda i: (layer, jnp.maximum(i - npt, 0), 0))]
    carry = list(cache_rows) if cache_rows is not None else []
    n_fixed = 5
    outs = pl.pallas_call(
        functools.partial(_project_kernel, n_prompt_tiles=npt, n_carry=len(carry)),
        out_shape=rows_shapes + [jax.ShapeDtypeStruct((t, n), F32) for _, n in names_f32]
        + [jax.ShapeDtypeStruct((t, n), BF16) for _, n in names_bf16],
        grid=(t // tm,),
        in_specs=[pl.BlockSpec((tm, D_MODEL), row),
                  pl.BlockSpec(w_pad.shape, lambda i: (0, 0), pipeline_mode=pl.Buffered(1)),
                  pl.BlockSpec((tm, LANES), row), pl.BlockSpec((tm, LANES), row), pl.BlockSpec((tm, LANES), row)]
        + [pl.BlockSpec(memory_space=pl.ANY)] * len(carry),
        out_specs=rows_specs + [pl.BlockSpec((tm, n), row) for _, n in names_f32 + names_bf16],
        scratch_shapes=[pltpu.VMEM((tm, sum(row_widths)), F32)],
        input_output_aliases={n_fixed + k: k for k in range(len(carry))},
        compiler_params=pltpu.CompilerParams(dimension_semantics=("arbitrary",), vmem_limit_bytes=VMEM_LIMIT),
        name="project_inputs",
    )(x, w_pad, cos_t, sa, sb, *carry)
    res = dict(zip([n for n, _ in names_f32 + names_bf16], outs[len(rows_shapes):]))
    res["cache_rows"] = tuple(outs[:len(rows_shapes)])
    return res


def _online_update(s, v, m_ref, l_ref, acc_ref):
    tk = s.shape[1]
    dv = acc_ref.shape[-1]
    m_prev = m_ref[...]
    m_next = jnp.maximum(m_prev, jnp.max(s, axis=-1, keepdims=True))
    alpha = jnp.exp(m_prev - m_next)
    p = jnp.exp(s - jnp.concatenate([m_next] * (tk // LANES), axis=1))
    l_ref[...] = alpha * l_ref[...] + jnp.sum(p, axis=-1, keepdims=True)
    m_ref[...] = m_next
    acc_ref[...] = acc_ref[...] * alpha[:, :dv] + jnp.dot(p.astype(BF16), v, preferred_element_type=F32)


def _init_state(*refs):
    for m_ref, l_ref, acc_ref in zip(refs[0::3], refs[1::3], refs[2::3]):
        m_ref[...] = jnp.full(m_ref.shape, NEG, F32)
        l_ref[...] = jnp.zeros(l_ref.shape, F32)
        acc_ref[...] = jnp.zeros(acc_ref.shape, F32)


def _normalised(l_ref, acc_ref):
    dv = acc_ref.shape[-1]
    return acc_ref[...] / jnp.maximum(l_ref[...], 1e-30)[:, :dv]


def _diff_attn_kernel(q_ref, k_ref, v_ref, par_ref, o_ref,
                      m0, l0, a0, m1, l1, a1, *, tq, tk, out_scale):
    qi = pl.program_id(2)
    q = q_ref[...]
    lane = lax.broadcasted_iota(jnp.int32, q.shape, 1)
    zero = jnp.zeros_like(q)
    qa = jnp.where(lane < HEAD_DIM, q, zero)
    qb = jnp.where(lane >= HEAD_DIM, q, zero)
    q0 = qi * tq
    row = q0 + lax.broadcasted_iota(jnp.int32, (tq, tk), 0)
    col = lax.broadcasted_iota(jnp.int32, (tq, tk), 1)
    _init_state(m0, l0, a0, m1, l1, a1)

    def step(j, causal):
        start = pl.multiple_of(j * tk, tk)
        k = k_ref[pl.ds(start, tk), :]
        v = v_ref[pl.ds(start, tk), :]
        sa, sb = _nt_dot(qa, k), _nt_dot(qb, k)
        if causal:
            mask = (col + start) <= row
            sa, sb = jnp.where(mask, sa, NEG), jnp.where(mask, sb, NEG)
        _online_update(sa, v, m0, l0, a0)
        _online_update(sb, v, m1, l1, a1)

    n_full = (q0 + 1) // tk
    lax.fori_loop(0, n_full, lambda j, c: (step(j, False), c)[1], 0)
    lax.fori_loop(n_full, (q0 + tq + tk - 1) // tk, lambda j, c: (step(j, True), c)[1], 0)
    lam = par_ref[0:1, :]
    g = par_ref[1:2, :]
    o = _normalised(l0, a0) - lam * _normalised(l1, a1)
    o = o * lax.rsqrt(jnp.mean(o * o, axis=-1, keepdims=True) + RMS_EPS) * g
    o_ref[...] = (o * out_scale).astype(o_ref.dtype)


def diff_attention(q, k, v, n_batch, s, lam, subln_g, lambda_init, tq=512, tk=512):
    nq = s // tq
    par = jnp.concatenate([jnp.broadcast_to(lam.astype(F32), (1, LANES)),
                           subln_g.astype(F32).reshape(1, LANES),
                           jnp.zeros((6, LANES), F32)], 0)
    kern = functools.partial(_diff_attn_kernel, tq=tq, tk=tk, out_scale=1.0 - lambda_init)
    st = [pltpu.VMEM((tq, LANES), F32)] * 6
    return pl.pallas_call(
        kern,
        out_shape=jax.ShapeDtypeStruct((n_batch * s, DIFF_HEADS * LANES), F32),
        grid=(n_batch, DIFF_HEADS, nq),
        in_specs=[pl.BlockSpec((tq, LANES), lambda bi, h, i: (bi * nq + i, h)),
                  pl.BlockSpec((s, LANES), lambda bi, h, i: (bi, h)),
                  pl.BlockSpec((s, LANES), lambda bi, h, i: (bi, h)),
                  pl.BlockSpec((8, LANES), lambda bi, h, i: (0, 0))],
        out_specs=pl.BlockSpec((tq, LANES), lambda bi, h, i: (bi * nq + i, h)),
        scratch_shapes=st,
        compiler_params=pltpu.CompilerParams(
            dimension_semantics=("parallel", "parallel", "arbitrary"), vmem_limit_bytes=VMEM_LIMIT),
        name="diff_attention",
    )(q, k, v, par)


def _rank_select(score, n_candidates, top_n):
    lane = lax.broadcasted_iota(jnp.int32, score.shape, 1)
    rank = jnp.zeros(score.shape, F32)
    for c in range(n_candidates):
        col = score[:, c:c + 1]
        ahead = (col > score) | ((col == score) & (c < lane))
        rank = rank + jnp.where(ahead, 1.0, 0.0)
    return rank < top_n


def _rank_select_rows(score, n_candidates, top_n):
    idx = lax.broadcasted_iota(jnp.int32, score.shape, 0)
    rank = jnp.zeros(score.shape, F32)
    for c in range(n_candidates):
        cand = score[c:c + 1]
        ahead = (cand > score) | ((cand == score) & (c < idx))
        rank = rank + jnp.where(ahead, 1.0, 0.0)
    return rank < top_n


def _rows_to_lanes(x_t):
    n, q = x_t.shape
    if n < LANES:
        x_t = jnp.concatenate([x_t, jnp.zeros((LANES - n, q), F32)], axis=0)
    return jnp.concatenate([x_t[:, c * LANES:(c + 1) * LANES].T for c in range(q // LANES)], axis=0)


def _stack_heads(q_ref, tq):
    first = lax.broadcasted_iota(jnp.int32, (tq, LANES), 1) < HEAD_DIM
    tiles = []
    for t in range(NSA_HEADS // 2):
        pair = q_ref[:, t * LANES:(t + 1) * LANES].astype(F32)
        tiles.append(jnp.where(first, pair, 0.0))
        tiles.append(jnp.where(first, pltpu.roll(pair, HEAD_DIM, 1), 0.0))
    return jnp.concatenate(tiles, axis=0).astype(BF16)


def _nsa_attn_kernel(qn_ref, qr_ref, cc_ref, ss_ref, ww_ref, gate_ref, ov_ref, e_ref, o_ref, m, l, acc,
                     *, tq, tk, tkw, n_cmp, n_slc):
    qi = pl.program_id(1)
    q0 = qi * tq
    nh = NSA_HEADS
    r = nh * tq
    qn = _stack_heads(qn_ref, tq)
    qr = _stack_heads(qr_ref, tq)

    cc = cc_ref[...]
    n_pad = cc.shape[0]
    qpos = q0 + lax.broadcasted_iota(jnp.int32, (tq, n_pad), 0)
    qpos = jnp.concatenate([qpos] * nh, axis=0)
    n_idx = lax.broadcasted_iota(jnp.int32, (r, n_pad), 1)
    ok = (n_idx * CMP_STRIDE + CMP_LEN - 1 <= qpos) & (n_idx < n_cmp)
    sc = jnp.where(ok, _nt_dot(qn, cc), NEG)
    pc = jnp.where(ok, jnp.exp(sc - jnp.max(sc, axis=-1, keepdims=True)), 0.0)
    pc = pc / jnp.maximum(jnp.sum(pc, axis=-1, keepdims=True), 1e-30)
    p_hi = pc.astype(BF16)
    p_lo = (pc - p_hi.astype(F32)).astype(BF16)
    o_cmp = jnp.dot(p_hi, cc, preferred_element_type=F32)
    ovt = ov_ref[...]
    imp = _nt_dot(ovt, p_hi) + _nt_dot(ovt, p_lo)
    imp = imp[:, 0:tq] + imp[:, tq:2 * tq] + imp[:, 2 * tq:3 * tq] + imp[:, 3 * tq:4 * tq]
    imp = imp[0:-(-n_slc // 8) * 8]
    blk = lax.broadcasted_iota(jnp.int32, imp.shape, 0)
    cur = (q0 + lax.broadcasted_iota(jnp.int32, imp.shape, 1)) // SLC_LEN
    valid = blk <= cur
    forced = (blk == 0) | (blk == cur) | (blk == cur - 1)
    score = jnp.where(forced, jnp.inf, jnp.where(valid, imp, -jnp.inf))
    chosen = jnp.where(valid & _rank_select_rows(score, n_slc, min(SLC_TOPN, n_slc)), 1.0, 0.0)
    chosen = _rows_to_lanes(chosen).astype(BF16)
    chosen = jnp.concatenate([chosen] * nh, axis=0)

    def sweep(kv_ref, tile, lo, hi, mask_fn):
        _init_state(m, l, acc)

        def body(j, c):
            start = pl.multiple_of(j * tile, tile)
            kv = kv_ref[pl.ds(start, tile), :]
            _online_update(jnp.where(mask_fn(j, start), _nt_dot(qr, kv), NEG), kv, m, l, acc)
            return c

        lax.fori_loop(lo, hi, body, 0)
        return _normalised(l, acc)

    row = jnp.concatenate([q0 + lax.broadcasted_iota(jnp.int32, (tq, tk), 0)] * nh, axis=0)
    col = lax.broadcasted_iota(jnp.int32, (r, tk), 1)
    o_slc = sweep(ss_ref, tk, 0, (q0 + tq + tk - 1) // tk,
                  lambda j, start: ((col + start) <= row)
                  & (jnp.dot(chosen, e_ref[j], preferred_element_type=F32) > 0.5))
    row_w = jnp.concatenate([q0 + lax.broadcasted_iota(jnp.int32, (tq, tkw), 0)] * nh, axis=0)
    col_w = lax.broadcasted_iota(jnp.int32, (r, tkw), 1)
    o_win = sweep(ww_ref, tkw, jnp.maximum(q0 - WINDOW + 1, 0) // tkw, (q0 + tq + tkw - 1) // tkw,
                  lambda j, start: ((col_w + start) <= row_w) & ((col_w + start) > row_w - WINDOW))

    g = jax.nn.sigmoid(gate_ref[...])
    first = lax.broadcasted_iota(jnp.int32, (tq, LANES), 1) < HEAD_DIM
    mixed = []
    for h in range(nh):
        rows = slice(h * tq, (h + 1) * tq)
        mixed.append(g[:, 3 * h:3 * h + 1] * o_cmp[rows] + g[:, 3 * h + 1:3 * h + 2] * o_slc[rows]
                     + g[:, 3 * h + 2:3 * h + 3] * o_win[rows])
    for t in range(nh // 2):
        o_ref[:, t * LANES:(t + 1) * LANES] = jnp.where(first, pltpu.roll(mixed[2 * t], HEAD_DIM, 1), mixed[2 * t + 1])


def _block_expander(n_chunks, n_rows, tk, block_len):
    key = jnp.arange(n_chunks)[:, None, None] * tk + jnp.arange(tk)[None, None, :]
    return (key // block_len == jnp.arange(n_rows)[None, :, None]).astype(BF16)


def nsa_attention(qn, qr, cc, ss, ww, gate, n_batch, s, tq=256, tk=512, tkw=256):
    nq = s // tq
    n_cmp = cc.shape[1]
    n_pad = -(-n_cmp // LANES) * LANES
    cc = jnp.pad(cc, ((0, 0), (0, n_pad - n_cmp), (0, 0)))
    n_slc = s // SLC_LEN
    assert n_slc <= LANES
    cmp_start = jnp.arange(n_pad) * CMP_STRIDE
    slc_start = jnp.arange(LANES) * SLC_LEN
    ov = ((cmp_start[None, :] <= slc_start[:, None] + SLC_LEN - 1)
          & (cmp_start[None, :] + CMP_LEN - 1 >= slc_start[:, None])).astype(BF16)
    expander = _block_expander(s // tk, LANES, tk, SLC_LEN)
    r = NSA_HEADS * tq
    row = lambda bi, i: (bi * nq + i, 0)
    return pl.pallas_call(
        functools.partial(_nsa_attn_kernel, tq=tq, tk=tk, tkw=tkw, n_cmp=n_cmp, n_slc=n_slc),
        out_shape=jax.ShapeDtypeStruct((n_batch * s, NSA_HEADS * HEAD_DIM), F32),
        grid=(n_batch, nq),
        in_specs=[pl.BlockSpec((tq, NSA_HEADS * HEAD_DIM), row),
                  pl.BlockSpec((tq, NSA_HEADS * HEAD_DIM), row),
                  pl.BlockSpec((None, n_pad, LANES), lambda bi, i: (bi, 0, 0)),
                  pl.BlockSpec((s, LANES), lambda bi, i: (bi, 0)),
                  pl.BlockSpec((s, LANES), lambda bi, i: (bi, 0)),
                  pl.BlockSpec((tq, LANES), row),
                  pl.BlockSpec(ov.shape, lambda bi, i: (0, 0)),
                  pl.BlockSpec(expander.shape, lambda bi, i: (0, 0, 0))],
        out_specs=pl.BlockSpec((tq, NSA_HEADS * HEAD_DIM), row),
        scratch_shapes=[pltpu.VMEM((r, LANES), F32), pltpu.VMEM((r, LANES), F32), pltpu.VMEM((r, LANES), F32)],
        compiler_params=pltpu.CompilerParams(
            dimension_semantics=("parallel", "arbitrary"), vmem_limit_bytes=VMEM_LIMIT),
        name="nsa_attention",
    )(qn, qr, cc, ss, ww, gate, ov, expander)


def _moba_attn_kernel(q_ref, k_ref, v_ref, km_ref, e_ref, o_ref, m0, l0, a0, m1, l1, a1, *, tq, tk, n_blk):
    qi = pl.program_id(2)
    q = q_ref[...]
    lane = lax.broadcasted_iota(jnp.int32, q.shape, 1)
    zero = jnp.zeros_like(q)
    qa = jnp.where(lane < HEAD_DIM, q, zero)
    qb = jnp.where(lane >= HEAD_DIM, q, zero)
    q0 = qi * tq
    row = q0 + lax.broadcasted_iota(jnp.int32, (tq, tk), 0)
    col = lax.broadcasted_iota(jnp.int32, (tq, tk), 1)

    km = km_ref[...]
    blk = lax.broadcasted_iota(jnp.int32, (km.shape[0], tq), 0)
    cur = (q0 + lax.broadcasted_iota(jnp.int32, (km.shape[0], tq), 1)) // MOBA_BLOCK
    earlier = blk < cur

    def block_mask(qh):
        gate = jnp.where(earlier, _nt_dot(km, qh), -jnp.inf)
        chosen = (earlier & _rank_select_rows(gate, n_blk, min(MOBA_TOPK, n_blk))) | (blk == cur)
        return _rows_to_lanes(jnp.where(chosen, 1.0, 0.0)).astype(BF16)

    bm0 = block_mask(qa)
    bm1 = block_mask(qb)
    _init_state(m0, l0, a0, m1, l1, a1)

    def body(j, carry):
        start = pl.multiple_of(j * tk, tk)
        k = k_ref[pl.ds(start, tk), :]
        v = v_ref[pl.ds(start, tk), :]
        causal = (col + start) <= row
        e = e_ref[j]
        mask0 = causal & (jnp.dot(bm0, e, preferred_element_type=F32) > 0.5)
        mask1 = causal & (jnp.dot(bm1, e, preferred_element_type=F32) > 0.5)
        _online_update(jnp.where(mask0, _nt_dot(qa, k), NEG), v, m0, l0, a0)
        _online_update(jnp.where(mask1, _nt_dot(qb, k), NEG), v, m1, l1, a1)
        return carry

    lax.fori_loop(0, (q0 + tq + tk - 1) // tk, body, 0)
    o_ref[...] = jnp.where(lane < HEAD_DIM, _normalised(l0, a0), _normalised(l1, a1)).astype(o_ref.dtype)


def moba_attention(q, k, v, k_mean, n_batch, s, tq=512, tk=512):
    nq = s // tq
    n_blk = k_mean.shape[1]
    n_pad = -(-n_blk // 16) * 16
    km = jnp.pad(k_mean, ((0, 0), (0, n_pad - n_blk), (0, 0))).astype(BF16)
    expander = _block_expander(s // tk, LANES, tk, MOBA_BLOCK)
    st = [pltpu.VMEM((tq, LANES), F32)] * 6
    return pl.pallas_call(
        functools.partial(_moba_attn_kernel, tq=tq, tk=tk, n_blk=n_blk),
        out_shape=jax.ShapeDtypeStruct((n_batch * s, MOBA_HEADS * HEAD_DIM), F32),
        grid=(n_batch, MOBA_HEADS // 2, nq),
        in_specs=[pl.BlockSpec((tq, LANES), lambda bi, h, i: (bi * nq + i, h)),
                  pl.BlockSpec((s, LANES), lambda bi, h, i: (bi, h)),
                  pl.BlockSpec((s, LANES), lambda bi, h, i: (bi, h)),
                  pl.BlockSpec((None, n_pad, LANES), lambda bi, h, i: (bi, 0, h)),
                  pl.BlockSpec(expander.shape, lambda bi, h, i: (0, 0, 0))],
        out_specs=pl.BlockSpec((tq, LANES), lambda bi, h, i: (bi * nq + i, h)),
        scratch_shapes=st,
        compiler_params=pltpu.CompilerParams(
            dimension_semantics=("parallel", "parallel", "arbitrary"), vmem_limit_bytes=VMEM_LIMIT),
        name="moba_attention",
    )(q, k, v, km, expander)


def _layer_norm(z, g, b):
    mu = jnp.mean(z, axis=-1, keepdims=True)
    zc = z - mu
    var = jnp.mean(zc * zc, axis=-1, keepdims=True)
    return zc * lax.rsqrt(var + LN_EPS) * g + b


def _split_bf16(a):
    hi = a.astype(BF16)
    return hi, (a - hi.astype(F32)).astype(BF16)


def _route_experts(xn, rwh_ref, rwl_ref, rb_ref):
    tm = xn.shape[0]
    group = N_EXPERTS // N_EXPERT_GROUPS
    xh, xl = _split_bf16(xn)
    rwh = rwh_ref[...]
    logits = _nt_dot(rwh, xh) + _nt_dot(rwh, xl) + _nt_dot(rwl_ref[...], xh)
    s = jax.nn.sigmoid(logits)
    sb = s + jnp.concatenate([rb_ref[...]] * (tm // LANES), axis=1)
    grp = sb.reshape(N_EXPERT_GROUPS, group, tm)
    member = lax.broadcasted_iota(jnp.int32, grp.shape, 1).astype(F32)
    m1 = jnp.max(grp, axis=1, keepdims=True)
    first = jnp.min(jnp.where(grp == m1, member, float(group)), axis=1, keepdims=True)
    m2 = jnp.max(jnp.where(member == first, -jnp.inf, grp), axis=1, keepdims=True)
    group_ok = _rank_select_rows(m1 + m2, N_EXPERT_GROUPS, TOPK_GROUPS)
    cand = jnp.where(group_ok, grp, -jnp.inf).reshape(N_EXPERTS, tm)
    w = jnp.where(_rank_select_rows(cand, N_EXPERTS, TOP_K), s, 0.0)
    w = w / jnp.sum(w, axis=0, keepdims=True) * ROUTED_SCALE
    gate = _rows_to_lanes(w)
    hi = gate.astype(BF16).astype(F32)
    lane = lax.broadcasted_iota(jnp.int32, gate.shape, 1)
    return jnp.where(lane < N_EXPERTS, hi, pltpu.roll(gate - hi, N_EXPERTS, 1)).astype(BF16)


def _merge_kernel(x_ref, odp_ref, ods_ref, onp_ref, ons_ref, omp_ref, oms_ref, g0_ref, g1_ref, g2_ref,
                  wd_ref, wn_ref, wm_ref, wo_ref, ln_ref, rwh_ref, rwl_ref, rb_ref, o_ref, gate_ref,
                  *, n_prompt_tiles):
    from_prompt = pl.program_id(0) < n_prompt_tiles

    def branch(p_r, s_r, w_r, g_r):
        o = jnp.where(from_prompt, p_r[...], s_r[...])
        y = jnp.dot(o.astype(BF16), w_r[...], preferred_element_type=F32)
        return jax.nn.sigmoid(g_r[...]) * y

    merged = (branch(odp_ref, ods_ref, wd_ref, g0_ref) + branch(onp_ref, ons_ref, wn_ref, g1_ref)
              + branch(omp_ref, oms_ref, wm_ref, g2_ref))
    y = jnp.dot(merged.astype(BF16), wo_ref[...], preferred_element_type=F32)
    z = DEEPNORM_ALPHA * x_ref[...] + y
    xn = _layer_norm(z, ln_ref[0:1, :], ln_ref[1:2, :])
    o_ref[...] = xn
    gate_ref[...] = _route_experts(xn, rwh_ref, rwl_ref, rb_ref)


def merge_project_norm(x, o_diff, o_nsa, o_moba, merge_gate, w_d, w_n, w_m, w_o, ln_g, ln_b,
                       router_w, router_b, tm=256):
    assert 2 * N_EXPERTS == LANES
    t = x.shape[0]
    tp, ts = o_diff[0].shape[0], o_diff[1].shape[0]
    assert tp % tm == 0 and ts % tm == 0 and tp + ts == t
    npt = tp // tm
    ln = jnp.concatenate([ln_g.reshape(1, -1), ln_b.reshape(1, -1), jnp.zeros((6, D_MODEL), F32)], 0)
    rwh, rwl = _split_bf16(router_w.astype(F32).T)
    rb = jnp.broadcast_to(router_b.astype(F32)[:, None], (N_EXPERTS, LANES))
    row = lambda i: (i, 0)
    full = lambda i: (0, 0)
    pair_specs, pair_args = [], []
    for o_p, o_s in (o_diff, o_nsa, o_moba):
        pair_specs += [pl.BlockSpec((tm, o_p.shape[1]), lambda i: (jnp.minimum(i, npt - 1), 0)),
                       pl.BlockSpec((tm, o_s.shape[1]), lambda i: (jnp.maximum(i - npt, 0), 0))]
        pair_args += [o_p, o_s]
    return pl.pallas_call(
        functools.partial(_merge_kernel, n_prompt_tiles=npt),
        out_shape=[jax.ShapeDtypeStruct((t, D_MODEL), F32), jax.ShapeDtypeStruct((t, LANES), BF16)],
        grid=(t // tm,),
        in_specs=[pl.BlockSpec((tm, D_MODEL), row)] + pair_specs + [
                  pl.BlockSpec((tm, D_MODEL), lambda i: (i, 0)),
                  pl.BlockSpec((tm, D_MODEL), lambda i: (i, 1)),
                  pl.BlockSpec((tm, D_MODEL), lambda i: (i, 2)),
                  pl.BlockSpec(w_d.shape, full), pl.BlockSpec(w_n.shape, full),
                  pl.BlockSpec(w_m.shape, full), pl.BlockSpec(w_o.shape, full),
                  pl.BlockSpec((8, D_MODEL), full),
                  pl.BlockSpec(rwh.shape, full), pl.BlockSpec(rwl.shape, full), pl.BlockSpec(rb.shape, full)],
        out_specs=[pl.BlockSpec((tm, D_MODEL), row), pl.BlockSpec((tm, LANES), row)],
        compiler_params=pltpu.CompilerParams(
            dimension_semantics=("parallel",), vmem_limit_bytes=VMEM_LIMIT),
        name="merge_project_norm",
    )(x, *pair_args, merge_gate, merge_gate, merge_gate, w_d, w_n, w_m, w_o, ln, rwh, rwl, rb)


def _moe_kernel(x_ref, gate_ref, pick_ref, wg_ref, wu_ref, wd_ref, ln_ref, o_ref, acc_ref):
    f = pl.program_id(1)

    @pl.when(f == 0)
    def _():
        acc_ref[...] = jnp.zeros(acc_ref.shape, F32)

    n_e = wg_ref.shape[0]
    wg = jnp.concatenate([wg_ref[j] for j in range(n_e)], axis=1)
    wu = jnp.concatenate([wu_ref[j] for j in range(n_e)], axis=1)
    xb = x_ref[...].astype(BF16)
    hg = jnp.dot(xb, wg, preferred_element_type=F32)
    hu = jnp.dot(xb, wu, preferred_element_type=F32)
    w = jnp.dot(gate_ref[...], pick_ref[...], preferred_element_type=F32)
    lane = lax.broadcasted_iota(jnp.int32, w.shape, 1)
    w = jnp.where((f * n_e + lane >= N_EXPERTS) & (lane < n_e), 1.0, w)
    h = jax.nn.silu(hg) * hu
    h = jnp.concatenate([h[:, j * D_EXPERT:(j + 1) * D_EXPERT] * w[:, j:j + 1] for j in range(n_e)], axis=1)
    acc_ref[...] += jnp.dot(h.astype(BF16), wd_ref[...], preferred_element_type=F32)

    @pl.when(f == pl.num_programs(1) - 1)
    def _():
        z = DEEPNORM_ALPHA * x_ref[...] + acc_ref[...]
        o_ref[...] = _layer_norm(z, ln_ref[0:1, :], ln_ref[1:2, :])


def moe_norm(x, gate, wg, wu, wd, ln_g, ln_b, tm=768, experts_per_step=MOE_EXPERTS_PER_STEP):
    t = x.shape[0]
    f_tot = wd.shape[0]
    tf = experts_per_step * D_EXPERT
    n_col = gate.shape[1]
    n_steps = wg.shape[0] // experts_per_step
    assert wg.shape[0] % experts_per_step == 0 and t % tm == 0
    expert = jnp.arange(n_steps)[:, None, None] * experts_per_step + jnp.arange(LANES)[None, None, :]
    src = jnp.arange(n_col)[None, :, None] % N_EXPERTS
    pick = ((src == expert) & (jnp.arange(LANES)[None, None, :] < experts_per_step)).astype(BF16)
    ln = jnp.concatenate([ln_g.reshape(1, -1), ln_b.reshape(1, -1), jnp.zeros((6, D_MODEL), F32)], 0)
    return pl.pallas_call(
        _moe_kernel,
        out_shape=jax.ShapeDtypeStruct((t, D_MODEL), F32),
        grid=(t // tm, f_tot // tf),
        in_specs=[pl.BlockSpec((tm, D_MODEL), lambda i, f: (i, 0)),
                  pl.BlockSpec((tm, n_col), lambda i, f: (i, 0)),
                  pl.BlockSpec((None, n_col, LANES), lambda i, f: (f, 0, 0)),
                  pl.BlockSpec((experts_per_step, D_MODEL, D_EXPERT), lambda i, f: (f, 0, 0)),
                  pl.BlockSpec((experts_per_step, D_MODEL, D_EXPERT), lambda i, f: (f, 0, 0)),
                  pl.BlockSpec((tf, D_MODEL), lambda i, f: (f, 0)),
                  pl.BlockSpec((8, D_MODEL), lambda i, f: (0, 0))],
        out_specs=pl.BlockSpec((tm, D_MODEL), lambda i, f: (i, 0)),
        scratch_shapes=[pltpu.VMEM((tm, D_MODEL), F32)],
        compiler_params=pltpu.CompilerParams(
            dimension_semantics=("parallel", "arbitrary"), vmem_limit_bytes=VMEM_LIMIT),
        name="moe_norm",
    )(x, gate, pick, wg, wu, wd, ln)


TOK_PAD = 8
NEW_PAD = 16


def _softmax_two(s, sn):
    m = jnp.maximum(jnp.max(s, axis=-1, keepdims=True), jnp.max(sn, axis=-1, keepdims=True))
    p = jnp.exp(s - m)
    pn = jnp.exp(sn - m)
    inv = 1.0 / (jnp.sum(p, axis=-1, keepdims=True) + jnp.sum(pn, axis=-1, keepdims=True))
    return p * inv, pn * inv


def _new_row_mask(rows, n_new):
    t = lax.broadcasted_iota(jnp.int32, (rows, NEW_PAD), 0) & (TOK_PAD - 1)
    i = lax.broadcasted_iota(jnp.int32, (rows, NEW_PAD), 1)
    return (i <= t) & (i < n_new)


def _dec_diff_kernel(pt_ref, q_ref, new_ref, par_ref, *rest, n_pages, n_new, out_scale):
    pages = rest[:n_pages]
    o_ref, s_ref = rest[n_pages:]
    nqk = 2 * DIFF_HEADS * HEAD_DIM
    half = DIFF_HEADS * TOK_PAD
    q = q_ref[...]
    for j in range(n_pages):
        s_ref[:, j * PAGE_SIZE:(j + 1) * PAGE_SIZE] = _nt_dot(q, pages[j][:, 0:nqk].astype(BF16))
    sn = _nt_dot(q, new_ref[:, 0:nqk].astype(BF16))
    sn = jnp.where(_new_row_mask(2 * half, n_new), sn, NEG)
    p, pn = _softmax_two(s_ref[...], sn)
    lam = par_ref[0:1, 0:1]
    a = (p[0:half] - lam * p[half:2 * half]).astype(BF16)
    an = (pn[0:half] - lam * pn[half:2 * half]).astype(BF16)
    o = jnp.dot(an, new_ref[:, nqk:].astype(BF16), preferred_element_type=F32)
    for j in range(n_pages):
        o = o + jnp.dot(a[:, j * PAGE_SIZE:(j + 1) * PAGE_SIZE], pages[j][:, nqk:].astype(BF16),
                        preferred_element_type=F32)
    g = par_ref[1:2, :]
    for h in range(DIFF_HEADS):
        oh = o[h * TOK_PAD:(h + 1) * TOK_PAD, h * LANES:(h + 1) * LANES]
        oh = oh * lax.rsqrt(jnp.mean(oh * oh, axis=-1, keepdims=True) + RMS_EPS) * g
        o_ref[:, h * LANES:(h + 1) * LANES] = oh * out_scale


def _page_specs(layer, n_pages, width):
    return [pl.BlockSpec((None, None, PAGE_SIZE, width),
                         lambda b, pt, j=j: (layer, pt[b * n_pages + j], 0, 0)) for j in range(n_pages)]


def _per_seq(shape):
    return pl.BlockSpec((None,) + shape, lambda b, pt: (b,) + (0,) * len(shape))


def _shared(shape):
    return pl.BlockSpec(shape, lambda b, pt: (0,) * len(shape))


def _pad_rows(a, n):
    return jnp.pad(a, ((0, 0), (0, n - a.shape[1])) + ((0, 0),) * (a.ndim - 2))


def decode_diff_attention(layer, page_table, cache, da_q, rows_new, lam, subln_g, lambda_init):
    b, n_new, _ = da_q.shape
    n_pages = page_table.shape[1]
    q = da_q.astype(F32).reshape(b, n_new, DIFF_HEADS, 2, HEAD_DIM)
    q = _pad_rows(jnp.transpose(q, (0, 3, 2, 1, 4)).reshape(b * 2 * DIFF_HEADS, n_new, HEAD_DIM), TOK_PAD)
    q = q.reshape(b, 2, DIFF_HEADS, TOK_PAD, 1, HEAD_DIM)
    head = 2 * jnp.arange(DIFF_HEADS)[None, :] + jnp.arange(2)[:, None]
    place = (head[:, :, None] == jnp.arange(2 * DIFF_HEADS)).astype(F32)
    qbd = (q * place[None, :, :, None, :, None]).reshape(b, 2 * DIFF_HEADS * TOK_PAD, 2 * DIFF_HEADS * HEAD_DIM)
    par = jnp.concatenate([jnp.broadcast_to(lam.astype(F32), (1, LANES)), subln_g.astype(F32).reshape(1, LANES),
                           jnp.zeros((6, LANES), F32)], 0)
    rows = 2 * DIFF_HEADS * TOK_PAD
    width = cache.shape[-1]
    return dict(
        body=functools.partial(_dec_diff_kernel, n_pages=n_pages, n_new=n_new, out_scale=1.0 - lambda_init),
        args=[qbd.astype(BF16), _pad_rows(rows_new, NEW_PAD), par] + [cache] * n_pages,
        in_specs=[_per_seq((rows, 2 * DIFF_HEADS * HEAD_DIM)), _per_seq((NEW_PAD, width)), _shared((8, LANES))]
        + _page_specs(layer, n_pages, width),
        out_shape=jax.ShapeDtypeStruct((b, TOK_PAD, DIFF_HEADS * LANES), F32),
        out_spec=_per_seq((TOK_PAD, DIFF_HEADS * LANES)),
        scratch=[pltpu.VMEM((rows, n_pages * PAGE_SIZE), F32)])


def _dec_moba_kernel(pt_ref, q_ref, new_ref, *rest, n_pages, n_new, past_len):
    pages = rest[:n_pages]
    o_ref, s_ref, km_ref = rest[n_pages:]
    w = MOBA_HEADS * HEAD_DIM
    rows = MOBA_HEADS * TOK_PAD
    ppb = MOBA_BLOCK // PAGE_SIZE
    n_blk = n_pages // ppb
    q = q_ref[...]
    km_ref[...] = jnp.zeros(km_ref.shape, F32)
    for n in range(n_blk):
        tot = jnp.sum(pages[ppb * n][:, 0:w], axis=0, keepdims=True)
        for j in range(ppb * n + 1, ppb * (n + 1)):
            tot = tot + jnp.sum(pages[j][:, 0:w], axis=0, keepdims=True)
        km_ref[n:n + 1, :] = tot / MOBA_BLOCK
    gate = _nt_dot(q, km_ref[...].astype(BF16))
    lane = lax.broadcasted_iota(jnp.int32, gate.shape, 1)
    t = lax.broadcasted_iota(jnp.int32, gate.shape, 0) & (TOK_PAD - 1)
    earlier = (lane < (past_len + t) // MOBA_BLOCK) & (lane < n_blk)
    gate = jnp.where(earlier, gate, -jnp.inf)
    chosen = jnp.where(earlier & _rank_select(gate, n_blk, MOBA_TOPK), 1.0, 0.0)
    for j in range(n_pages):
        sj = _nt_dot(q, pages[j][:, 0:w].astype(BF16))
        n = j // ppb
        s_ref[:, j * PAGE_SIZE:(j + 1) * PAGE_SIZE] = jnp.where(chosen[:, n:n + 1] > 0.5, sj, NEG)
    sn = jnp.where(_new_row_mask(rows, n_new), _nt_dot(q, new_ref[:, 0:w].astype(BF16)), NEG)
    p, pn = _softmax_two(s_ref[...], sn)
    p = p.astype(BF16)
    o = jnp.dot(pn.astype(BF16), new_ref[:, w:].astype(BF16), preferred_element_type=F32)
    for j in range(n_pages):
        o = o + jnp.dot(p[:, j * PAGE_SIZE:(j + 1) * PAGE_SIZE], pages[j][:, w:].astype(BF16),
                        preferred_element_type=F32)
    for h in range(MOBA_HEADS):
        o_ref[:, h * HEAD_DIM:(h + 1) * HEAD_DIM] = o[h * TOK_PAD:(h + 1) * TOK_PAD, h * HEAD_DIM:(h + 1) * HEAD_DIM]


def _head_tiles(a, n_heads):
    b, n_new, _ = a.shape
    a = a.astype(F32).reshape(b, n_new, n_heads, HEAD_DIM)
    return _pad_rows(jnp.transpose(a, (0, 2, 1, 3)).reshape(b * n_heads, n_new, HEAD_DIM), TOK_PAD).reshape(
        b, n_heads, TOK_PAD, HEAD_DIM)


def decode_moba_attention(layer, page_table, cache, m_q, rows_new):
    b, n_new, _ = m_q.shape
    n_pages = page_table.shape[1]
    past_len = n_pages * PAGE_SIZE
    assert past_len % MOBA_BLOCK == 0 and n_new <= TOK_PAD and n_pages * PAGE_SIZE // MOBA_BLOCK <= NEW_PAD
    q = _head_tiles(m_q, MOBA_HEADS)
    place = jnp.eye(MOBA_HEADS, dtype=F32)
    q = (q[:, :, :, None, :] * place[None, :, None, :, None]).reshape(b, MOBA_HEADS * TOK_PAD, MOBA_HEADS * HEAD_DIM)
    rows = MOBA_HEADS * TOK_PAD
    width = cache.shape[-1]
    return dict(
        body=functools.partial(_dec_moba_kernel, n_pages=n_pages, n_new=n_new, past_len=past_len),
        args=[q.astype(BF16), _pad_rows(rows_new, NEW_PAD)] + [cache] * n_pages,
        in_specs=[_per_seq((rows, MOBA_HEADS * HEAD_DIM)), _per_seq((NEW_PAD, width))]
        + _page_specs(layer, n_pages, width),
        out_shape=jax.ShapeDtypeStruct((b, TOK_PAD, MOBA_HEADS * HEAD_DIM), F32),
        out_spec=_per_seq((TOK_PAD, MOBA_HEADS * HEAD_DIM)),
        scratch=[pltpu.VMEM((rows, past_len), F32), pltpu.VMEM((NEW_PAD, MOBA_HEADS * HEAD_DIM), F32)])


def _dec_nsa_kernel(pt_ref, qn_ref, qr_ref, gate_ref, new_ref, wnew_ref, win_ref,
                    wc_ref, cb_ref, w2_ref, ov_ref, ex_ref, *rest, n_pages, n_new, past_len, win_pos0):
    pages = rest[:n_pages]
    o_ref, cmp_ref, slc_ref = rest[n_pages:]
    rows = NSA_HEADS * TOK_PAD
    n_chunk = past_len // CMP_STRIDE
    n_cmp = (past_len + n_new - CMP_LEN) // CMP_STRIDE + 1
    n_slc = -(-(past_len + n_new) // SLC_LEN)
    hid = cb_ref.shape[1] // 2
    for j in range(n_pages):
        cmp_ref[j * PAGE_SIZE:(j + 1) * PAGE_SIZE, :] = pages[j][:, 0:2 * HEAD_DIM]
        slc_ref[j * PAGE_SIZE:(j + 1) * PAGE_SIZE, :] = pages[j][:, 2 * HEAD_DIM:4 * HEAD_DIM]

    y = jnp.zeros((n_chunk, 4 * hid), F32)
    for r in range(CMP_STRIDE):
        xr = cmp_ref[pl.ds(r, n_chunk, stride=CMP_STRIDE), :].astype(BF16)
        y = y + jnp.dot(xr, wc_ref[r], preferred_element_type=F32)
    hk = y[:, 0:hid] + pltpu.roll(y[:, hid:2 * hid], n_chunk - 1, 0) + cb_ref[0:1, 0:hid]
    hv = y[:, 2 * hid:3 * hid] + pltpu.roll(y[:, 3 * hid:4 * hid], n_chunk - 1, 0) + cb_ref[0:1, hid:2 * hid]
    cc = jnp.dot(jax.nn.gelu(jnp.concatenate([hk, hv], axis=1)).astype(BF16), w2_ref[...],
                 preferred_element_type=F32).astype(BF16)

    t = lax.broadcasted_iota(jnp.int32, (rows, n_chunk), 0) & (TOK_PAD - 1)
    n_idx = lax.broadcasted_iota(jnp.int32, (rows, n_chunk), 1)
    ok = (n_idx * CMP_STRIDE + CMP_LEN - 1 <= past_len + t) & (n_idx < n_cmp)
    sc = jnp.where(ok, _nt_dot(qn_ref[...], cc), NEG)
    pc = jnp.exp(sc - jnp.max(sc, axis=-1, keepdims=True))
    pc = pc / jnp.sum(pc, axis=-1, keepdims=True)
    p_hi = pc.astype(BF16)
    p_lo = (pc - p_hi.astype(F32)).astype(BF16)
    o_cmp = jnp.dot(p_hi, cc, preferred_element_type=F32)[:, HEAD_DIM:]
    ov = ov_ref[...]
    imp = jnp.dot(p_hi, ov, preferred_element_type=F32) + jnp.dot(p_lo, ov, preferred_element_type=F32)
    imp = imp[0:TOK_PAD] + imp[TOK_PAD:2 * TOK_PAD] + imp[2 * TOK_PAD:3 * TOK_PAD] + imp[3 * TOK_PAD:4 * TOK_PAD]
    blk = lax.broadcasted_iota(jnp.int32, imp.shape, 1)
    cur = (past_len + lax.broadcasted_iota(jnp.int32, imp.shape, 0)) // SLC_LEN
    valid = blk <= cur
    forced = (blk == 0) | (blk == cur) | (blk == cur - 1)
    score = jnp.where(forced, jnp.inf, jnp.where(valid, imp, -jnp.inf))
    chosen = jnp.where(valid & _rank_select(score, n_slc, min(SLC_TOPN, n_slc)), 1.0, 0.0).astype(BF16)
    key_ok = jnp.dot(chosen, ex_ref[...], preferred_element_type=F32)
    key_ok = jnp.concatenate([key_ok] * NSA_HEADS, axis=0)
    new_mask = _new_row_mask(rows, n_new)

    qr = qr_ref[...]
    kv = slc_ref[...].astype(BF16)
    kv_new = new_ref[:, 2 * HEAD_DIM:4 * HEAD_DIM].astype(BF16)
    p, pn = _softmax_two(jnp.where(key_ok > 0.5, _nt_dot(qr, kv), NEG), jnp.where(new_mask, _nt_dot(qr, kv_new), NEG))
    o_slc = (jnp.dot(p.astype(BF16), kv, preferred_element_type=F32)
             + jnp.dot(pn.astype(BF16), kv_new, preferred_element_type=F32))[:, HEAD_DIM:]

    kv = win_ref[...].astype(BF16)
    kv_new = wnew_ref[...].astype(BF16)
    n_win = kv.shape[0]
    wpos = win_pos0 + lax.broadcasted_iota(jnp.int32, (rows, n_win), 1)
    qpos = past_len + (lax.broadcasted_iota(jnp.int32, (rows, n_win), 0) & (TOK_PAD - 1))
    ok = (wpos <= qpos) & (wpos > qpos - WINDOW)
    p, pn = _softmax_two(jnp.where(ok, _nt_dot(qr, kv), NEG), jnp.where(new_mask, _nt_dot(qr, kv_new), NEG))
    o_win = (jnp.dot(p.astype(BF16), kv, preferred_element_type=F32)
             + jnp.dot(pn.astype(BF16), kv_new, preferred_element_type=F32))[:, HEAD_DIM:]

    g = jax.nn.sigmoid(gate_ref[...])
    o = g[:, 0:1] * o_cmp + g[:, 1:2] * o_slc + g[:, 2:3] * o_win
    for h in range(NSA_HEADS):
        o_ref[:, h * HEAD_DIM:(h + 1) * HEAD_DIM] = o[h * TOK_PAD:(h + 1) * TOK_PAD]


def decode_nsa_attention(layer, page_table, cache, win_state, n_q, nq_r, n_gate, rows_new, rows_win,
                         cmp_pos, cmp_w1, cmp_w2):
    b, n_new, _ = n_q.shape
    n_pages = page_table.shape[1]
    past_len = n_pages * PAGE_SIZE
    n_win = win_state.shape[2]
    assert past_len % SLC_LEN == 0 and n_new < CMP_STRIDE and past_len >= CMP_LEN and n_new <= TOK_PAD
    n_chunk = past_len // CMP_STRIDE
    assert n_chunk == LANES, "compressed tokens are laid out on one lane tile"
    hid = cmp_w1.shape[-1]
    lane_pad = lambda a: jnp.pad(a, ((0, 0),) * (a.ndim - 1) + ((0, LANES - a.shape[-1]),))
    qn = lane_pad(_head_tiles(n_q, NSA_HEADS)).reshape(b, -1, LANES).astype(BF16)
    qr = lane_pad(_head_tiles(nq_r, NSA_HEADS)).reshape(b, -1, LANES).astype(BF16)
    g = jnp.transpose(n_gate[..., :N_GATE].reshape(b, n_new, NSA_HEADS, 3), (0, 2, 1, 3))
    g = lane_pad(_pad_rows(g.reshape(b * NSA_HEADS, n_new, 3), TOK_PAD)).reshape(b, -1, LANES)
    w1 = cmp_w1.reshape(2, 2, CMP_STRIDE, HEAD_DIM, hid)
    zero = jnp.zeros((CMP_STRIDE, HEAD_DIM, 2 * hid), F32)
    top = jnp.concatenate([w1[0, 0], w1[0, 1], zero], axis=-1)
    bot = jnp.concatenate([zero, w1[1, 0], w1[1, 1]], axis=-1)
    wc = jnp.concatenate([top, bot], axis=1).astype(BF16)
    bias = jnp.concatenate([cmp_pos[0].reshape(1, -1) @ cmp_w1[0], cmp_pos[1].reshape(1, -1) @ cmp_w1[1]], -1)
    cb = jnp.concatenate([bias, jnp.zeros((7, 2 * hid), F32)], 0)
    zw = jnp.zeros((hid, HEAD_DIM), F32)
    w2 = jnp.concatenate([jnp.concatenate([cmp_w2[0], zw], 1), jnp.concatenate([zw, cmp_w2[1]], 1)], 0).astype(BF16)
    cmp_start = jnp.arange(n_chunk) * CMP_STRIDE
    slc_start = jnp.arange(LANES) * SLC_LEN
    ov = ((cmp_start[:, None] <= slc_start[None, :] + SLC_LEN - 1)
          & (cmp_start[:, None] + CMP_LEN - 1 >= slc_start[None, :])).astype(BF16)
    ex = (jnp.arange(past_len)[None, :] // SLC_LEN == jnp.arange(LANES)[:, None]).astype(BF16)
    rows = NSA_HEADS * TOK_PAD
    width = cache.shape[-1]
    return dict(
        body=functools.partial(_dec_nsa_kernel, n_pages=n_pages, n_new=n_new, past_len=past_len,
                               win_pos0=past_len - n_win),
        args=[qn, qr, g, _pad_rows(rows_new, NEW_PAD), _pad_rows(rows_win, NEW_PAD), win_state,
              wc, cb, w2, ov, ex] + [cache] * n_pages,
        in_specs=[_per_seq((rows, LANES)), _per_seq((rows, LANES)), _per_seq((rows, LANES)),
                  _per_seq((NEW_PAD, width)), _per_seq((NEW_PAD, 2 * HEAD_DIM)),
                  pl.BlockSpec((None, None, n_win, 2 * HEAD_DIM), lambda bi, pt: (layer, bi, 0, 0)),
                  _shared(wc.shape), _shared(cb.shape), _shared(w2.shape), _shared(ov.shape), _shared(ex.shape)]
        + _page_specs(layer, n_pages, width),
        out_shape=jax.ShapeDtypeStruct((b, TOK_PAD, NSA_HEADS * HEAD_DIM), F32),
        out_spec=_per_seq((TOK_PAD, NSA_HEADS * HEAD_DIM)),
        scratch=[pltpu.VMEM((past_len, 2 * HEAD_DIM), F32), pltpu.VMEM((past_len, 2 * HEAD_DIM), F32)])


def _decode_kernel(pt_ref, *refs, parts):
    pos = 0
    ins = []
    for _, n_in, _ in parts:
        ins.append(refs[pos:pos + n_in])
        pos += n_in
    outs = refs[pos:pos + len(parts)]
    pos += len(parts)
    for (body, _, n_scratch), part_ins, o_ref in zip(parts, ins, outs):
        body(pt_ref, *part_ins, o_ref, *refs[pos:pos + n_scratch])
        pos += n_scratch


def decode_attention(page_table, parts, n_new):
    b = page_table.shape[0]
    outs = pl.pallas_call(
        functools.partial(_decode_kernel, parts=[(p["body"], len(p["args"]), len(p["scratch"])) for p in parts]),
        out_shape=[p["out_shape"] for p in parts],
        grid_spec=pltpu.PrefetchScalarGridSpec(
            num_scalar_prefetch=1, grid=(b,),
            in_specs=[s for p in parts for s in p["in_specs"]],
            out_specs=[p["out_spec"] for p in parts],
            scratch_shapes=[s for p in parts for s in p["scratch"]]),
        compiler_params=pltpu.CompilerParams(dimension_semantics=("arbitrary",), vmem_limit_bytes=VMEM_LIMIT),
        name="decode_attention",
    )(page_table.reshape(-1), *[a for p in parts for a in p["args"]])
    return [o[:, :n_new] for o in outs]


def _compress(kv, pos_emb, w1, w2):
    b, l, _ = kv.shape
    n_chunk = l // CMP_STRIDE
    n_cmp = (l - CMP_LEN) // CMP_STRIDE + 1
    x = kv[:, :n_chunk * CMP_STRIDE].reshape(b * n_chunk, CMP_STRIDE * HEAD_DIM)
    half = CMP_STRIDE * HEAD_DIM
    w_cat = jnp.concatenate([w1[:half], w1[half:]], axis=1).astype(BF16)
    rows = x.shape[0]
    tm = 512 if rows % 512 == 0 else rows
    y = matmul(x, w_cat, tm, w_cat.shape[1]).reshape(b, n_chunk, 2, w1.shape[1])
    bias = pos_emb.reshape(1, -1) @ w1
    hid = y[:, :n_cmp, 0] + y[:, 1:n_cmp + 1, 1] + bias
    return jax.nn.gelu(hid) @ w2


def kernel(x_prompt, x_sample, cache_diff, cache_nsa, cache_moba, state_nsa_win, page_table, w_in, diff_lambda, diff_subln, nsa_cmp_pos, nsa_cmp_w1, nsa_cmp_w2, w_br_diff, w_br_nsa, w_br_moba, w_out, ln1_g, ln1_b, ln2_g, ln2_b, router_w, router_b, exp_w_gate, exp_w_up, exp_w_down, sh_w_gate, sh_w_up, sh_w_down):
    bp, sp, _ = x_prompt.shape
    bs, ss, _ = x_sample.shape
    tp, ts = bp * sp, bs * ss
    n_pages = page_table.shape[1]
    past_len = n_pages * PAGE_SIZE
    pos_p = jnp.arange(sp, dtype=jnp.int32)
    pos_s = past_len + jnp.arange(ss, dtype=jnp.int32)
    pos = jnp.concatenate([jnp.tile(pos_p, bp), jnp.tile(pos_s, bs)])
    inv = ROPE_THETA ** (-jnp.arange(HALF, dtype=F32) / HALF)
    ang = pos.astype(F32)[:, None] * inv[None, :]
    cos, sin = jnp.cos(ang), jnp.sin(ang)

    x = jnp.concatenate([x_prompt.reshape(tp, D_MODEL), x_sample.reshape(ts, D_MODEL)], 0)
    win_p, win_s = [], []
    cache_rows = None
    for l in range(DEPTH):
        lambda_init = 0.8 - 0.6 * math.exp(-0.3 * l)
        pj = project_inputs(x, repack_input_weight(w_in[l]), cos, sin, l, tp, cache_rows)
        cache_rows = pj["cache_rows"]
        rows_win = pj["rows_win"]
        rows_diff_p, rows_nsa_p, rows_moba_p = (a[l] for a in cache_rows[0::2])
        rows_diff_s, rows_nsa_s, rows_moba_s = (a[l].reshape(bs, ss, -1) for a in cache_rows[1::2])

        lp_ = diff_lambda[l].astype(F32)
        lam = jnp.exp(jnp.sum(lp_[0] * lp_[1])) - jnp.exp(jnp.sum(lp_[2] * lp_[3])) + lambda_init

        def pr(a):
            return a.reshape(bp, sp, a.shape[-1])

        o_diff_p = diff_attention(pj["q_diff"], pj["k_diff"], pj["v_diff"], bp, sp, lam, diff_subln[l], lambda_init)
        ck_c = _compress(pr(rows_nsa_p[:, 0:HEAD_DIM]), nsa_cmp_pos[l, 0], nsa_cmp_w1[l, 0], nsa_cmp_w2[l, 0])
        cv_c = _compress(pr(rows_nsa_p[:, HEAD_DIM:2 * HEAD_DIM]), nsa_cmp_pos[l, 1], nsa_cmp_w1[l, 1],
                         nsa_cmp_w2[l, 1])
        o_nsa_p = nsa_attention(pj["q_nsa"], pj["q_nsa_rot"], jnp.concatenate([ck_c, cv_c], -1).astype(BF16),
                                pj["kv_slc"], pj["kv_win"], pj["nsa_gate"], bp, sp)
        k_mean = jnp.mean(rows_moba_p[:, 0:MOBA_HEADS * HEAD_DIM].reshape(bp, sp // MOBA_BLOCK, MOBA_BLOCK, -1),
                          axis=2)
        o_moba_p = moba_attention(pj["q_moba"], pj["k_moba"], pj["v_moba"], k_mean, bp, sp)

        def sm(a):
            return a[tp:].reshape(bs, ss, a.shape[-1])

        o_diff_s, o_nsa_s, o_moba_s = decode_attention(page_table, [
            decode_diff_attention(l, page_table, cache_diff, sm(pj["q_diff"]), rows_diff_s,
                                  lam, diff_subln[l], lambda_init),
            decode_nsa_attention(l, page_table, cache_nsa, state_nsa_win, sm(pj["q_nsa"]),
                                 sm(pj["q_nsa_rot"]), sm(pj["nsa_gate"]), rows_nsa_s, sm(rows_win),
                                 nsa_cmp_pos[l], nsa_cmp_w1[l], nsa_cmp_w2[l]),
            decode_moba_attention(l, page_table, cache_moba, sm(pj["q_moba"]), rows_moba_s)], ss)

        x, expert_gate = merge_project_norm(
            x, (o_diff_p, o_diff_s.reshape(ts, -1)), (o_nsa_p, o_nsa_s.reshape(ts, -1)),
            (o_moba_p, o_moba_s.reshape(ts, -1)),
            pj["merge_gate"], w_br_diff[l].astype(BF16), w_br_nsa[l].astype(BF16),
            w_br_moba[l].astype(BF16), w_out[l].astype(BF16), ln1_g[l], ln1_b[l], router_w[l], router_b[l])

        def with_shared(w_exp, w_sh):
            w_sh = jnp.transpose(w_sh.reshape(D_MODEL, D_SHARED // D_EXPERT, D_EXPERT), (1, 0, 2))
            return jnp.concatenate([w_exp, w_sh], 0).astype(BF16)

        wd = jnp.concatenate([exp_w_down[l].reshape(-1, D_MODEL), sh_w_down[l]], 0).astype(BF16)
        x = moe_norm(x, expert_gate, with_shared(exp_w_gate[l], sh_w_gate[l]), with_shared(exp_w_up[l], sh_w_up[l]),
                     wd, ln2_g[l], ln2_b[l])

        n_keep = min(WINDOW, sp)
        win_p.append(pr(rows_win[:tp])[:, sp - n_keep:])
        full_win = jnp.concatenate([state_nsa_win[l], sm(rows_win)], 1)
        n_keep = min(WINDOW, past_len + ss)
        win_s.append(full_win[:, full_win.shape[1] - n_keep:])

    new_p = [a.reshape(DEPTH, bp, sp, a.shape[-1]) for a in cache_rows[0::2]]
    new_s = [a.reshape(DEPTH, bs, ss, a.shape[-1]) for a in cache_rows[1::2]]
    return (x[:tp].reshape(bp, sp, D_MODEL), x[tp:].reshape(bs, ss, D_MODEL),
            new_p[0], new_s[0], new_p[1], new_s[1], new_p[2], new_s[2], jnp.stack(win_p, 0), jnp.stack(win_s, 0))
```
